```python
import jax, jax.numpy as jnp
from jax import lax
import numpy as np

D_MODEL = 1024
BATCH = 2
SEQ = 8192
DEPTH = 1

CONV_WIDTH = D_MODEL
CONV_K = 3
RET_HEADS = 4
RET_QK_DIM = D_MODEL // RET_HEADS
RET_V_DIM = 2 * RET_QK_DIM
RET_QK = RET_HEADS * RET_QK_DIM
RET_V = RET_HEADS * RET_V_DIM
CHUNK = 128
ROPE_BASE = 10000.0
D_FF = -(-8 * D_MODEL // (3 * 256)) * 256
EPS = 1e-6
IN_SIZES = (CONV_WIDTH, CONV_WIDTH, CONV_WIDTH, RET_QK, RET_QK, RET_V, RET_V, D_MODEL, D_MODEL)
N_IN = sum(IN_SIZES)

kernel_name = "hybrid_shortconv_retention_gated_block"


def rmsnorm(x, g):
    xf = x.astype(jnp.float32)
    y = xf * lax.rsqrt(jnp.mean(xf * xf, axis=-1, keepdims=True) + EPS)
    return (y * g.astype(jnp.float32)).astype(x.dtype)


def rotary(t):
    s, d = t.shape[1], t.shape[-1]
    pos = jnp.arange(s, dtype=jnp.float32)
    freqs = ROPE_BASE ** (-jnp.arange(0, d, 2, dtype=jnp.float32) / d)
    ang = pos[:, None] * freqs[None, :]
    cos = jnp.cos(ang)[None, :, None, :]
    sin = jnp.sin(ang)[None, :, None, :]
    t1, t2 = t[..., : d // 2], t[..., d // 2:]
    return jnp.concatenate([t1 * cos - t2 * sin, t1 * sin + t2 * cos], axis=-1)


def retention_chunkwise(q, k, v, log_gamma, strict):
    b, h, s, dk = q.shape
    dv = v.shape[-1]
    n_chunks = s // CHUNK

    def chunks(t):
        return jnp.moveaxis(t.reshape(b, h, n_chunks, CHUNK, t.shape[-1]), 2, 0)

    idx = jnp.arange(CHUNK, dtype=jnp.float32)
    diff = idx[:, None] - idx[None, :]
    mask = (diff > 0) if strict else (diff >= 0)
    lg = log_gamma[:, None, None]
    intra_decay = jnp.where(mask[None], jnp.exp(jnp.maximum(diff, 0.0)[None] * lg), 0.0)
    xi = jnp.exp((idx + 1.0)[None, :] * log_gamma[:, None])[..., None]
    zeta = jnp.exp((CHUNK - 1.0 - idx)[None, :] * log_gamma[:, None])[..., None]
    chunk_decay = jnp.exp(CHUNK * log_gamma)[:, None, None]

    def step(state, qkv):
        qc, kc, vc = qkv
        scores = jnp.einsum('bhcd,bhmd->bhcm', qc, kc) * intra_decay
        inner = jnp.einsum('bhcm,bhme->bhce', scores, vc)
        cross = jnp.einsum('bhcd,bhde->bhce', qc, state) * xi
        state = chunk_decay * state + jnp.einsum('bhcd,bhce->bhde', kc * zeta, vc)
        return state, inner + cross

    state0 = jnp.zeros((b, h, dk, dv), jnp.float32)
    _, out = lax.scan(step, state0, (chunks(q), chunks(k), chunks(v)))
    return jnp.moveaxis(out, 0, 2).reshape(b, h, s, dv)


def setup_inputs(seed: int = 0) -> dict:
    key = jax.random.key(seed)
    ks = jax.random.split(key, 16)
    f32 = jnp.float32

    def normal(k, shape, scale):
        return jax.random.normal(k, shape, f32) * scale

    def gain(k, shape):
        return 1.0 + 0.05 * jax.random.normal(k, shape, f32)

    base_logit = jnp.log(2.0 ** (5.0 + jnp.arange(RET_HEADS, dtype=f32)) - 1.0)
    return {
        "x": jax.random.normal(ks[0], (BATCH, SEQ, D_MODEL), f32),
        "g_mix": gain(ks[1], (DEPTH, D_MODEL)),
        "w_in": normal(ks[2], (DEPTH, D_MODEL, N_IN), D_MODEL ** -0.5),
        "w_conv": normal(ks[3], (DEPTH, CONV_K, CONV_WIDTH), CONV_K ** -0.5),
        "dec_f": base_logit[None] + 0.01 * jax.random.normal(ks[4], (DEPTH, RET_HEADS), f32),
        "dec_b": base_logit[None] + 0.01 * jax.random.normal(ks[5], (DEPTH, RET_HEADS), f32),
        "g_ret": gain(ks[6], (DEPTH, RET_V)),
        "w_a_out": normal(ks[7], (DEPTH, CONV_WIDTH, D_MODEL), CONV_WIDTH ** -0.5),
        "w_r_out": normal(ks[8], (DEPTH, RET_V, D_MODEL), RET_V ** -0.5),
        "w_o": normal(ks[9], (DEPTH, D_MODEL, D_MODEL), D_MODEL ** -0.5),
        "g_ffn": gain(ks[10], (DEPTH, D_MODEL)),
        "w_ff_gate": normal(ks[11], (DEPTH, D_MODEL, D_FF), D_MODEL ** -0.5),
        "w_ff_up": normal(ks[12], (DEPTH, D_MODEL, D_FF), D_MODEL ** -0.5),
        "w_ff_down": normal(ks[13], (DEPTH, D_FF, D_MODEL), D_FF ** -0.5),
        "g_final": gain(ks[14], (D_MODEL,)),
    }


def reference(x, g_mix, w_in, w_conv, dec_f, dec_b, g_ret, w_a_out, w_r_out, w_o,
              g_ffn, w_ff_gate, w_ff_up, w_ff_down, g_final):
    b, s, _ = x.shape
    dt = x.dtype
    split_points = [int(p) for p in np.cumsum(IN_SIZES)[:-1]]
    for l in range(DEPTH):
        h = rmsnorm(x, g_mix[l])
        proj = h @ w_in[l]
        xc, gb, gc, q, k, v, g_sw, gate_a, gate_r = jnp.split(proj, split_points, axis=-1)

        u = gc * xc
        u_pad = jnp.pad(u, ((0, 0), (CONV_K // 2, CONV_K // 2), (0, 0)))
        conv = u_pad[:, 0:s] * w_conv[l, 0]
        for tap in range(1, CONV_K):
            conv = conv + u_pad[:, tap:tap + s] * w_conv[l, tap]
        y_a = (gb * conv) @ w_a_out[l]

        qf = rotary(q.reshape(b, s, RET_HEADS, RET_QK_DIM).astype(jnp.float32)) * (RET_QK_DIM ** -0.5)
        kf = rotary(k.reshape(b, s, RET_HEADS, RET_QK_DIM).astype(jnp.float32))
        vf = v.reshape(b, s, RET_HEADS, RET_V_DIM).astype(jnp.float32)
        qf, kf, vf = (jnp.transpose(t, (0, 2, 1, 3)) for t in (qf, kf, vf))
        lg_f = jax.nn.log_sigmoid(dec_f[l].astype(jnp.float32))
        lg_b = jax.nn.log_sigmoid(dec_b[l].astype(jnp.float32))
        ret_f = retention_chunkwise(qf, kf, vf, lg_f, strict=False)
        ret_b = jnp.flip(retention_chunkwise(jnp.flip(qf, 2), jnp.flip(kf, 2), jnp.flip(vf, 2),
                                             lg_b, strict=True), 2)
        ret = ret_f + ret_b
        mu = jnp.mean(ret, axis=-1, keepdims=True)
        var = jnp.mean(jnp.square(ret - mu), axis=-1, keepdims=True)
        ret = (ret - mu) * lax.rsqrt(var + EPS)
        ret = jnp.transpose(ret, (0, 2, 1, 3)).reshape(b, s, RET_V) * g_ret[l].astype(jnp.float32)
        y_r = (ret.astype(dt) * jax.nn.silu(g_sw)) @ w_r_out[l]

        merged = jax.nn.sigmoid(gate_a) * y_a + jax.nn.sigmoid(gate_r) * y_r
        x = x + merged @ w_o[l]

        h2 = rmsnorm(x, g_ffn[l])
        x = x + (jax.nn.silu(h2 @ w_ff_gate[l]) * (h2 @ w_ff_up[l])) @ w_ff_down[l]
    return rmsnorm(x, g_final)
```

```python
import functools

import jax
import jax.numpy as jnp
from jax import lax
from jax.experimental import pallas as pl
from jax.experimental.pallas import tpu as pltpu

F32 = jnp.float32
BF16 = jnp.bfloat16

D_MODEL = 1024
RET_HEADS = 4
QK_DIM = D_MODEL // RET_HEADS
V_DIM = 2 * QK_DIM
RET_V = RET_HEADS * V_DIM
ROPE_BASE = 10000.0
EPS = 1e-6
CONV_K = 3

SEC = 1024
SEC_XC, SEC_GB, SEC_GC, SEC_Q, SEC_K, SEC_V, SEC_GSW, SEC_GA, SEC_GR = 0, 1, 2, 3, 4, 5, 7, 9, 10
N_SEC = 11

RET_CHUNK = 256
BF16_SUBLANES = 16

VMEM_LIMIT = 56 * 1024 * 1024


def _sigmoid(x):
    return 1.0 / (1.0 + jnp.exp(-x))


def _rms_scale(x):
    return x * lax.rsqrt(jnp.mean(x * x, axis=-1, keepdims=True) + EPS)


def _inproj_kernel(x_ref, g_ref, w_ref, cos_ref, sin_ref, o_ref, h_ref):
    j = pl.program_id(1)

    @pl.when(j == 0)
    def _():
        h_ref[...] = (_rms_scale(x_ref[...]) * g_ref[...]).astype(BF16)

    acc = jnp.dot(h_ref[...], w_ref[...], preferred_element_type=F32)

    def rotary(scale):
        cos = cos_ref[...]
        sin = sin_ref[...]
        half = QK_DIM // 2
        for hh in range(RET_HEADS):
            lo = hh * QK_DIM
            t1 = acc[:, lo:lo + half]
            t2 = acc[:, lo + half:lo + QK_DIM]
            o_ref[:, lo:lo + half] = ((t1 * cos - t2 * sin) * scale).astype(BF16)
            o_ref[:, lo + half:lo + QK_DIM] = ((t1 * sin + t2 * cos) * scale).astype(BF16)

    is_q = j == SEC_Q
    is_k = j == SEC_K
    is_silu = jnp.logical_or(j == SEC_GSW, j == SEC_GSW + 1)
    is_sig = j >= SEC_GA
    is_plain = jnp.logical_not(is_q | is_k | is_silu | is_sig)

    @pl.when(is_plain)
    def _():
        o_ref[...] = acc.astype(BF16)

    @pl.when(is_q)
    def _():
        rotary(QK_DIM ** -0.5)

    @pl.when(is_k)
    def _():
        rotary(1.0)

    @pl.when(is_silu)
    def _():
        o_ref[...] = (acc * _sigmoid(acc)).astype(BF16)

    @pl.when(is_sig)
    def _():
        o_ref[...] = _sigmoid(acc).astype(BF16)


def _inproj(x2, g_mix, w_in, cos, sin, tm):
    t, d = x2.shape
    s = cos.shape[0]
    n_pos_blocks = s // tm
    return pl.pallas_call(
        _inproj_kernel,
        grid=(t // tm, N_SEC),
        in_specs=[
            pl.BlockSpec((tm, d), lambda i, j: (i, 0)),
            pl.BlockSpec((1, d), lambda i, j: (0, 0)),
            pl.BlockSpec((d, SEC), lambda i, j: (0, j)),
            pl.BlockSpec((tm, QK_DIM // 2), lambda i, j: (i % n_pos_blocks, 0)),
            pl.BlockSpec((tm, QK_DIM // 2), lambda i, j: (i % n_pos_blocks, 0)),
        ],
        out_specs=pl.BlockSpec((tm, SEC), lambda i, j: (i, j)),
        out_shape=jax.ShapeDtypeStruct((t, N_SEC * SEC), BF16),
        scratch_shapes=[pltpu.VMEM((tm, d), BF16)],
        compiler_params=pltpu.CompilerParams(
            dimension_semantics=("arbitrary", "arbitrary"),
            vmem_limit_bytes=VMEM_LIMIT),
        name="inproj",
    )(x2, g_mix, w_in, cos, sin)


def _retention_kernel(lg_ref, q_ref, k_ref, v_ref, gsw_ref, gret_ref, z_ref,
                      decay_ref, xif_ref, xib_ref, zetaf_ref, zetab_ref,
                      state_ref, stateb16_ref, rall_ref, *, n_chunks):
    c = RET_CHUNK
    h = pl.program_id(1)
    t = pl.program_id(2)
    lg_f = lg_ref[0, h]
    lg_b = lg_ref[1, h]
    lanes = 128

    @pl.when(t == 0)
    def _():
        row = lax.broadcasted_iota(jnp.int32, (c, c), 0).astype(F32)
        col = lax.broadcasted_iota(jnp.int32, (c, c), 1).astype(F32)
        diff = row - col
        decay_ref[...] = jnp.where(diff >= 0.0,
                                   jnp.exp(jnp.maximum(diff, 0.0) * lg_f),
                                   jnp.exp(jnp.maximum(-diff, 0.0) * lg_b))
        idx = lax.broadcasted_iota(jnp.int32, (c, lanes), 0).astype(F32)
        xif_ref[...] = jnp.exp((idx + 1.0) * lg_f)
        zetaf_ref[...] = jnp.exp((c - 1.0 - idx) * lg_f)
        xib_ref[...] = jnp.exp((c - idx) * lg_b)
        zetab_ref[...] = jnp.exp(idx * lg_b)
        state_ref[...] = jnp.zeros_like(state_ref)

    def scaled_k(zeta_ref):
        zeta = zeta_ref[...]
        k = k_ref[...].astype(F32)
        return jnp.concatenate(
            [(k[:, n * lanes:(n + 1) * lanes] * zeta).astype(BF16)
             for n in range(QK_DIM // lanes)], axis=1)

    def state_update(zeta_ref, lg):
        chunk_decay = jnp.exp(jnp.zeros((1, V_DIM), F32) + c * lg)
        kv = lax.dot_general(scaled_k(zeta_ref), v_ref[...],
                             (((0,), (0,)), ((), ())), preferred_element_type=F32)
        state_ref[...] = chunk_decay * state_ref[...] + kv

    @pl.when(t < n_chunks)
    def _():
        rall_ref[n_chunks - 1 - t] = state_ref[...].astype(BF16)
        state_update(zetab_ref, lg_b)

    @pl.when(t == n_chunks)
    def _():
        state_ref[...] = jnp.zeros_like(state_ref)

    @pl.when(t >= n_chunks)
    def _():
        q = q_ref[...]
        v = v_ref[...]
        scores = lax.dot_general(q, k_ref[...], (((1,), (1,)), ((), ())),
                                 preferred_element_type=F32)
        p = (scores * decay_ref[...]).astype(BF16)
        inner = jnp.dot(p, v, preferred_element_type=F32)
        stateb16_ref[...] = state_ref[...].astype(BF16)
        cross_f = jnp.dot(q, stateb16_ref[...], preferred_element_type=F32)
        cross_b = jnp.dot(q, rall_ref[t - n_chunks], preferred_element_type=F32)
        xif = xif_ref[...]
        xib = xib_ref[...]
        ret = jnp.concatenate(
            [inner[:, n * lanes:(n + 1) * lanes]
             + cross_f[:, n * lanes:(n + 1) * lanes] * xif
             + cross_b[:, n * lanes:(n + 1) * lanes] * xib
             for n in range(V_DIM // lanes)], axis=1)
        mu = jnp.mean(ret, axis=-1, keepdims=True)
        dev = ret - mu
        var = jnp.mean(dev * dev, axis=-1, keepdims=True)
        normed = dev * lax.rsqrt(var + EPS) * gret_ref[...]
        z_ref[...] = (normed * gsw_ref[...].astype(F32)).astype(BF16)
        state_update(zetaf_ref, lg_f)


def _retention(lg, proj, g_ret, batch, seq):
    c = RET_CHUNK
    nc = seq // c
    t = batch * seq
    qb = (SEC_Q * SEC) // QK_DIM
    kb = (SEC_K * SEC) // QK_DIM
    vb = (SEC_V * SEC) // V_DIM
    gb = (SEC_GSW * SEC) // V_DIM

    def fwd_chunk(tt):
        return jnp.maximum(tt - nc, 0)

    def kv_chunk(tt):
        return jnp.where(tt < nc, nc - 1 - tt, tt - nc)

    lanes = 128
    return pl.pallas_call(
        functools.partial(_retention_kernel, n_chunks=nc),
        grid=(batch, RET_HEADS, 2 * nc),
        in_specs=[
            pl.BlockSpec(memory_space=pltpu.SMEM),
            pl.BlockSpec((c, QK_DIM), lambda b, h, tt: (b * nc + fwd_chunk(tt), qb + h)),
            pl.BlockSpec((c, QK_DIM), lambda b, h, tt: (b * nc + kv_chunk(tt), kb + h)),
            pl.BlockSpec((c, V_DIM), lambda b, h, tt: (b * nc + kv_chunk(tt), vb + h)),
            pl.BlockSpec((c, V_DIM), lambda b, h, tt: (b * nc + fwd_chunk(tt), gb + h)),
            pl.BlockSpec((1, V_DIM), lambda b, h, tt: (0, h)),
        ],
        out_specs=pl.BlockSpec((c, V_DIM), lambda b, h, tt: (b * nc + fwd_chunk(tt), h)),
        out_shape=jax.ShapeDtypeStruct((t, RET_V), BF16),
        scratch_shapes=[
            pltpu.VMEM((c, c), F32),
            pltpu.VMEM((c, lanes), F32),
            pltpu.VMEM((c, lanes), F32),
            pltpu.VMEM((c, lanes), F32),
            pltpu.VMEM((c, lanes), F32),
            pltpu.VMEM((QK_DIM, V_DIM), F32),
            pltpu.VMEM((QK_DIM, V_DIM), BF16),
            pltpu.VMEM((nc, QK_DIM, V_DIM), BF16),
        ],
        compiler_params=pltpu.CompilerParams(
            dimension_semantics=("arbitrary", "arbitrary", "arbitrary"),
            vmem_limit_bytes=VMEM_LIMIT),
        name="retention",
    )(lg, proj, proj, proj, proj, g_ret)


def _merge_kernel(x_ref, xc_ref, gb_ref, gc_ref, xcp_ref, gcp_ref, xcn_ref, gcn_ref,
                  ga_ref, gr_ref, z_ref, wconv_ref, wa_ref, wr_ref, wo_ref, o_ref,
                  *, tiles_per_seq):
    i = pl.program_id(0)
    tm = x_ref.shape[0]
    u = gc_ref[...].astype(F32) * xc_ref[...].astype(F32)
    last = BF16_SUBLANES - 1
    u_before = (gcp_ref[last:last + 1, :].astype(F32) * xcp_ref[last:last + 1, :].astype(F32))
    u_after = gcn_ref[0:1, :].astype(F32) * xcn_ref[0:1, :].astype(F32)
    pos = i % tiles_per_seq
    u_before = u_before * (pos != 0).astype(F32)
    u_after = u_after * (pos != tiles_per_seq - 1).astype(F32)
    row = lax.broadcasted_iota(jnp.int32, u.shape, 0)
    u_prev = jnp.where(row == 0, u_before, pltpu.roll(u, 1, axis=0))
    u_next = jnp.where(row == tm - 1, u_after, pltpu.roll(u, tm - 1, axis=0))
    conv = u_prev * wconv_ref[0:1, :] + u * wconv_ref[1:2, :] + u_next * wconv_ref[2:3, :]
    a_in = (gb_ref[...].astype(F32) * conv).astype(BF16)
    y_a = jnp.dot(a_in, wa_ref[...], preferred_element_type=F32)
    y_r = jnp.dot(z_ref[...], wr_ref[...], preferred_element_type=F32)
    merged = ga_ref[...].astype(F32) * y_a + gr_ref[...].astype(F32) * y_r
    o_ref[...] = x_ref[...] + jnp.dot(merged.astype(BF16), wo_ref[...],
                                      preferred_element_type=F32)


def _const_spec(shape):
    return pl.BlockSpec(shape, lambda i: (0,) * len(shape), pipeline_mode=pl.Buffered(1))


def _merge(x2, proj, z, w_conv, w_a, w_r, w_o, seq, tm):
    t, d = x2.shape
    hb = tm // BF16_SUBLANES
    n_halo_blocks = t // BF16_SUBLANES

    def prev_block(i):
        return jnp.maximum(i * hb - 1, 0)

    def next_block(i):
        return jnp.minimum((i + 1) * hb, n_halo_blocks - 1)

    halo = (BF16_SUBLANES, SEC)
    return pl.pallas_call(
        functools.partial(_merge_kernel, tiles_per_seq=seq // tm),
        grid=(t // tm,),
        in_specs=[
            pl.BlockSpec((tm, d), lambda i: (i, 0)),
            pl.BlockSpec((tm, SEC), lambda i: (i, SEC_XC)),
            pl.BlockSpec((tm, SEC), lambda i: (i, SEC_GB)),
            pl.BlockSpec((tm, SEC), lambda i: (i, SEC_GC)),
            pl.BlockSpec(halo, lambda i: (prev_block(i), SEC_XC)),
            pl.BlockSpec(halo, lambda i: (prev_block(i), SEC_GC)),
            pl.BlockSpec(halo, lambda i: (next_block(i), SEC_XC)),
            pl.BlockSpec(halo, lambda i: (next_block(i), SEC_GC)),
            pl.BlockSpec((tm, SEC), lambda i: (i, SEC_GA)),
            pl.BlockSpec((tm, SEC), lambda i: (i, SEC_GR)),
            pl.BlockSpec((tm, RET_V), lambda i: (i, 0)),
            _const_spec(w_conv.shape),
            _const_spec(w_a.shape),
            _const_spec(w_r.shape),
            _const_spec(w_o.shape),
        ],
        out_specs=pl.BlockSpec((tm, d), lambda i: (i, 0)),
        out_shape=jax.ShapeDtypeStruct((t, d), F32),
        compiler_params=pltpu.CompilerParams(
            dimension_semantics=("arbitrary",),
            vmem_limit_bytes=VMEM_LIMIT),
        name="merge",
    )(x2, proj, proj, proj, proj, proj, proj, proj, proj, proj, z, w_conv, w_a, w_r, w_o)


def _ffn_kernel(x_ref, gffn_ref, wg_ref, wu_ref, wd_ref, gfin_ref, o_ref, *, final_norm):
    x = x_ref[...]
    h2 = (_rms_scale(x) * gffn_ref[...]).astype(BF16)
    gate = jnp.dot(h2, wg_ref[...], preferred_element_type=F32)
    up = jnp.dot(h2, wu_ref[...], preferred_element_type=F32)
    act = (gate * _sigmoid(gate) * up).astype(BF16)
    y = x + jnp.dot(act, wd_ref[...], preferred_element_type=F32)
    o_ref[...] = _rms_scale(y) * gfin_ref[...] if final_norm else y


def _ffn(x1, g_ffn, w_gate, w_up, w_down, g_final, final_norm, tm):
    t, d = x1.shape
    return pl.pallas_call(
        functools.partial(_ffn_kernel, final_norm=final_norm),
        grid=(t // tm,),
        in_specs=[
            pl.BlockSpec((tm, d), lambda i: (i, 0)),
            _const_spec(g_ffn.shape),
            _const_spec(w_gate.shape),
            _const_spec(w_up.shape),
            _const_spec(w_down.shape),
            _const_spec(g_final.shape),
        ],
        out_specs=pl.BlockSpec((tm, d), lambda i: (i, 0)),
        out_shape=jax.ShapeDtypeStruct((t, d), F32),
        compiler_params=pltpu.CompilerParams(
            dimension_semantics=("arbitrary",),
            vmem_limit_bytes=VMEM_LIMIT),
        name="ffn",
    )(x1, g_ffn, w_gate, w_up, w_down, g_final)


def _rotary_tables(seq):
    pos = jnp.arange(seq, dtype=F32)
    freqs = ROPE_BASE ** (-jnp.arange(0, QK_DIM, 2, dtype=F32) / QK_DIM)
    ang = pos[:, None] * freqs[None, :]
    return jnp.cos(ang), jnp.sin(ang)


def kernel(x, g_mix, w_in, w_conv, dec_f, dec_b, g_ret, w_a_out, w_r_out, w_o,
           g_ffn, w_ff_gate, w_ff_up, w_ff_down, g_final):
    batch, seq, d = x.shape
    depth = w_in.shape[0]
    assert d == D_MODEL and seq % RET_CHUNK == 0
    x2 = x.reshape(batch * seq, d)
    cos, sin = _rotary_tables(seq)
    for l in range(depth):
        lg = jnp.stack([jax.nn.log_sigmoid(dec_f[l].astype(F32)),
                        jax.nn.log_sigmoid(dec_b[l].astype(F32))])
        proj = _inproj(x2, g_mix[l][None, :], w_in[l].astype(BF16), cos, sin, tm=1024)
        z = _retention(lg, proj, g_ret[l][None, :], batch, seq)
        x1 = _merge(x2, proj, z, w_conv[l], w_a_out[l].astype(BF16),
                    w_r_out[l].astype(BF16), w_o[l].astype(BF16), seq, tm=512)
        x2 = _ffn(x1, g_ffn[l][None, :], w_ff_gate[l].astype(BF16),
                  w_ff_up[l].astype(BF16), w_ff_down[l].astype(BF16),
                  g_final[None, :], final_norm=(l == depth - 1), tm=512)
    return x2.reshape(batch, seq, d)
```

```python
import functools

import jax
import jax.numpy as jnp
from jax import lax
from jax.experimental import pallas as pl
from jax.experimental.pallas import tpu as pltpu

F32 = jnp.float32
BF16 = jnp.bfloat16

D_MODEL = 1024
RET_HEADS = 4
QK_DIM = D_MODEL // RET_HEADS
V_DIM = 2 * QK_DIM
RET_V = RET_HEADS * V_DIM
ROPE_BASE = 10000.0
EPS = 1e-6
CONV_K = 3

SEC = 1024
SEC_XC, SEC_GB, SEC_GC, SEC_Q, SEC_K, SEC_V, SEC_GSW, SEC_GA, SEC_GR = 0, 1, 2, 3, 4, 5, 7, 9, 10
N_SEC = 11

RET_CHUNK = 256
LANES = 128
BF16_SUBLANES = 16

VMEM_LIMIT = 56 * 1024 * 1024


def _sigmoid(x):
    return 1.0 / (1.0 + jnp.exp(-x))


def _rms_scale(x):
    return x * lax.rsqrt(jnp.mean(x * x, axis=-1, keepdims=True) + EPS)


def _const_spec(shape):
    return pl.BlockSpec(shape, lambda *_: (0,) * len(shape), pipeline_mode=pl.Buffered(1))


def _inproj_kernel(x_ref, g_ref, w_ref, cos_ref, sin_ref, gret_ref, o_ref):
    h = (_rms_scale(x_ref[...]) * g_ref[...]).astype(BF16)
    half = QK_DIM // 2
    for sec in range(N_SEC):
        base = sec * SEC
        acc = jnp.dot(h, w_ref[:, base:base + SEC], preferred_element_type=F32)
        if sec in (SEC_Q, SEC_K):
            scale = QK_DIM ** -0.5 if sec == SEC_Q else 1.0
            cos = cos_ref[...]
            sin = sin_ref[...]
            for hh in range(RET_HEADS):
                lo = hh * QK_DIM
                t1 = acc[:, lo:lo + half]
                t2 = acc[:, lo + half:lo + QK_DIM]
                o_ref[:, base + lo:base + lo + half] = (
                    (t1 * cos - t2 * sin) * scale).astype(BF16)
                o_ref[:, base + lo + half:base + lo + QK_DIM] = (
                    (t1 * sin + t2 * cos) * scale).astype(BF16)
        elif SEC_GSW <= sec < SEC_GA:
            gret = gret_ref[:, base - SEC_GSW * SEC:base - SEC_GSW * SEC + SEC]
            o_ref[:, base:base + SEC] = (acc * _sigmoid(acc) * gret).astype(BF16)
        elif sec >= SEC_GA:
            o_ref[:, base:base + SEC] = _sigmoid(acc).astype(BF16)
        else:
            o_ref[:, base:base + SEC] = acc.astype(BF16)


def _inproj(x2, g_mix, w_in, cos, sin, g_ret, tm):
    t, d = x2.shape
    n_pos_blocks = cos.shape[0] // tm
    n_in = w_in.shape[1]
    return pl.pallas_call(
        _inproj_kernel,
        grid=(t // tm,),
        in_specs=[
            pl.BlockSpec((tm, d), lambda i: (i, 0)),
            _const_spec(g_mix.shape),
            _const_spec(w_in.shape),
            pl.BlockSpec((tm, QK_DIM // 2), lambda i: (i % n_pos_blocks, 0)),
            pl.BlockSpec((tm, QK_DIM // 2), lambda i: (i % n_pos_blocks, 0)),
            _const_spec(g_ret.shape),
        ],
        out_specs=pl.BlockSpec((tm, n_in), lambda i: (i, 0)),
        out_shape=jax.ShapeDtypeStruct((t, n_in), BF16),
        compiler_params=pltpu.CompilerParams(
            dimension_semantics=("arbitrary",),
            vmem_limit_bytes=VMEM_LIMIT),
        name="inproj",
    )(x2, g_mix, w_in, cos, sin, g_ret)


def _retention_kernel(lg_ref, q_ref, k_ref, v_ref, gsw_ref, z_ref,
                      decay_ref, xif_ref, xib_ref, zetaf_ref, zetab_ref,
                      state_ref, rall_ref, *, n_blocks, chunks_per_block):
    c = RET_CHUNK
    h = pl.program_id(1)
    t = pl.program_id(2)
    lg_f = lg_ref[0, h]
    lg_b = lg_ref[1, h]

    @pl.when(t == 0)
    def _():
        row = lax.broadcasted_iota(jnp.int32, (c, c), 0).astype(F32)
        col = lax.broadcasted_iota(jnp.int32, (c, c), 1).astype(F32)
        diff = row - col
        decay_ref[...] = jnp.where(diff >= 0.0,
                                   jnp.exp(jnp.maximum(diff, 0.0) * lg_f),
                                   jnp.exp(jnp.maximum(-diff, 0.0) * lg_b))
        idx = lax.broadcasted_iota(jnp.int32, (c, LANES), 0).astype(F32)
        xif_ref[...] = jnp.exp((idx + 1.0) * lg_f)
        zetaf_ref[...] = jnp.exp((c - 1.0 - idx) * lg_f)
        xib_ref[...] = jnp.exp((c - idx) * lg_b)
        zetab_ref[...] = jnp.exp(idx * lg_b)
        state_ref[...] = jnp.zeros_like(state_ref)

    def row_scaled(a, scale_ref):
        scale = scale_ref[...]
        a32 = a.astype(F32)
        return jnp.concatenate(
            [(a32[:, n * LANES:(n + 1) * LANES] * scale).astype(BF16)
             for n in range(a.shape[1] // LANES)], axis=1)

    def state_update(k, v, zeta_ref, chunk_decay):
        kv = lax.dot_general(row_scaled(k, zeta_ref), v,
                             (((0,), (0,)), ((), ())), preferred_element_type=F32)
        state_ref[...] = chunk_decay * state_ref[...] + kv

    def chunk_rows(ci):
        return pl.ds(pl.multiple_of(ci * c, c), c)

    @pl.when(t < n_blocks)
    def _():
        blk = n_blocks - 1 - t
        chunk_decay = jnp.exp(jnp.zeros((1, V_DIM), F32) + c * lg_b)

        def body(i, carry):
            ci = chunks_per_block - 1 - i
            rows = chunk_rows(ci)
            rall_ref[blk * chunks_per_block + ci] = state_ref[...].astype(BF16)
            state_update(k_ref[rows, :], v_ref[rows, :], zetab_ref, chunk_decay)
            return carry

        lax.fori_loop(0, chunks_per_block, body, 0, unroll=True)

    @pl.when(t == n_blocks)
    def _():
        state_ref[...] = jnp.zeros_like(state_ref)

    @pl.when(t >= n_blocks)
    def _():
        blk = t - n_blocks
        chunk_decay = jnp.exp(jnp.zeros((1, V_DIM), F32) + c * lg_f)

        def body(ci, carry):
            rows = chunk_rows(ci)
            q = q_ref[rows, :]
            k = k_ref[rows, :]
            v = v_ref[rows, :]
            scores = lax.dot_general(q, k, (((1,), (1,)), ((), ())),
                                     preferred_element_type=F32)
            p = (scores * decay_ref[...]).astype(BF16)
            lhs = jnp.concatenate([p, row_scaled(q, xif_ref), row_scaled(q, xib_ref)], axis=1)
            rhs = jnp.concatenate([v, state_ref[...].astype(BF16),
                                   rall_ref[blk * chunks_per_block + ci]], axis=0)
            ret = jnp.dot(lhs, rhs, preferred_element_type=F32)
            mu = jnp.mean(ret, axis=-1, keepdims=True)
            dev = ret - mu
            var = jnp.mean(dev * dev, axis=-1, keepdims=True)
            normed = dev * lax.rsqrt(var + EPS)
            z_ref[rows, :] = (normed * gsw_ref[rows, :].astype(F32)).astype(BF16)
            state_update(k, v, zetaf_ref, chunk_decay)
            return carry

        lax.fori_loop(0, chunks_per_block, body, 0, unroll=True)


def _retention(lg, proj, batch, seq, block_tokens):
    c = RET_CHUNK
    nb = seq // block_tokens
    cpb = block_tokens // c
    t = batch * seq
    qb = (SEC_Q * SEC) // QK_DIM
    kb = (SEC_K * SEC) // QK_DIM
    vb = (SEC_V * SEC) // V_DIM
    gb = (SEC_GSW * SEC) // V_DIM

    def fwd_block(tt):
        return jnp.maximum(tt - nb, 0)

    def kv_block(tt):
        return jnp.where(tt < nb, nb - 1 - tt, tt - nb)

    return pl.pallas_call(
        functools.partial(_retention_kernel, n_blocks=nb, chunks_per_block=cpb),
        grid=(batch, RET_HEADS, 2 * nb),
        in_specs=[
            pl.BlockSpec(memory_space=pltpu.SMEM),
            pl.BlockSpec((block_tokens, QK_DIM), lambda b, h, tt: (b * nb + fwd_block(tt), qb + h)),
            pl.BlockSpec((block_tokens, QK_DIM), lambda b, h, tt: (b * nb + kv_block(tt), kb + h)),
            pl.BlockSpec((block_tokens, V_DIM), lambda b, h, tt: (b * nb + kv_block(tt), vb + h)),
            pl.BlockSpec((block_tokens, V_DIM), lambda b, h, tt: (b * nb + fwd_block(tt), gb + h)),
        ],
        out_specs=pl.BlockSpec((block_tokens, V_DIM), lambda b, h, tt: (b * nb + fwd_block(tt), h)),
        out_shape=jax.ShapeDtypeStruct((t, RET_V), BF16),
        scratch_shapes=[
            pltpu.VMEM((c, c), F32),
            pltpu.VMEM((c, LANES), F32),
            pltpu.VMEM((c, LANES), F32),
            pltpu.VMEM((c, LANES), F32),
            pltpu.VMEM((c, LANES), F32),
            pltpu.VMEM((QK_DIM, V_DIM), F32),
            pltpu.VMEM((seq // c, QK_DIM, V_DIM), BF16),
        ],
        compiler_params=pltpu.CompilerParams(
            dimension_semantics=("arbitrary", "arbitrary", "arbitrary"),
            vmem_limit_bytes=VMEM_LIMIT),
        name="retention",
    )(lg, proj, proj, proj, proj)


def _merge_kernel(x_ref, xc_ref, gb_ref, gc_ref, xcp_ref, gcp_ref, xcn_ref, gcn_ref,
                  ga_ref, gr_ref, z_ref, wconv_ref, wa_ref, wr_ref, wo_ref, o_ref,
                  *, tiles_per_seq):
    i = pl.program_id(0)
    tm = x_ref.shape[0]
    u = gc_ref[...].astype(F32) * xc_ref[...].astype(F32)
    last = BF16_SUBLANES - 1
    u_before = (gcp_ref[last:last + 1, :].astype(F32) * xcp_ref[last:last + 1, :].astype(F32))
    u_after = gcn_ref[0:1, :].astype(F32) * xcn_ref[0:1, :].astype(F32)
    pos = i % tiles_per_seq
    u_before = u_before * (pos != 0).astype(F32)
    u_after = u_after * (pos != tiles_per_seq - 1).astype(F32)
    row = lax.broadcasted_iota(jnp.int32, u.shape, 0)
    u_prev = jnp.where(row == 0, u_before, pltpu.roll(u, 1, axis=0))
    u_next = jnp.where(row == tm - 1, u_after, pltpu.roll(u, tm - 1, axis=0))
    conv = u_prev * wconv_ref[0:1, :] + u * wconv_ref[1:2, :] + u_next * wconv_ref[2:3, :]
    a_in = (gb_ref[...].astype(F32) * conv).astype(BF16)
    y_a = jnp.dot(a_in, wa_ref[...], preferred_element_type=F32)
    y_r = jnp.dot(z_ref[...], wr_ref[...], preferred_element_type=F32)
    merged = ga_ref[...].astype(F32) * y_a + gr_ref[...].astype(F32) * y_r
    o_ref[...] = x_ref[...] + jnp.dot(merged.astype(BF16), wo_ref[...],
                                      preferred_element_type=F32)


def _merge(x2, proj, z, w_conv, w_a, w_r, w_o, seq, tm):
    t, d = x2.shape
    hb = tm // BF16_SUBLANES
    n_halo_blocks = t // BF16_SUBLANES

    def prev_block(i):
        return jnp.maximum(i * hb - 1, 0)

    def next_block(i):
        return jnp.minimum((i + 1) * hb, n_halo_blocks - 1)

    halo = (BF16_SUBLANES, SEC)
    return pl.pallas_call(
        functools.partial(_merge_kernel, tiles_per_seq=seq // tm),
        grid=(t // tm,),
        in_specs=[
            pl.BlockSpec((tm, d), lambda i: (i, 0)),
            pl.BlockSpec((tm, SEC), lambda i: (i, SEC_XC)),
            pl.BlockSpec((tm, SEC), lambda i: (i, SEC_GB)),
            pl.BlockSpec((tm, SEC), lambda i: (i, SEC_GC)),
            pl.BlockSpec(halo, lambda i: (prev_block(i), SEC_XC)),
            pl.BlockSpec(halo, lambda i: (prev_block(i), SEC_GC)),
            pl.BlockSpec(halo, lambda i: (next_block(i), SEC_XC)),
            pl.BlockSpec(halo, lambda i: (next_block(i), SEC_GC)),
            pl.BlockSpec((tm, SEC), lambda i: (i, SEC_GA)),
            pl.BlockSpec((tm, SEC), lambda i: (i, SEC_GR)),
            pl.BlockSpec((tm, RET_V), lambda i: (i, 0)),
            _const_spec(w_conv.shape),
            _const_spec(w_a.shape),
            _const_spec(w_r.shape),
            _const_spec(w_o.shape),
        ],
        out_specs=pl.BlockSpec((tm, d), lambda i: (i, 0)),
        out_shape=jax.ShapeDtypeStruct((t, d), F32),
        compiler_params=pltpu.CompilerParams(
            dimension_semantics=("arbitrary",),
            vmem_limit_bytes=VMEM_LIMIT),
        name="merge",
    )(x2, proj, proj, proj, proj, proj, proj, proj, proj, proj, z, w_conv, w_a, w_r, w_o)


def _ffn_kernel(x_ref, gffn_ref, wg_ref, wu_ref, wd_ref, gfin_ref, o_ref, *, final_norm):
    x = x_ref[...]
    h2 = (_rms_scale(x) * gffn_ref[...]).astype(BF16)
    gate = jnp.dot(h2, wg_ref[...], preferred_element_type=F32)
    up = jnp.dot(h2, wu_ref[...], preferred_element_type=F32)
    act = (gate * _sigmoid(gate) * up).astype(BF16)
    y = x + jnp.dot(act, wd_ref[...], preferred_element_type=F32)
    o_ref[...] = _rms_scale(y) * gfin_ref[...] if final_norm else y


def _ffn(x1, g_ffn, w_gate, w_up, w_down, g_final, final_norm, tm):
    t, d = x1.shape
    return pl.pallas_call(
        functools.partial(_ffn_kernel, final_norm=final_norm),
        grid=(t // tm,),
        in_specs=[
            pl.BlockSpec((tm, d), lambda i: (i, 0)),
            _const_spec(g_ffn.shape),
            _const_spec(w_gate.shape),
            _const_spec(w_up.shape),
            _const_spec(w_down.shape),
            _const_spec(g_final.shape),
        ],
        out_specs=pl.BlockSpec((tm, d), lambda i: (i, 0)),
        out_shape=jax.ShapeDtypeStruct((t, d), F32),
        compiler_params=pltpu.CompilerParams(
            dimension_semantics=("arbitrary",),
            vmem_limit_bytes=VMEM_LIMIT),
        name="ffn",
    )(x1, g_ffn, w_gate, w_up, w_down, g_final)


def _rotary_tables(seq):
    pos = jnp.arange(seq, dtype=F32)
    freqs = ROPE_BASE ** (-jnp.arange(0, QK_DIM, 2, dtype=F32) / QK_DIM)
    ang = pos[:, None] * freqs[None, :]
    return jnp.cos(ang), jnp.sin(ang)


def kernel(x, g_mix, w_in, w_conv, dec_f, dec_b, g_ret, w_a_out, w_r_out, w_o,
           g_ffn, w_ff_gate, w_ff_up, w_ff_down, g_final):
    batch, seq, d = x.shape
    depth = w_in.shape[0]
    assert d == D_MODEL and seq % RET_CHUNK == 0
    x2 = x.reshape(batch * seq, d)
    cos, sin = _rotary_tables(seq)
    for l in range(depth):
        lg = jnp.stack([jax.nn.log_sigmoid(dec_f[l].astype(F32)),
                        jax.nn.log_sigmoid(dec_b[l].astype(F32))])
        proj = _inproj(x2, g_mix[l][None, :], w_in[l].astype(BF16), cos, sin,
                       g_ret[l][None, :].astype(F32), tm=256)
        z = _retention(lg, proj, batch, seq, block_tokens=2048)
        x1 = _merge(x2, proj, z, w_conv[l], w_a_out[l].astype(BF16),
                    w_r_out[l].astype(BF16), w_o[l].astype(BF16), seq, tm=512)
        x2 = _ffn(x1, g_ffn[l][None, :], w_ff_gate[l].astype(BF16),
                  w_ff_up[l].astype(BF16), w_ff_down[l].astype(BF16),
                  g_final[None, :], final_norm=(l == depth - 1), tm=512)
    return x2.reshape(batch, seq, d)
```

```python
import functools

import jax
import jax.numpy as jnp
from jax import lax
from jax.experimental import pallas as pl
from jax.experimental.pallas import tpu as pltpu

F32 = jnp.float32
BF16 = jnp.bfloat16

D_MODEL = 1024
RET_HEADS = 4
QK_DIM = D_MODEL // RET_HEADS
V_DIM = 2 * QK_DIM
RET_V = RET_HEADS * V_DIM
ROPE_BASE = 10000.0
EPS = 1e-6
CONV_K = 3

SEC = 1024
SEC_XC, SEC_GB, SEC_GC, SEC_Q, SEC_K, SEC_V, SEC_GSW, SEC_GA, SEC_GR = 0, 1, 2, 3, 4, 5, 7, 9, 10
N_SEC = 11

RET_CHUNK = 256
LANES = 128
BF16_SUBLANES = 16

VMEM_LIMIT = 56 * 1024 * 1024


def _sigmoid(x):
    return 1.0 / (1.0 + jnp.exp(-x))


def _rms_scale(x):
    return x * lax.rsqrt(jnp.mean(x * x, axis=-1, keepdims=True) + EPS)


def _const_spec(shape):
    return pl.BlockSpec(shape, lambda *_: (0,) * len(shape), pipeline_mode=pl.Buffered(1))


def _inproj_kernel(x_ref, g_ref, w_ref, cos_base_ref, sin_base_ref, cos_off_ref, sin_off_ref,
                   gret_ref, o_ref, *, n_pos_blocks):
    h = (_rms_scale(x_ref[...]) * g_ref[...]).astype(BF16)
    half = QK_DIM // 2
    pos_block = pl.ds(pl.program_id(0) % n_pos_blocks, 1)
    cos_b = cos_base_ref[pos_block, :]
    sin_b = sin_base_ref[pos_block, :]
    cos = cos_b * cos_off_ref[...] - sin_b * sin_off_ref[...]
    sin = sin_b * cos_off_ref[...] + cos_b * sin_off_ref[...]
    order = sorted(range(N_SEC), key=lambda s: (not (SEC_Q <= s <= SEC_K or s >= SEC_GSW), s))
    for sec in order:
        base = sec * SEC
        acc = jnp.dot(h, w_ref[:, base:base + SEC], preferred_element_type=F32)
        if sec in (SEC_Q, SEC_K):
            scale = QK_DIM ** -0.5 if sec == SEC_Q else 1.0
            for hh in range(RET_HEADS):
                lo = hh * QK_DIM
                t1 = acc[:, lo:lo + half]
                t2 = acc[:, lo + half:lo + QK_DIM]
                o_ref[:, base + lo:base + lo + half] = (
                    (t1 * cos - t2 * sin) * scale).astype(BF16)
                o_ref[:, base + lo + half:base + lo + QK_DIM] = (
                    (t1 * sin + t2 * cos) * scale).astype(BF16)
        elif SEC_GSW <= sec < SEC_GA:
            gret = gret_ref[:, base - SEC_GSW * SEC:base - SEC_GSW * SEC + SEC]
            o_ref[:, base:base + SEC] = (acc * _sigmoid(acc) * gret).astype(BF16)
        elif sec >= SEC_GA:
            o_ref[:, base:base + SEC] = _sigmoid(acc).astype(BF16)
        else:
            o_ref[:, base:base + SEC] = acc.astype(BF16)


def _inproj(x2, g_mix, w_in, rot, g_ret, tm):
    t, d = x2.shape
    cos_base, sin_base, cos_off, sin_off = rot
    assert cos_off.shape[0] == tm
    n_in = w_in.shape[1]
    return pl.pallas_call(
        functools.partial(_inproj_kernel, n_pos_blocks=cos_base.shape[0]),
        grid=(t // tm,),
        in_specs=[
            pl.BlockSpec((tm, d), lambda i: (i, 0)),
            _const_spec(g_mix.shape),
            _const_spec(w_in.shape),
            _const_spec(cos_base.shape),
            _const_spec(sin_base.shape),
            _const_spec(cos_off.shape),
            _const_spec(sin_off.shape),
            _const_spec(g_ret.shape),
        ],
        out_specs=pl.BlockSpec((tm, n_in), lambda i: (i, 0)),
        out_shape=jax.ShapeDtypeStruct((t, n_in), BF16),
        compiler_params=pltpu.CompilerParams(
            dimension_semantics=("arbitrary",),
            vmem_limit_bytes=VMEM_LIMIT),
        name="inproj",
    )(x2, g_mix, w_in, cos_base, sin_base, cos_off, sin_off, g_ret)


def _retention_kernel(lg_ref, q_ref, k_ref, v_ref, z_ref,
                      decay_ref, xif_ref, xib_ref, zetaf_ref, zetab_ref,
                      state_ref, rall_ref, *, n_blocks, chunks_per_block):
    c = RET_CHUNK
    h = pl.program_id(1)
    t = pl.program_id(2)
    lg_f = lg_ref[0, h]
    lg_b = lg_ref[1, h]

    @pl.when(t == 0)
    def _():
        row = lax.broadcasted_iota(jnp.int32, (c, c), 0).astype(F32)
        col = lax.broadcasted_iota(jnp.int32, (c, c), 1).astype(F32)
        diff = row - col
        decay_ref[...] = jnp.where(diff >= 0.0,
                                   jnp.exp(jnp.maximum(diff, 0.0) * lg_f),
                                   jnp.exp(jnp.maximum(-diff, 0.0) * lg_b))
        idx = lax.broadcasted_iota(jnp.int32, (c, LANES), 0).astype(F32)
        xif_ref[...] = jnp.exp((idx + 1.0) * lg_f).astype(BF16)
        zetaf_ref[...] = jnp.exp((c - 1.0 - idx) * lg_f).astype(BF16)
        xib_ref[...] = jnp.exp((c - idx) * lg_b).astype(BF16)
        zetab_ref[...] = jnp.exp(idx * lg_b).astype(BF16)
        state_ref[...] = jnp.zeros_like(state_ref)

    def row_scaled(a, scale_ref):
        scale = scale_ref[...]
        return jnp.concatenate(
            [a[:, n * LANES:(n + 1) * LANES] * scale
             for n in range(a.shape[1] // LANES)], axis=1)

    def state_update(k, v, zeta_ref, chunk_decay):
        kv = lax.dot_general(row_scaled(k, zeta_ref), v,
                             (((0,), (0,)), ((), ())), preferred_element_type=F32)
        state_ref[...] = chunk_decay * state_ref[...] + kv

    def chunk_rows(ci):
        return pl.ds(pl.multiple_of(ci * c, c), c)

    @pl.when(t < n_blocks)
    def _():
        blk = n_blocks - 1 - t
        chunk_decay = jnp.exp(jnp.zeros((1, V_DIM), F32) + c * lg_b)

        def body(i, carry):
            ci = chunks_per_block - 1 - i
            rows = chunk_rows(ci)
            rall_ref[blk * chunks_per_block + ci] = state_ref[...].astype(BF16)
            state_update(k_ref[rows, :], v_ref[rows, :], zetab_ref, chunk_decay)
            return carry

        lax.fori_loop(0, chunks_per_block, body, 0, unroll=True)

    @pl.when(t == n_blocks)
    def _():
        state_ref[...] = jnp.zeros_like(state_ref)

    @pl.when(t >= n_blocks)
    def _():
        blk = t - n_blocks
        chunk_decay = jnp.exp(jnp.zeros((1, V_DIM), F32) + c * lg_f)

        def body(ci, carry):
            rows = chunk_rows(ci)
            q = q_ref[rows, :]
            k = k_ref[rows, :]
            v = v_ref[rows, :]
            scores = lax.dot_general(q, k, (((1,), (1,)), ((), ())),
                                     preferred_element_type=F32)
            p = (scores * decay_ref[...]).astype(BF16)
            lhs = jnp.concatenate([p, row_scaled(q, xif_ref), row_scaled(q, xib_ref)], axis=1)
            rhs = jnp.concatenate([v, state_ref[...].astype(BF16),
                                   rall_ref[blk * chunks_per_block + ci]], axis=0)
            ret = jnp.dot(lhs, rhs, preferred_element_type=F32)
            mu = jnp.mean(ret, axis=-1, keepdims=True)
            dev = ret - mu
            var = jnp.mean(dev * dev, axis=-1, keepdims=True)
            z_ref[rows, :] = (dev * lax.rsqrt(var + EPS)).astype(BF16)
            state_update(k, v, zetaf_ref, chunk_decay)
            return carry

        lax.fori_loop(0, chunks_per_block, body, 0, unroll=True)


def _retention(lg, proj, batch, seq, block_tokens):
    c = RET_CHUNK
    nb = seq // block_tokens
    cpb = block_tokens // c
    t = batch * seq
    qb = (SEC_Q * SEC) // QK_DIM
    kb = (SEC_K * SEC) // QK_DIM
    vb = (SEC_V * SEC) // V_DIM

    def fwd_block(tt):
        return jnp.maximum(tt - nb, 0)

    def kv_block(tt):
        return jnp.where(tt < nb, nb - 1 - tt, tt - nb)

    return pl.pallas_call(
        functools.partial(_retention_kernel, n_blocks=nb, chunks_per_block=cpb),
        grid=(batch, RET_HEADS, 2 * nb),
        in_specs=[
            pl.BlockSpec(memory_space=pltpu.SMEM),
            pl.BlockSpec((block_tokens, QK_DIM), lambda b, h, tt: (b * nb + fwd_block(tt), qb + h)),
            pl.BlockSpec((block_tokens, QK_DIM), lambda b, h, tt: (b * nb + kv_block(tt), kb + h)),
            pl.BlockSpec((block_tokens, V_DIM), lambda b, h, tt: (b * nb + kv_block(tt), vb + h)),
        ],
        out_specs=pl.BlockSpec((block_tokens, V_DIM), lambda b, h, tt: (b * nb + fwd_block(tt), h)),
        out_shape=jax.ShapeDtypeStruct((t, RET_V), BF16),
        scratch_shapes=[
            pltpu.VMEM((c, c), F32),
            pltpu.VMEM((c, LANES), BF16),
            pltpu.VMEM((c, LANES), BF16),
            pltpu.VMEM((c, LANES), BF16),
            pltpu.VMEM((c, LANES), BF16),
            pltpu.VMEM((QK_DIM, V_DIM), F32),
            pltpu.VMEM((seq // c, QK_DIM, V_DIM), BF16),
        ],
        compiler_params=pltpu.CompilerParams(
            dimension_semantics=("arbitrary", "arbitrary", "arbitrary"),
            vmem_limit_bytes=VMEM_LIMIT),
        name="retention",
    )(lg, proj, proj, proj)


def _merge_kernel(x_ref, xc_ref, gb_ref, gc_ref, xcp_ref, gcp_ref, xcn_ref, gcn_ref,
                  ga_ref, gr_ref, z_ref, gsw0_ref, gsw1_ref, wconv_ref, wa_ref, wr_ref, wo_ref,
                  o_ref, *, tiles_per_seq):
    i = pl.program_id(0)
    tm = x_ref.shape[0]
    u = gc_ref[...].astype(F32) * xc_ref[...].astype(F32)
    last = BF16_SUBLANES - 1
    u_before = (gcp_ref[last:last + 1, :].astype(F32) * xcp_ref[last:last + 1, :].astype(F32))
    u_after = gcn_ref[0:1, :].astype(F32) * xcn_ref[0:1, :].astype(F32)
    pos = i % tiles_per_seq
    u_before = u_before * (pos != 0).astype(F32)
    u_after = u_after * (pos != tiles_per_seq - 1).astype(F32)
    row = lax.broadcasted_iota(jnp.int32, u.shape, 0)
    u_prev = jnp.where(row == 0, u_before, pltpu.roll(u, 1, axis=0))
    u_next = jnp.where(row == tm - 1, u_after, pltpu.roll(u, tm - 1, axis=0))
    conv = u_prev * wconv_ref[0:1, :] + u * wconv_ref[1:2, :] + u_next * wconv_ref[2:3, :]
    a_in = (gb_ref[...].astype(F32) * conv).astype(BF16)
    y_a = jnp.dot(a_in, wa_ref[...], preferred_element_type=F32)
    r_in = jnp.concatenate([z_ref[:, :SEC] * gsw0_ref[...], z_ref[:, SEC:] * gsw1_ref[...]], axis=1)
    y_r = jnp.dot(r_in, wr_ref[...], preferred_element_type=F32)
    merged = ga_ref[...].astype(F32) * y_a + gr_ref[...].astype(F32) * y_r
    o_ref[...] = x_ref[...] + jnp.dot(merged.astype(BF16), wo_ref[...],
                                      preferred_element_type=F32)


def _merge(x2, proj, z, w_conv, w_a, w_r, w_o, seq, tm):
    t, d = x2.shape
    hb = tm // BF16_SUBLANES
    n_halo_blocks = t // BF16_SUBLANES

    def prev_block(i):
        return jnp.maximum(i * hb - 1, 0)

    def next_block(i):
        return jnp.minimum((i + 1) * hb, n_halo_blocks - 1)

    halo = (BF16_SUBLANES, SEC)
    return pl.pallas_call(
        functools.partial(_merge_kernel, tiles_per_seq=seq // tm),
        grid=(t // tm,),
        in_specs=[
            pl.BlockSpec((tm, d), lambda i: (i, 0)),
            pl.BlockSpec((tm, SEC), lambda i: (i, SEC_XC)),
            pl.BlockSpec((tm, SEC), lambda i: (i, SEC_GB)),
            pl.BlockSpec((tm, SEC), lambda i: (i, SEC_GC)),
            pl.BlockSpec(halo, lambda i: (prev_block(i), SEC_XC)),
            pl.BlockSpec(halo, lambda i: (prev_block(i), SEC_GC)),
            pl.BlockSpec(halo, lambda i: (next_block(i), SEC_XC)),
            pl.BlockSpec(halo, lambda i: (next_block(i), SEC_GC)),
            pl.BlockSpec((tm, SEC), lambda i: (i, SEC_GA)),
            pl.BlockSpec((tm, SEC), lambda i: (i, SEC_GR)),
            pl.BlockSpec((tm, RET_V), lambda i: (i, 0)),
            pl.BlockSpec((tm, SEC), lambda i: (i, SEC_GSW)),
            pl.BlockSpec((tm, SEC), lambda i: (i, SEC_GSW + 1)),
            _const_spec(w_conv.shape),
            _const_spec(w_a.shape),
            _const_spec(w_r.shape),
            _const_spec(w_o.shape),
        ],
        out_specs=pl.BlockSpec((tm, d), lambda i: (i, 0)),
        out_shape=jax.ShapeDtypeStruct((t, d), F32),
        compiler_params=pltpu.CompilerParams(
            dimension_semantics=("arbitrary",),
            vmem_limit_bytes=VMEM_LIMIT),
        name="merge",
    )(x2, proj, proj, proj, proj, proj, proj, proj, proj, proj, z, proj, proj,
      w_conv, w_a, w_r, w_o)


def _ffn_kernel(x_ref, gffn_ref, wg_ref, wu_ref, wd_ref, gfin_ref, o_ref, *, final_norm):
    x = x_ref[...]
    h2 = (_rms_scale(x) * gffn_ref[...]).astype(BF16)
    gate = jnp.dot(h2, wg_ref[...], preferred_element_type=F32)
    up = jnp.dot(h2, wu_ref[...], preferred_element_type=F32)
    act = (gate * _sigmoid(gate) * up).astype(BF16)
    y = x + jnp.dot(act, wd_ref[...], preferred_element_type=F32)
    o_ref[...] = _rms_scale(y) * gfin_ref[...] if final_norm else y


def _ffn(x1, g_ffn, w_gate, w_up, w_down, g_final, final_norm, tm):
    t, d = x1.shape
    return pl.pallas_call(
        functools.partial(_ffn_kernel, final_norm=final_norm),
        grid=(t // tm,),
        in_specs=[
            pl.BlockSpec((tm, d), lambda i: (i, 0)),
            _const_spec(g_ffn.shape),
            _const_spec(w_gate.shape),
            _const_spec(w_up.shape),
            _const_spec(w_down.shape),
            _const_spec(g_final.shape),
        ],
        out_specs=pl.BlockSpec((tm, d), lambda i: (i, 0)),
        out_shape=jax.ShapeDtypeStruct((t, d), F32),
        compiler_params=pltpu.CompilerParams(
            dimension_semantics=("arbitrary",),
            vmem_limit_bytes=VMEM_LIMIT),
        name="ffn",
    )(x1, g_ffn, w_gate, w_up, w_down, g_final)


def _rotary_tables(seq, tm):
    freqs = ROPE_BASE ** (-jnp.arange(0, QK_DIM, 2, dtype=F32) / QK_DIM)
    base = (jnp.arange(seq // tm, dtype=F32) * tm)[:, None] * freqs[None, :]
    off = jnp.arange(tm, dtype=F32)[:, None] * freqs[None, :]
    return jnp.cos(base), jnp.sin(base), jnp.cos(off), jnp.sin(off)


def kernel(x, g_mix, w_in, w_conv, dec_f, dec_b, g_ret, w_a_out, w_r_out, w_o,
           g_ffn, w_ff_gate, w_ff_up, w_ff_down, g_final):
    batch, seq, d = x.shape
    depth = w_in.shape[0]
    assert d == D_MODEL and seq % RET_CHUNK == 0
    x2 = x.reshape(batch * seq, d)
    tm_in = 256
    rot = _rotary_tables(seq, tm_in)
    for l in range(depth):
        lg = jnp.stack([jax.nn.log_sigmoid(dec_f[l].astype(F32)),
                        jax.nn.log_sigmoid(dec_b[l].astype(F32))])
        proj = _inproj(x2, g_mix[l][None, :], w_in[l].astype(BF16), rot,
                       g_ret[l][None, :].astype(F32), tm=tm_in)
        z = _retention(lg, proj, batch, seq, block_tokens=2048)
        x1 = _merge(x2, proj, z, w_conv[l], w_a_out[l].astype(BF16),
                    w_r_out[l].astype(BF16), w_o[l].astype(BF16), seq, tm=512)
        x2 = _ffn(x1, g_ffn[l][None, :], w_ff_gate[l].astype(BF16),
                  w_ff_up[l].astype(BF16), w_ff_down[l].astype(BF16),
                  g_final[None, :], final_norm=(l == depth - 1), tm=512)
    return x2.reshape(batch, seq, d)
```

```python
import functools

import jax
import jax.numpy as jnp
from jax import lax
from jax.experimental import pallas as pl
from jax.experimental.pallas import tpu as pltpu

F32 = jnp.float32
BF16 = jnp.bfloat16

D_MODEL = 1024
RET_HEADS = 4
QK_DIM = D_MODEL // RET_HEADS
V_DIM = 2 * QK_DIM
RET_V = RET_HEADS * V_DIM
ROPE_BASE = 10000.0
EPS = 1e-6
CONV_K = 3

SEC = 1024
W_XC, W_GB, W_GC, W_Q, W_K, W_V, W_GSW, W_GA, W_GR = 0, 1, 2, 3, 4, 5, 7, 9, 10
N_W_SEC = 11
SEC_XC, SEC_GB, SEC_GC, SEC_Q, SEC_V, SEC_GSW, SEC_GA, SEC_GR = 0, 1, 2, 3, 4, 6, 8, 9
N_OUT_SEC = 10

RET_CHUNK = 256
LANES = 128
BF16_SUBLANES = 16

VMEM_LIMIT = 56 * 1024 * 1024


def _sigmoid(x):
    return 1.0 / (1.0 + jnp.exp(-x))


def _rms_scale(x):
    return x * lax.rsqrt(jnp.mean(x * x, axis=-1, keepdims=True) + EPS)


def _const_spec(shape):
    return pl.BlockSpec(shape, lambda *_: (0,) * len(shape), pipeline_mode=pl.Buffered(1))


def _inproj_kernel(x_ref, g_ref, w_ref, cos_base_ref, sin_base_ref, cos_off_ref, sin_off_ref,
                   gret_ref, o_ref, kt_ref, *, n_pos_blocks):
    h = (_rms_scale(x_ref[...]) * g_ref[...]).astype(BF16)
    half = QK_DIM // 2
    pos_block = pl.ds(pl.program_id(0) % n_pos_blocks, 1)
    cos_b = cos_base_ref[pos_block, :]
    sin_b = sin_base_ref[pos_block, :]
    cos = cos_b * cos_off_ref[...] - sin_b * sin_off_ref[...]
    sin = sin_b * cos_off_ref[...] + cos_b * sin_off_ref[...]
    order = sorted(range(N_W_SEC), key=lambda s: (not (W_Q <= s <= W_K or s >= W_GSW), s))
    for sec in order:
        acc = jnp.dot(h, w_ref[:, sec * SEC:(sec + 1) * SEC], preferred_element_type=F32)
        out = (sec if sec < W_K else sec - 1) * SEC
        if sec in (W_Q, W_K):
            for hh in range(RET_HEADS):
                lo = hh * QK_DIM
                t1 = acc[:, lo:lo + half]
                t2 = acc[:, lo + half:lo + QK_DIM]
                r1 = t1 * cos - t2 * sin
                r2 = t1 * sin + t2 * cos
                if sec == W_Q:
                    scale = QK_DIM ** -0.5
                    o_ref[:, out + lo:out + lo + half] = (r1 * scale).astype(BF16)
                    o_ref[:, out + lo + half:out + lo + QK_DIM] = (r2 * scale).astype(BF16)
                else:
                    kt_ref[lo:lo + half, :] = r1.T.astype(BF16)
                    kt_ref[lo + half:lo + QK_DIM, :] = r2.T.astype(BF16)
        elif W_GSW <= sec < W_GA:
            gret = gret_ref[:, (sec - W_GSW) * SEC:(sec - W_GSW + 1) * SEC]
            o_ref[:, out:out + SEC] = (acc * _sigmoid(acc) * gret).astype(BF16)
        elif sec >= W_GA:
            o_ref[:, out:out + SEC] = _sigmoid(acc).astype(BF16)
        else:
            o_ref[:, out:out + SEC] = acc.astype(BF16)


def _inproj(x2, g_mix, w_in, rot, g_ret, tm):
    t, d = x2.shape
    cos_base, sin_base, cos_off, sin_off = rot
    assert cos_off.shape[0] == tm
    n_out = N_OUT_SEC * SEC
    return pl.pallas_call(
        functools.partial(_inproj_kernel, n_pos_blocks=cos_base.shape[0]),
        grid=(t // tm,),
        in_specs=[
            pl.BlockSpec((tm, d), lambda i: (i, 0)),
            _const_spec(g_mix.shape),
            _const_spec(w_in.shape),
            _const_spec(cos_base.shape),
            _const_spec(sin_base.shape),
            _const_spec(cos_off.shape),
            _const_spec(sin_off.shape),
            _const_spec(g_ret.shape),
        ],
        out_specs=[
            pl.BlockSpec((tm, n_out), lambda i: (i, 0)),
            pl.BlockSpec((RET_HEADS * QK_DIM, tm), lambda i: (0, i)),
        ],
        out_shape=[
            jax.ShapeDtypeStruct((t, n_out), BF16),
            jax.ShapeDtypeStruct((RET_HEADS * QK_DIM, t), BF16),
        ],
        compiler_params=pltpu.CompilerParams(
            dimension_semantics=("arbitrary",),
            vmem_limit_bytes=VMEM_LIMIT),
        name="inproj",
    )(x2, g_mix, w_in, cos_base, sin_base, cos_off, sin_off, g_ret)


def _retention_kernel(lg_ref, q_ref, ktf_ref, vf_ref, ktb_ref, vb_ref, z_ref,
                      decay_ref, xif_ref, xib_ref, zetaf_ref, zetab_ref,
                      fstate_ref, bstate_ref, rall_ref,
                      *, n_pairs, n_blocks, chunks_per_block):
    c = RET_CHUNK
    cpb = chunks_per_block
    n_chunks = n_blocks * cpb
    p = pl.program_id(0)
    j = pl.program_id(1)
    head_f = jnp.maximum(p - 1, 0) % RET_HEADS
    head_b = jnp.minimum(p, n_pairs - 1) % RET_HEADS
    lg_f = lg_ref[0, head_f]
    lg_fb = lg_ref[1, head_f]
    lg_b = lg_ref[1, head_b]
    has_fwd = p >= 1
    has_bwd = p < n_pairs

    @pl.when(jnp.logical_and(j == 0, has_fwd))
    def _():
        row = lax.broadcasted_iota(jnp.int32, (c, c), 0).astype(F32)
        col = lax.broadcasted_iota(jnp.int32, (c, c), 1).astype(F32)
        diff = row - col
        decay_ref[...] = jnp.where(diff >= 0.0,
                                   jnp.exp(jnp.maximum(diff, 0.0) * lg_f),
                                   jnp.exp(jnp.maximum(-diff, 0.0) * lg_fb))
        zetaf_ref[...] = jnp.exp((c - 1.0 - col) * lg_f).astype(BF16)
        idx = lax.broadcasted_iota(jnp.int32, (c, LANES), 0).astype(F32)
        xif_ref[...] = jnp.exp((idx + 1.0) * lg_f).astype(BF16)
        xib_ref[...] = jnp.exp((c - idx) * lg_fb).astype(BF16)
        fstate_ref[...] = jnp.zeros_like(fstate_ref)

    @pl.when(jnp.logical_and(j == 0, has_bwd))
    def _():
        col = lax.broadcasted_iota(jnp.int32, (c, c), 1).astype(F32)
        zetab_ref[...] = jnp.exp(col * lg_b).astype(BF16)
        bstate_ref[...] = jnp.zeros_like(bstate_ref)

    def row_scaled(a, scale_ref):
        scale = scale_ref[...]
        return jnp.concatenate(
            [a[:, n * LANES:(n + 1) * LANES] * scale
             for n in range(a.shape[1] // LANES)], axis=1)

    def decayed_kv(kt_ref, v_ref, zeta_ref, ci):
        rows = slice(ci * c, (ci + 1) * c)
        return jnp.dot(kt_ref[:, rows] * zeta_ref[...], v_ref[rows, :],
                       preferred_element_type=F32)

    def bwd_sweep():
        blk = n_blocks - 1 - j
        slot = (p % 2) * n_chunks
        chunk_decay = jnp.exp(jnp.zeros((1, V_DIM), F32) + c * lg_b)
        state = bstate_ref[...]
        for ci in reversed(range(cpb)):
            rall_ref[slot + blk * cpb + ci] = state.astype(BF16)
            state = chunk_decay * state + decayed_kv(ktb_ref, vb_ref, zetab_ref, ci)
        bstate_ref[...] = state

    def fwd_sweep():
        slot = ((p - 1) % 2) * n_chunks
        chunk_decay = jnp.exp(jnp.zeros((1, V_DIM), F32) + c * lg_f)

        def decayed_scores(ci):
            rows = slice(ci * c, (ci + 1) * c)
            scores = jnp.dot(q_ref[rows, :], ktf_ref[:, rows], preferred_element_type=F32)
            return (scores * decay_ref[...]).astype(BF16)

        state = fstate_ref[...]
        pmat_next = decayed_scores(0)
        kv_next = decayed_kv(ktf_ref, vf_ref, zetaf_ref, 0)
        for ci in range(cpb):
            rows = slice(ci * c, (ci + 1) * c)
            pmat, kv = pmat_next, kv_next
            if ci + 1 < cpb:
                pmat_next = decayed_scores(ci + 1)
                kv_next = decayed_kv(ktf_ref, vf_ref, zetaf_ref, ci + 1)
            q = q_ref[rows, :]
            lhs = jnp.concatenate([pmat, row_scaled(q, xib_ref), row_scaled(q, xif_ref)],
                                  axis=1)
            rhs = jnp.concatenate([vf_ref[rows, :], rall_ref[slot + j * cpb + ci],
                                   state.astype(BF16)], axis=0)
            ret = jnp.dot(lhs, rhs, preferred_element_type=F32)
            mu = jnp.mean(ret, axis=-1, keepdims=True)
            dev = ret - mu
            var = jnp.mean(dev * dev, axis=-1, keepdims=True)
            z_ref[rows, :] = (dev * lax.rsqrt(var + EPS)).astype(BF16)
            state = chunk_decay * state + kv
        fstate_ref[...] = state

    @pl.when(jnp.logical_not(has_fwd))
    def _():
        bwd_sweep()

    @pl.when(jnp.logical_and(has_fwd, has_bwd))
    def _():
        fwd_sweep()
        bwd_sweep()

    @pl.when(jnp.logical_not(has_bwd))
    def _():
        fwd_sweep()


def _retention(lg, proj, kt, batch, seq, block_tokens):
    c = RET_CHUNK
    nb = seq // block_tokens
    cpb = block_tokens // c
    nc = seq // c
    t = batch * seq
    n_pairs = batch * RET_HEADS
    q_col = (SEC_Q * SEC) // QK_DIM
    v_col = (SEC_V * SEC) // V_DIM

    def fwd_idx(p, j):
        pair = jnp.maximum(p - 1, 0)
        return pair // RET_HEADS, pair % RET_HEADS, jnp.where(p == 0, 0, j)

    def bwd_idx(p, j):
        pair = jnp.minimum(p, n_pairs - 1)
        return pair // RET_HEADS, pair % RET_HEADS, jnp.where(p == n_pairs, 0, nb - 1 - j)

    def tok_major(idx_fn, col0):
        def index_map(p, j):
            b, h, blk = idx_fn(p, j)
            return b * nb + blk, col0 + h
        return index_map

    def head_major(idx_fn):
        def index_map(p, j):
            b, h, blk = idx_fn(p, j)
            return h, b * nb + blk
        return index_map

    return pl.pallas_call(
        functools.partial(_retention_kernel, n_pairs=n_pairs, n_blocks=nb,
                          chunks_per_block=cpb),
        grid=(n_pairs + 1, nb),
        in_specs=[
            pl.BlockSpec(memory_space=pltpu.SMEM),
            pl.BlockSpec((block_tokens, QK_DIM), tok_major(fwd_idx, q_col)),
            pl.BlockSpec((QK_DIM, block_tokens), head_major(fwd_idx)),
            pl.BlockSpec((block_tokens, V_DIM), tok_major(fwd_idx, v_col)),
            pl.BlockSpec((QK_DIM, block_tokens), head_major(bwd_idx)),
            pl.BlockSpec((block_tokens, V_DIM), tok_major(bwd_idx, v_col)),
        ],
        out_specs=pl.BlockSpec((block_tokens, V_DIM), tok_major(fwd_idx, 0)),
        out_shape=jax.ShapeDtypeStruct((t, RET_V), BF16),
        scratch_shapes=[
            pltpu.VMEM((c, c), F32),
            pltpu.VMEM((c, LANES), BF16),
            pltpu.VMEM((c, LANES), BF16),
            pltpu.VMEM((QK_DIM, c), BF16),
            pltpu.VMEM((QK_DIM, c), BF16),
            pltpu.VMEM((QK_DIM, V_DIM), F32),
            pltpu.VMEM((QK_DIM, V_DIM), F32),
            pltpu.VMEM((2 * nc, QK_DIM, V_DIM), BF16),
        ],
        compiler_params=pltpu.CompilerParams(
            dimension_semantics=("arbitrary", "arbitrary"),
            vmem_limit_bytes=VMEM_LIMIT),
        name="retention",
    )(lg, proj, kt, proj, kt, proj)


def _merge_kernel(x_ref, xc_ref, gb_ref, gc_ref, xcp_ref, gcp_ref, xcn_ref, gcn_ref,
                  ga_ref, gr_ref, z_ref, gsw0_ref, gsw1_ref, wconv_ref, wa_ref, wr_ref, wo_ref,
                  o_ref, *, tiles_per_seq):
    i = pl.program_id(0)
    tm = x_ref.shape[0]
    u = gc_ref[...].astype(F32) * xc_ref[...].astype(F32)
    last = BF16_SUBLANES - 1
    u_before = (gcp_ref[last:last + 1, :].astype(F32) * xcp_ref[last:last + 1, :].astype(F32))
    u_after = gcn_ref[0:1, :].astype(F32) * xcn_ref[0:1, :].astype(F32)
    pos = i % tiles_per_seq
    u_before = u_before * (pos != 0).astype(F32)
    u_after = u_after * (pos != tiles_per_seq - 1).astype(F32)
    row = lax.broadcasted_iota(jnp.int32, u.shape, 0)
    u_prev = jnp.where(row == 0, u_before, pltpu.roll(u, 1, axis=0))
    u_next = jnp.where(row == tm - 1, u_after, pltpu.roll(u, tm - 1, axis=0))
    conv = u_prev * wconv_ref[0:1, :] + u * wconv_ref[1:2, :] + u_next * wconv_ref[2:3, :]
    a_in = (gb_ref[...].astype(F32) * conv).astype(BF16)
    y_a = jnp.dot(a_in, wa_ref[...], preferred_element_type=F32)
    r_in = jnp.concatenate([z_ref[:, :SEC] * gsw0_ref[...], z_ref[:, SEC:] * gsw1_ref[...]], axis=1)
    y_r = jnp.dot(r_in, wr_ref[...], preferred_element_type=F32)
    merged = ga_ref[...].astype(F32) * y_a + gr_ref[...].astype(F32) * y_r
    o_ref[...] = x_ref[...] + jnp.dot(merged.astype(BF16), wo_ref[...],
                                      preferred_element_type=F32)


def _merge(x2, proj, z, w_conv, w_a, w_r, w_o, seq, tm):
    t, d = x2.shape
    hb = tm // BF16_SUBLANES
    n_halo_blocks = t // BF16_SUBLANES

    def prev_block(i):
        return jnp.maximum(i * hb - 1, 0)

    def next_block(i):
        return jnp.minimum((i + 1) * hb, n_halo_blocks - 1)

    halo = (BF16_SUBLANES, SEC)
    return pl.pallas_call(
        functools.partial(_merge_kernel, tiles_per_seq=seq // tm),
        grid=(t // tm,),
        in_specs=[
            pl.BlockSpec((tm, d), lambda i: (i, 0)),
            pl.BlockSpec((tm, SEC), lambda i: (i, SEC_XC)),
            pl.BlockSpec((tm, SEC), lambda i: (i, SEC_GB)),
            pl.BlockSpec((tm, SEC), lambda i: (i, SEC_GC)),
            pl.BlockSpec(halo, lambda i: (prev_block(i), SEC_XC)),
            pl.BlockSpec(halo, lambda i: (prev_block(i), SEC_GC)),
            pl.BlockSpec(halo, lambda i: (next_block(i), SEC_XC)),
            pl.BlockSpec(halo, lambda i: (next_block(i), SEC_GC)),
            pl.BlockSpec((tm, SEC), lambda i: (i, SEC_GA)),
            pl.BlockSpec((tm, SEC), lambda i: (i, SEC_GR)),
            pl.BlockSpec((tm, RET_V), lambda i: (i, 0)),
            pl.BlockSpec((tm, SEC), lambda i: (i, SEC_GSW)),
            pl.BlockSpec((tm, SEC), lambda i: (i, SEC_GSW + 1)),
            _const_spec(w_conv.shape),
            _const_spec(w_a.shape),
            _const_spec(w_r.shape),
            _const_spec(w_o.shape),
        ],
        out_specs=pl.BlockSpec((tm, d), lambda i: (i, 0)),
        out_shape=jax.ShapeDtypeStruct((t, d), F32),
        compiler_params=pltpu.CompilerParams(
            dimension_semantics=("arbitrary",),
            vmem_limit_bytes=VMEM_LIMIT),
        name="merge",
    )(x2, proj, proj, proj, proj, proj, proj, proj, proj, proj, z, proj, proj,
      w_conv, w_a, w_r, w_o)


def _ffn_kernel(x_ref, gffn_ref, wg_ref, wu_ref, wd_ref, gfin_ref, o_ref, *, final_norm):
    x = x_ref[...]
    h2 = (_rms_scale(x) * gffn_ref[...]).astype(BF16)
    gate = jnp.dot(h2, wg_ref[...], preferred_element_type=F32)
    up = jnp.dot(h2, wu_ref[...], preferred_element_type=F32)
    act = (gate * _sigmoid(gate) * up).astype(BF16)
    y = x + jnp.dot(act, wd_ref[...], preferred_element_type=F32)
    o_ref[...] = _rms_scale(y) * gfin_ref[...] if final_norm else y


def _ffn(x1, g_ffn, w_gate, w_up, w_down, g_final, final_norm, tm):
    t, d = x1.shape
    return pl.pallas_call(
        functools.partial(_ffn_kernel, final_norm=final_norm),
        grid=(t // tm,),
        in_specs=[
            pl.BlockSpec((tm, d), lambda i: (i, 0)),
            _const_spec(g_ffn.shape),
            _const_spec(w_gate.shape),
            _const_spec(w_up.shape),
            _const_spec(w_down.shape),
            _const_spec(g_final.shape),
        ],
        out_specs=pl.BlockSpec((tm, d), lambda i: (i, 0)),
        out_shape=jax.ShapeDtypeStruct((t, d), F32),
        compiler_params=pltpu.CompilerParams(
            dimension_semantics=("arbitrary",),
            vmem_limit_bytes=VMEM_LIMIT),
        name="ffn",
    )(x1, g_ffn, w_gate, w_up, w_down, g_final)


def _rotary_tables(seq, tm):
    freqs = ROPE_BASE ** (-jnp.arange(0, QK_DIM, 2, dtype=F32) / QK_DIM)
    base = (jnp.arange(seq // tm, dtype=F32) * tm)[:, None] * freqs[None, :]
    off = jnp.arange(tm, dtype=F32)[:, None] * freqs[None, :]
    return jnp.cos(base), jnp.sin(base), jnp.cos(off), jnp.sin(off)


def kernel(x, g_mix, w_in, w_conv, dec_f, dec_b, g_ret, w_a_out, w_r_out, w_o,
           g_ffn, w_ff_gate, w_ff_up, w_ff_down, g_final):
    batch, seq, d = x.shape
    depth = w_in.shape[0]
    assert d == D_MODEL and seq % RET_CHUNK == 0
    x2 = x.reshape(batch * seq, d)
    tm_in = 512
    rot = _rotary_tables(seq, tm_in)
    for l in range(depth):
        lg = jnp.stack([jax.nn.log_sigmoid(dec_f[l].astype(F32)),
                        jax.nn.log_sigmoid(dec_b[l].astype(F32))])
        proj, kt = _inproj(x2, g_mix[l][None, :], w_in[l].astype(BF16), rot,
                           g_ret[l][None, :].astype(F32), tm=tm_in)
        z = _retention(lg, proj, kt, batch, seq, block_tokens=2048)
        x1 = _merge(x2, proj, z, w_conv[l], w_a_out[l].astype(BF16),
                    w_r_out[l].astype(BF16), w_o[l].astype(BF16), seq, tm=512)
        x2 = _ffn(x1, g_ffn[l][None, :], w_ff_gate[l].astype(BF16),
                  w_ff_up[l].astype(BF16), w_ff_down[l].astype(BF16),
                  g_final[None, :], final_norm=(l == depth - 1), tm=512)
    return x2.reshape(batch, seq, d)
```

```python
import functools

import jax
import jax.numpy as jnp
from jax import lax
from jax.experimental import pallas as pl
from jax.experimental.pallas import tpu as pltpu

F32 = jnp.float32
BF16 = jnp.bfloat16

D_MODEL = 1024
RET_HEADS = 4
QK_DIM = D_MODEL // RET_HEADS
V_DIM = 2 * QK_DIM
RET_V = RET_HEADS * V_DIM
ROPE_BASE = 10000.0
EPS = 1e-6
CONV_K = 3

SEC = 1024
W_XC, W_GB, W_GC, W_Q, W_K, W_V, W_GSW, W_GA, W_GR = 0, 1, 2, 3, 4, 5, 7, 9, 10
N_W_SEC = 11
SEC_XC, SEC_GB, SEC_GC, SEC_Q, SEC_V, SEC_GSW, SEC_GA, SEC_GR = 0, 1, 2, 3, 4, 6, 8, 9
N_OUT_SEC = 10

RET_CHUNK = 256
SIDE_CAST_STEPS = 16
MERGE_SUBBLOCKS = 4
LANES = 128
BF16_SUBLANES = 16

VMEM_LIMIT = 56 * 1024 * 1024


def _sigmoid(x):
    return 1.0 / (1.0 + jnp.exp(-x))


def _rms_scale(x):
    return x * lax.rsqrt(jnp.mean(x * x, axis=-1, keepdims=True) + EPS)


def _const_spec(shape):
    return pl.BlockSpec(shape, lambda *_: (0,) * len(shape), pipeline_mode=pl.Buffered(1))


def _inproj_kernel(x_ref, g_ref, w_ref, cos_base_ref, sin_base_ref, cos_off_ref, sin_off_ref,
                   gret_ref, *rest, n_pos_blocks, n_side):
    side_in = rest[:n_side]
    o_ref, kt_ref = rest[n_side:n_side + 2]
    side_jobs = list(zip(side_in, rest[n_side + 2:]))
    h = (_rms_scale(x_ref[...]) * g_ref[...]).astype(BF16)
    half = QK_DIM // 2
    pos_block = pl.ds(pl.program_id(0) % n_pos_blocks, 1)
    cos_b = cos_base_ref[pos_block, :]
    sin_b = sin_base_ref[pos_block, :]
    cos = cos_b * cos_off_ref[...] - sin_b * sin_off_ref[...]
    sin = sin_b * cos_off_ref[...] + cos_b * sin_off_ref[...]
    order = sorted(range(N_W_SEC), key=lambda s: (not (W_Q <= s <= W_K or s >= W_GSW), s))
    for sec in order:
        acc = jnp.dot(h, w_ref[:, sec * SEC:(sec + 1) * SEC], preferred_element_type=F32)
        out = (sec if sec < W_K else sec - 1) * SEC
        if sec in (W_Q, W_K):
            for hh in range(RET_HEADS):
                lo = hh * QK_DIM
                t1 = acc[:, lo:lo + half]
                t2 = acc[:, lo + half:lo + QK_DIM]
                r1 = t1 * cos - t2 * sin
                r2 = t1 * sin + t2 * cos
                if sec == W_Q:
                    scale = QK_DIM ** -0.5
                    o_ref[:, out + lo:out + lo + half] = (r1 * scale).astype(BF16)
                    o_ref[:, out + lo + half:out + lo + QK_DIM] = (r2 * scale).astype(BF16)
                else:
                    kt_ref[lo:lo + half, :] = r1.T.astype(BF16)
                    kt_ref[lo + half:lo + QK_DIM, :] = r2.T.astype(BF16)
        elif W_GSW <= sec < W_GA:
            gret = gret_ref[:, (sec - W_GSW) * SEC:(sec - W_GSW + 1) * SEC]
            o_ref[:, out:out + SEC] = (acc * _sigmoid(acc) * gret).astype(BF16)
        elif sec >= W_GA:
            o_ref[:, out:out + SEC] = _sigmoid(acc).astype(BF16)
        else:
            o_ref[:, out:out + SEC] = acc.astype(BF16)

    @pl.when(pl.program_id(0) < SIDE_CAST_STEPS)
    def _():
        for src, dst in side_jobs:
            dst[...] = src[...].astype(BF16)


def _inproj(x2, g_mix, w_in, rot, g_ret, side_weights, tm):
    t, d = x2.shape
    cos_base, sin_base, cos_off, sin_off = rot
    assert cos_off.shape[0] == tm
    n_out = N_OUT_SEC * SEC
    n_steps = t // tm
    assert n_steps >= SIDE_CAST_STEPS

    def slab_spec(w):
        rows = w.shape[0] // SIDE_CAST_STEPS
        assert rows * SIDE_CAST_STEPS == w.shape[0] and rows % BF16_SUBLANES == 0
        return pl.BlockSpec((rows, w.shape[1]),
                            lambda i: (jnp.minimum(i, SIDE_CAST_STEPS - 1), 0))

    side_specs = [slab_spec(w) for w in side_weights]
    outs = pl.pallas_call(
        functools.partial(_inproj_kernel, n_pos_blocks=cos_base.shape[0],
                          n_side=len(side_weights)),
        grid=(n_steps,),
        in_specs=[
            pl.BlockSpec((tm, d), lambda i: (i, 0)),
            _const_spec(g_mix.shape),
            _const_spec(w_in.shape),
            _const_spec(cos_base.shape),
            _const_spec(sin_base.shape),
            _const_spec(cos_off.shape),
            _const_spec(sin_off.shape),
            _const_spec(g_ret.shape),
        ] + side_specs,
        out_specs=[
            pl.BlockSpec((tm, n_out), lambda i: (i, 0)),
            pl.BlockSpec((RET_HEADS * QK_DIM, tm), lambda i: (0, i)),
        ] + side_specs,
        out_shape=[
            jax.ShapeDtypeStruct((t, n_out), BF16),
            jax.ShapeDtypeStruct((RET_HEADS * QK_DIM, t), BF16),
        ] + [jax.ShapeDtypeStruct(w.shape, BF16) for w in side_weights],
        compiler_params=pltpu.CompilerParams(
            dimension_semantics=("arbitrary",),
            vmem_limit_bytes=VMEM_LIMIT),
        name="inproj",
    )(x2, g_mix, w_in, cos_base, sin_base, cos_off, sin_off, g_ret, *side_weights)
    return outs[0], outs[1], outs[2:]


def _retention_kernel(lg_ref, q_ref, ktf_ref, vf_ref, ktb_ref, vb_ref, z_ref,
                      decay_ref, xif_ref, xib_ref, zetaf_ref, zetab_ref,
                      fstate_ref, bstate_ref, rall_ref,
                      *, n_pairs, n_blocks, chunks_per_block):
    c = RET_CHUNK
    cpb = chunks_per_block
    n_chunks = n_blocks * cpb
    p = pl.program_id(0)
    j = pl.program_id(1)
    head_f = jnp.maximum(p - 1, 0) % RET_HEADS
    head_b = jnp.minimum(p, n_pairs - 1) % RET_HEADS
    lg_f = lg_ref[0, head_f]
    lg_fb = lg_ref[1, head_f]
    lg_b = lg_ref[1, head_b]
    has_fwd = p >= 1
    has_bwd = p < n_pairs

    @pl.when(jnp.logical_and(j == 0, has_fwd))
    def _():
        row = lax.broadcasted_iota(jnp.int32, (c, c), 0).astype(F32)
        col = lax.broadcasted_iota(jnp.int32, (c, c), 1).astype(F32)
        diff = row - col
        decay_ref[...] = jnp.where(diff >= 0.0,
                                   jnp.exp(jnp.maximum(diff, 0.0) * lg_f),
                                   jnp.exp(jnp.maximum(-diff, 0.0) * lg_fb))
        zetaf_ref[...] = jnp.exp((c - 1.0 - col) * lg_f).astype(BF16)
        idx = lax.broadcasted_iota(jnp.int32, (c, LANES), 0).astype(F32)
        xif_ref[...] = jnp.exp((idx + 1.0) * lg_f).astype(BF16)
        xib_ref[...] = jnp.exp((c - idx) * lg_fb).astype(BF16)
        fstate_ref[...] = jnp.zeros_like(fstate_ref)

    @pl.when(jnp.logical_and(j == 0, has_bwd))
    def _():
        col = lax.broadcasted_iota(jnp.int32, (c, c), 1).astype(F32)
        zetab_ref[...] = jnp.exp(col * lg_b).astype(BF16)
        bstate_ref[...] = jnp.zeros_like(bstate_ref)

    def row_scaled(a, scale_ref):
        scale = scale_ref[...]
        return jnp.concatenate(
            [a[:, n * LANES:(n + 1) * LANES] * scale
             for n in range(a.shape[1] // LANES)], axis=1)

    def decayed_kv(kt_ref, v_ref, zeta_ref, ci):
        rows = slice(ci * c, (ci + 1) * c)
        return jnp.dot(kt_ref[:, rows] * zeta_ref[...], v_ref[rows, :],
                       preferred_element_type=F32)

    def bwd_sweep():
        blk = n_blocks - 1 - j
        slot = (p % 2) * n_chunks
        chunk_decay = jnp.exp(jnp.zeros((1, V_DIM), F32) + c * lg_b)
        state = bstate_ref[...]
        for ci in reversed(range(cpb)):
            rall_ref[slot + blk * cpb + ci] = state.astype(BF16)
            state = chunk_decay * state + decayed_kv(ktb_ref, vb_ref, zetab_ref, ci)
        bstate_ref[...] = state

    def fwd_sweep():
        slot = ((p - 1) % 2) * n_chunks
        chunk_decay = jnp.exp(jnp.zeros((1, V_DIM), F32) + c * lg_f)

        def decayed_scores(ci):
            rows = slice(ci * c, (ci + 1) * c)
            scores = jnp.dot(q_ref[rows, :], ktf_ref[:, rows], preferred_element_type=F32)
            return (scores * decay_ref[...]).astype(BF16)

        state = fstate_ref[...]
        pmat_next = decayed_scores(0)
        kv_next = decayed_kv(ktf_ref, vf_ref, zetaf_ref, 0)
        for ci in range(cpb):
            rows = slice(ci * c, (ci + 1) * c)
            pmat, kv = pmat_next, kv_next
            if ci + 1 < cpb:
                pmat_next = decayed_scores(ci + 1)
                kv_next = decayed_kv(ktf_ref, vf_ref, zetaf_ref, ci + 1)
            q = q_ref[rows, :]
            lhs = jnp.concatenate([pmat, row_scaled(q, xib_ref), row_scaled(q, xif_ref)],
                                  axis=1)
            rhs = jnp.concatenate([vf_ref[rows, :], rall_ref[slot + j * cpb + ci],
                                   state.astype(BF16)], axis=0)
            ret = jnp.dot(lhs, rhs, preferred_element_type=F32)
            mu = jnp.mean(ret, axis=-1, keepdims=True)
            dev = ret - mu
            var = jnp.mean(dev * dev, axis=-1, keepdims=True)
            z_ref[rows, :] = (dev * lax.rsqrt(var + EPS)).astype(BF16)
            state = chunk_decay * state + kv
        fstate_ref[...] = state

    @pl.when(jnp.logical_not(has_fwd))
    def _():
        bwd_sweep()

    @pl.when(jnp.logical_and(has_fwd, has_bwd))
    def _():
        fwd_sweep()
        bwd_sweep()

    @pl.when(jnp.logical_not(has_bwd))
    def _():
        fwd_sweep()


def _retention(lg, proj, kt, batch, seq, block_tokens):
    c = RET_CHUNK
    nb = seq // block_tokens
    cpb = block_tokens // c
    nc = seq // c
    t = batch * seq
    n_pairs = batch * RET_HEADS
    q_col = (SEC_Q * SEC) // QK_DIM
    v_col = (SEC_V * SEC) // V_DIM

    def fwd_idx(p, j):
        pair = jnp.maximum(p - 1, 0)
        return pair // RET_HEADS, pair % RET_HEADS, jnp.where(p == 0, 0, j)

    def bwd_idx(p, j):
        pair = jnp.minimum(p, n_pairs - 1)
        return pair // RET_HEADS, pair % RET_HEADS, jnp.where(p == n_pairs, 0, nb - 1 - j)

    def tok_major(idx_fn, col0):
        def index_map(p, j):
            b, h, blk = idx_fn(p, j)
            return b * nb + blk, col0 + h
        return index_map

    def head_major(idx_fn):
        def index_map(p, j):
            b, h, blk = idx_fn(p, j)
            return h, b * nb + blk
        return index_map

    return pl.pallas_call(
        functools.partial(_retention_kernel, n_pairs=n_pairs, n_blocks=nb,
                          chunks_per_block=cpb),
        grid=(n_pairs + 1, nb),
        in_specs=[
            pl.BlockSpec(memory_space=pltpu.SMEM),
            pl.BlockSpec((block_tokens, QK_DIM), tok_major(fwd_idx, q_col)),
            pl.BlockSpec((QK_DIM, block_tokens), head_major(fwd_idx)),
            pl.BlockSpec((block_tokens, V_DIM), tok_major(fwd_idx, v_col)),
            pl.BlockSpec((QK_DIM, block_tokens), head_major(bwd_idx)),
            pl.BlockSpec((block_tokens, V_DIM), tok_major(bwd_idx, v_col)),
        ],
        out_specs=pl.BlockSpec((block_tokens, V_DIM), tok_major(fwd_idx, 0)),
        out_shape=jax.ShapeDtypeStruct((t, RET_V), BF16),
        scratch_shapes=[
            pltpu.VMEM((c, c), F32),
            pltpu.VMEM((c, LANES), BF16),
            pltpu.VMEM((c, LANES), BF16),
            pltpu.VMEM((QK_DIM, c), BF16),
            pltpu.VMEM((QK_DIM, c), BF16),
            pltpu.VMEM((QK_DIM, V_DIM), F32),
            pltpu.VMEM((QK_DIM, V_DIM), F32),
            pltpu.VMEM((2 * nc, QK_DIM, V_DIM), BF16),
        ],
        compiler_params=pltpu.CompilerParams(
            dimension_semantics=("arbitrary", "arbitrary"),
            vmem_limit_bytes=VMEM_LIMIT),
        name="retention",
    )(lg, proj, kt, proj, kt, proj)


def _merge_kernel(x_ref, xc_ref, gb_ref, gc_ref, xcp_ref, gcp_ref, xcn_ref, gcn_ref,
                  ga_ref, gr_ref, z_ref, gsw0_ref, gsw1_ref, wconv_ref, wa_ref, wr_ref, wo_ref,
                  o_ref, *, tiles_per_seq):
    i = pl.program_id(0)
    tm = x_ref.shape[0]
    u = gc_ref[...].astype(F32) * xc_ref[...].astype(F32)
    last = BF16_SUBLANES - 1
    u_before = (gcp_ref[last:last + 1, :].astype(F32) * xcp_ref[last:last + 1, :].astype(F32))
    u_after = gcn_ref[0:1, :].astype(F32) * xcn_ref[0:1, :].astype(F32)
    pos = i % tiles_per_seq
    u_before = u_before * (pos != 0).astype(F32)
    u_after = u_after * (pos != tiles_per_seq - 1).astype(F32)
    row = lax.broadcasted_iota(jnp.int32, u.shape, 0)
    u_prev = jnp.where(row == 0, u_before, pltpu.roll(u, 1, axis=0))
    u_next = jnp.where(row == tm - 1, u_after, pltpu.roll(u, tm - 1, axis=0))
    conv = u_prev * wconv_ref[0:1, :] + u * wconv_ref[1:2, :] + u_next * wconv_ref[2:3, :]
    a_in = (gb_ref[...].astype(F32) * conv).astype(BF16)

    def branch_outputs(rows):
        r_in = jnp.concatenate([z_ref[rows, :SEC] * gsw0_ref[rows, :],
                                z_ref[rows, SEC:] * gsw1_ref[rows, :]], axis=1)
        y_r = jnp.dot(r_in, wr_ref[...], preferred_element_type=F32)
        y_a = jnp.dot(a_in[rows, :], wa_ref[...], preferred_element_type=F32)
        return y_a, y_r

    sub = tm // MERGE_SUBBLOCKS
    blocks = [slice(s * sub, (s + 1) * sub) for s in range(MERGE_SUBBLOCKS)]
    y_next = branch_outputs(blocks[0])
    for s, rows in enumerate(blocks):
        y_a, y_r = y_next
        if s + 1 < len(blocks):
            y_next = branch_outputs(blocks[s + 1])
        merged = ga_ref[rows, :].astype(F32) * y_a + gr_ref[rows, :].astype(F32) * y_r
        o_ref[rows, :] = x_ref[rows, :] + jnp.dot(merged.astype(BF16), wo_ref[...],
                                                  preferred_element_type=F32)


def _merge(x2, proj, z, w_conv, w_a, w_r, w_o, seq, tm):
    t, d = x2.shape
    hb = tm // BF16_SUBLANES
    n_halo_blocks = t // BF16_SUBLANES

    def prev_block(i):
        return jnp.maximum(i * hb - 1, 0)

    def next_block(i):
        return jnp.minimum((i + 1) * hb, n_halo_blocks - 1)

    halo = (BF16_SUBLANES, SEC)
    return pl.pallas_call(
        functools.partial(_merge_kernel, tiles_per_seq=seq // tm),
        grid=(t // tm,),
        in_specs=[
            pl.BlockSpec((tm, d), lambda i: (i, 0)),
            pl.BlockSpec((tm, SEC), lambda i: (i, SEC_XC)),
            pl.BlockSpec((tm, SEC), lambda i: (i, SEC_GB)),
            pl.BlockSpec((tm, SEC), lambda i: (i, SEC_GC)),
            pl.BlockSpec(halo, lambda i: (prev_block(i), SEC_XC)),
            pl.BlockSpec(halo, lambda i: (prev_block(i), SEC_GC)),
            pl.BlockSpec(halo, lambda i: (next_block(i), SEC_XC)),
            pl.BlockSpec(halo, lambda i: (next_block(i), SEC_GC)),
            pl.BlockSpec((tm, SEC), lambda i: (i, SEC_GA)),
            pl.BlockSpec((tm, SEC), lambda i: (i, SEC_GR)),
            pl.BlockSpec((tm, RET_V), lambda i: (i, 0)),
            pl.BlockSpec((tm, SEC), lambda i: (i, SEC_GSW)),
            pl.BlockSpec((tm, SEC), lambda i: (i, SEC_GSW + 1)),
            _const_spec(w_conv.shape),
            _const_spec(w_a.shape),
            _const_spec(w_r.shape),
            _const_spec(w_o.shape),
        ],
        out_specs=pl.BlockSpec((tm, d), lambda i: (i, 0)),
        out_shape=jax.ShapeDtypeStruct((t, d), F32),
        compiler_params=pltpu.CompilerParams(
            dimension_semantics=("arbitrary",),
            vmem_limit_bytes=VMEM_LIMIT),
        name="merge",
    )(x2, proj, proj, proj, proj, proj, proj, proj, proj, proj, z, proj, proj,
      w_conv, w_a, w_r, w_o)


def _ffn_kernel(x_ref, gffn_ref, wg_ref, wu_ref, wd_ref, gfin_ref, o_ref, *, final_norm):
    def gate_up(rows):
        h2 = (_rms_scale(x_ref[rows, :]) * gffn_ref[...]).astype(BF16)
        return (jnp.dot(h2, wg_ref[...], preferred_element_type=F32),
                jnp.dot(h2, wu_ref[...], preferred_element_type=F32))

    sub = x_ref.shape[0] // MERGE_SUBBLOCKS
    blocks = [slice(s * sub, (s + 1) * sub) for s in range(MERGE_SUBBLOCKS)]
    gu_next = gate_up(blocks[0])
    for s, rows in enumerate(blocks):
        gate, up = gu_next
        if s + 1 < len(blocks):
            gu_next = gate_up(blocks[s + 1])
        act = (gate * _sigmoid(gate) * up).astype(BF16)
        y = x_ref[rows, :] + jnp.dot(act, wd_ref[...], preferred_element_type=F32)
        o_ref[rows, :] = _rms_scale(y) * gfin_ref[...] if final_norm else y


def _ffn(x1, g_ffn, w_gate, w_up, w_down, g_final, final_norm, tm):
    t, d = x1.shape
    return pl.pallas_call(
        functools.partial(_ffn_kernel, final_norm=final_norm),
        grid=(t // tm,),
        in_specs=[
            pl.BlockSpec((tm, d), lambda i: (i, 0)),
            _const_spec(g_ffn.shape),
            _const_spec(w_gate.shape),
            _const_spec(w_up.shape),
            _const_spec(w_down.shape),
            _const_spec(g_final.shape),
        ],
        out_specs=pl.BlockSpec((tm, d), lambda i: (i, 0)),
        out_shape=jax.ShapeDtypeStruct((t, d), F32),
        compiler_params=pltpu.CompilerParams(
            dimension_semantics=("arbitrary",),
            vmem_limit_bytes=VMEM_LIMIT),
        name="ffn",
    )(x1, g_ffn, w_gate, w_up, w_down, g_final)


def _rotary_tables(seq, tm):
    freqs = ROPE_BASE ** (-jnp.arange(0, QK_DIM, 2, dtype=F32) / QK_DIM)
    base = (jnp.arange(seq // tm, dtype=F32) * tm)[:, None] * freqs[None, :]
    off = jnp.arange(tm, dtype=F32)[:, None] * freqs[None, :]
    return jnp.cos(base), jnp.sin(base), jnp.cos(off), jnp.sin(off)


def kernel(x, g_mix, w_in, w_conv, dec_f, dec_b, g_ret, w_a_out, w_r_out, w_o,
           g_ffn, w_ff_gate, w_ff_up, w_ff_down, g_final):
    batch, seq, d = x.shape
    depth = w_in.shape[0]
    assert d == D_MODEL and seq % RET_CHUNK == 0
    x2 = x.reshape(batch * seq, d)
    tm_in = 256
    rot = _rotary_tables(seq, tm_in)
    for l in range(depth):
        lg = jnp.stack([jax.nn.log_sigmoid(dec_f[l].astype(F32)),
                        jax.nn.log_sigmoid(dec_b[l].astype(F32))])
        side = [w_a_out[l], w_r_out[l], w_o[l], w_ff_gate[l], w_ff_up[l], w_ff_down[l]]
        proj, kt, (w_a, w_r, w_ob, w_gate, w_up, w_down) = _inproj(
            x2, g_mix[l][None, :], w_in[l].astype(BF16), rot,
            g_ret[l][None, :].astype(F32), side, tm=tm_in)
        z = _retention(lg, proj, kt, batch, seq, block_tokens=2048)
        x1 = _merge(x2, proj, z, w_conv[l], w_a, w_r, w_ob, seq, tm=512)
        x2 = _ffn(x1, g_ffn[l][None, :], w_gate, w_up, w_down,
                  g_final[None, :], final_norm=(l == depth - 1), tm=512)
    return x2.reshape(batch, seq, d)
```

```python
import functools

import jax
import jax.numpy as jnp
from jax import lax
from jax.experimental import pallas as pl
from jax.experimental.pallas import tpu as pltpu

F32 = jnp.float32
BF16 = jnp.bfloat16

D_MODEL = 1024
RET_HEADS = 4
QK_DIM = D_MODEL // RET_HEADS
V_DIM = 2 * QK_DIM
RET_V = RET_HEADS * V_DIM
ROPE_BASE = 10000.0
EPS = 1e-6
CONV_K = 3

SEC = 1024
W_XC, W_GB, W_GC, W_Q, W_K, W_V, W_GSW, W_GA, W_GR, N_W_SEC = 0, 1, 2, 3, 4, 5, 7, 9, 10, 11
SEC_U, SEC_GB, SEC_GSW, SEC_V, SEC_Q, SEC_GA, SEC_GR, N_OUT_SEC = 0, 1, 2, 4, 6, 7, 8, 9

RET_CHUNK = 256
SIDE_CAST_STEPS = 16
ROW_SUBBLOCKS = 2
LANES = 128
BF16_SUBLANES = 16

VMEM_LIMIT = 56 * 1024 * 1024


def _sigmoid(x):
    return 1.0 / (1.0 + jnp.exp(-x))


def _rms_scale(x):
    return x * lax.rsqrt(jnp.mean(x * x, axis=-1, keepdims=True) + EPS)


def _const_spec(shape):
    return pl.BlockSpec(shape, lambda *_: (0,) * len(shape), pipeline_mode=pl.Buffered(1))


def _inproj_kernel(x_ref, g_ref, w_ref, cos_base_ref, sin_base_ref, cos_off_ref, sin_off_ref,
                   gret_ref, *rest, n_pos_blocks, n_side):
    side_in = rest[:n_side]
    o_ref, kt_ref = rest[n_side:n_side + 2]
    side_out = rest[n_side + 2:]
    h = (_rms_scale(x_ref[...]) * g_ref[...]).astype(BF16)
    half = QK_DIM // 2
    pos_block = pl.ds(pl.program_id(0) % n_pos_blocks, 1)
    cos_b = cos_base_ref[pos_block, :]
    sin_b = sin_base_ref[pos_block, :]
    cos = cos_b * cos_off_ref[...] - sin_b * sin_off_ref[...]
    sin = sin_b * cos_off_ref[...] + cos_b * sin_off_ref[...]

    def project(first_sec, n_sec):
        return jnp.dot(h, w_ref[:, first_sec * SEC:(first_sec + n_sec) * SEC],
                       preferred_element_type=F32)

    def out_cols(sec, n_sec=1):
        return slice(sec * SEC, (sec + n_sec) * SEC)

    qk = project(W_Q, 2)
    for sec in range(2):
        for hh in range(RET_HEADS):
            lo = sec * SEC + hh * QK_DIM
            t1 = qk[:, lo:lo + half]
            t2 = qk[:, lo + half:lo + QK_DIM]
            r1 = t1 * cos - t2 * sin
            r2 = t1 * sin + t2 * cos
            if sec == 0:
                scale = QK_DIM ** -0.5
                out = SEC_Q * SEC + hh * QK_DIM
                o_ref[:, out:out + half] = (r1 * scale).astype(BF16)
                o_ref[:, out + half:out + QK_DIM] = (r2 * scale).astype(BF16)
            else:
                out = hh * QK_DIM
                kt_ref[out:out + half, :] = r1.T.astype(BF16)
                kt_ref[out + half:out + QK_DIM, :] = r2.T.astype(BF16)

    gsw = project(W_GSW, 2)
    o_ref[:, out_cols(SEC_GSW, 2)] = (gsw * _sigmoid(gsw) * gret_ref[...]).astype(BF16)

    gates = project(W_GA, 2)
    o_ref[:, out_cols(SEC_GA, 2)] = _sigmoid(gates).astype(BF16)

    conv_in = project(W_XC, 3)
    o_ref[:, out_cols(SEC_U)] = (conv_in[:, out_cols(W_GC)] * conv_in[:, out_cols(W_XC)]
                                 ).astype(BF16)
    o_ref[:, out_cols(SEC_GB)] = conv_in[:, out_cols(W_GB)].astype(BF16)

    o_ref[:, out_cols(SEC_V, 2)] = project(W_V, 2).astype(BF16)

    @pl.when(pl.program_id(0) < SIDE_CAST_STEPS)
    def _():
        for src, dst in zip(side_in, side_out):
            dst[...] = src[...].astype(BF16)


def _inproj(x2, g_mix, w_in, rot, g_ret, side_weights, tm):
    t, d = x2.shape
    cos_base, sin_base, cos_off, sin_off = rot
    assert cos_off.shape[0] == tm and w_in.shape[1] == N_W_SEC * SEC
    n_out = N_OUT_SEC * SEC
    n_steps = t // tm
    assert n_steps >= SIDE_CAST_STEPS

    def slab_spec(w):
        rows = w.shape[0] // SIDE_CAST_STEPS
        assert rows * SIDE_CAST_STEPS == w.shape[0] and rows % BF16_SUBLANES == 0
        return pl.BlockSpec((rows, w.shape[1]),
                            lambda i: (jnp.minimum(i, SIDE_CAST_STEPS - 1), 0))

    side_specs = [slab_spec(w) for w in side_weights]
    outs = pl.pallas_call(
        functools.partial(_inproj_kernel, n_pos_blocks=cos_base.shape[0],
                          n_side=len(side_weights)),
        grid=(n_steps,),
        in_specs=[
            pl.BlockSpec((tm, d), lambda i: (i, 0)),
            _const_spec(g_mix.shape),
            _const_spec(w_in.shape),
            _const_spec(cos_base.shape),
            _const_spec(sin_base.shape),
            _const_spec(cos_off.shape),
            _const_spec(sin_off.shape),
            _const_spec(g_ret.shape),
        ] + side_specs,
        out_specs=[
            pl.BlockSpec((tm, n_out), lambda i: (i, 0)),
            pl.BlockSpec((RET_HEADS * QK_DIM, tm), lambda i: (0, i)),
        ] + side_specs,
        out_shape=[
            jax.ShapeDtypeStruct((t, n_out), BF16),
            jax.ShapeDtypeStruct((RET_HEADS * QK_DIM, t), BF16),
        ] + [jax.ShapeDtypeStruct(w.shape, BF16) for w in side_weights],
        compiler_params=pltpu.CompilerParams(
            dimension_semantics=("arbitrary",),
            vmem_limit_bytes=VMEM_LIMIT),
        name="inproj",
    )(x2, g_mix, w_in, cos_base, sin_base, cos_off, sin_off, g_ret, *side_weights)
    return outs[0], outs[1], outs[2:]


def _retention_kernel(lg_ref, q_ref, ktf_ref, vf_ref, ktb_ref, vb_ref, z_ref,
                      decay_ref, xif_ref, xib_ref, zetaf_ref, zetab_ref,
                      fstate_ref, bstate_ref, rall_ref,
                      *, n_pairs, n_blocks, chunks_per_block):
    c = RET_CHUNK
    cpb = chunks_per_block
    n_chunks = n_blocks * cpb
    p = pl.program_id(0)
    j = pl.program_id(1)
    head_f = jnp.maximum(p - 1, 0) % RET_HEADS
    head_b = jnp.minimum(p, n_pairs - 1) % RET_HEADS
    lg_f = lg_ref[0, head_f]
    lg_fb = lg_ref[1, head_f]
    lg_b = lg_ref[1, head_b]
    has_fwd = p >= 1
    has_bwd = p < n_pairs

    @pl.when(jnp.logical_and(j == 0, has_fwd))
    def _():
        row = lax.broadcasted_iota(jnp.int32, (c, c), 0).astype(F32)
        col = lax.broadcasted_iota(jnp.int32, (c, c), 1).astype(F32)
        diff = row - col
        decay_ref[...] = jnp.where(diff >= 0.0,
                                   jnp.exp(jnp.maximum(diff, 0.0) * lg_f),
                                   jnp.exp(jnp.maximum(-diff, 0.0) * lg_fb))
        zetaf_ref[...] = jnp.exp((c - 1.0 - col) * lg_f).astype(BF16)
        idx = lax.broadcasted_iota(jnp.int32, (c, LANES), 0).astype(F32)
        xif_ref[...] = jnp.exp((idx + 1.0) * lg_f).astype(BF16)
        xib_ref[...] = jnp.exp((c - idx) * lg_fb).astype(BF16)
        fstate_ref[...] = jnp.zeros_like(fstate_ref)

    @pl.when(jnp.logical_and(j == 0, has_bwd))
    def _():
        col = lax.broadcasted_iota(jnp.int32, (c, c), 1).astype(F32)
        zetab_ref[...] = jnp.exp(col * lg_b).astype(BF16)
        bstate_ref[...] = jnp.zeros_like(bstate_ref)

    def row_scaled(a, scale_ref):
        scale = scale_ref[...]
        return jnp.concatenate(
            [a[:, n * LANES:(n + 1) * LANES] * scale
             for n in range(a.shape[1] // LANES)], axis=1)

    def decayed_kv(kt_ref, v_ref, zeta_ref, ci):
        rows = slice(ci * c, (ci + 1) * c)
        return jnp.dot(kt_ref[:, rows] * zeta_ref[...], v_ref[rows, :],
                       preferred_element_type=F32)

    def bwd_sweep():
        blk = n_blocks - 1 - j
        slot = (p % 2) * n_chunks
        chunk_decay = jnp.exp(jnp.zeros((1, V_DIM), F32) + c * lg_b)
        state = bstate_ref[...]
        for ci in reversed(range(cpb)):
            rall_ref[slot + blk * cpb + ci] = state.astype(BF16)
            state = chunk_decay * state + decayed_kv(ktb_ref, vb_ref, zetab_ref, ci)
        bstate_ref[...] = state

    def fwd_sweep():
        slot = ((p - 1) % 2) * n_chunks
        chunk_decay = jnp.exp(jnp.zeros((1, V_DIM), F32) + c * lg_f)

        def decayed_scores(ci):
            rows = slice(ci * c, (ci + 1) * c)
            scores = jnp.dot(q_ref[rows, :], ktf_ref[:, rows], preferred_element_type=F32)
            return (scores * decay_ref[...]).astype(BF16)

        state = fstate_ref[...]
        pmat_next = decayed_scores(0)
        kv_next = decayed_kv(ktf_ref, vf_ref, zetaf_ref, 0)
        for ci in range(cpb):
            rows = slice(ci * c, (ci + 1) * c)
            pmat, kv = pmat_next, kv_next
            if ci + 1 < cpb:
                pmat_next = decayed_scores(ci + 1)
                kv_next = decayed_kv(ktf_ref, vf_ref, zetaf_ref, ci + 1)
            q = q_ref[rows, :]
            lhs = jnp.concatenate([pmat, row_scaled(q, xib_ref), row_scaled(q, xif_ref)],
                                  axis=1)
            rhs = jnp.concatenate([vf_ref[rows, :], rall_ref[slot + j * cpb + ci],
                                   state.astype(BF16)], axis=0)
            ret = jnp.dot(lhs, rhs, preferred_element_type=F32)
            mu = jnp.mean(ret, axis=-1, keepdims=True)
            dev = ret - mu
            var = jnp.mean(dev * dev, axis=-1, keepdims=True)
            z_ref[rows, :] = (dev * lax.rsqrt(var + EPS)).astype(BF16)
            state = chunk_decay * state + kv
        fstate_ref[...] = state

    @pl.when(jnp.logical_not(has_fwd))
    def _():
        bwd_sweep()

    @pl.when(jnp.logical_and(has_fwd, has_bwd))
    def _():
        fwd_sweep()
        bwd_sweep()

    @pl.when(jnp.logical_not(has_bwd))
    def _():
        fwd_sweep()


def _retention(lg, proj, kt, batch, seq, block_tokens):
    c = RET_CHUNK
    nb = seq // block_tokens
    cpb = block_tokens // c
    nc = seq // c
    t = batch * seq
    n_pairs = batch * RET_HEADS
    q_col = (SEC_Q * SEC) // QK_DIM
    v_col = (SEC_V * SEC) // V_DIM

    def fwd_idx(p, j):
        pair = jnp.maximum(p - 1, 0)
        return pair // RET_HEADS, pair % RET_HEADS, jnp.where(p == 0, 0, j)

    def bwd_idx(p, j):
        pair = jnp.minimum(p, n_pairs - 1)
        return pair // RET_HEADS, pair % RET_HEADS, jnp.where(p == n_pairs, 0, nb - 1 - j)

    def tok_major(idx_fn, col0):
        def index_map(p, j):
            b, h, blk = idx_fn(p, j)
            return b * nb + blk, col0 + h
        return index_map

    def head_major(idx_fn):
        def index_map(p, j):
            b, h, blk = idx_fn(p, j)
            return h, b * nb + blk
        return index_map

    return pl.pallas_call(
        functools.partial(_retention_kernel, n_pairs=n_pairs, n_blocks=nb,
                          chunks_per_block=cpb),
        grid=(n_pairs + 1, nb),
        in_specs=[
            pl.BlockSpec(memory_space=pltpu.SMEM),
            pl.BlockSpec((block_tokens, QK_DIM), tok_major(fwd_idx, q_col)),
            pl.BlockSpec((QK_DIM, block_tokens), head_major(fwd_idx)),
            pl.BlockSpec((block_tokens, V_DIM), tok_major(fwd_idx, v_col)),
            pl.BlockSpec((QK_DIM, block_tokens), head_major(bwd_idx)),
            pl.BlockSpec((block_tokens, V_DIM), tok_major(bwd_idx, v_col)),
        ],
        out_specs=pl.BlockSpec((block_tokens, V_DIM), tok_major(fwd_idx, 0)),
        out_shape=jax.ShapeDtypeStruct((t, RET_V), BF16),
        scratch_shapes=[
            pltpu.VMEM((c, c), F32),
            pltpu.VMEM((c, LANES), BF16),
            pltpu.VMEM((c, LANES), BF16),
            pltpu.VMEM((QK_DIM, c), BF16),
            pltpu.VMEM((QK_DIM, c), BF16),
            pltpu.VMEM((QK_DIM, V_DIM), F32),
            pltpu.VMEM((QK_DIM, V_DIM), F32),
            pltpu.VMEM((2 * nc, QK_DIM, V_DIM), BF16),
        ],
        compiler_params=pltpu.CompilerParams(
            dimension_semantics=("arbitrary", "arbitrary"),
            vmem_limit_bytes=VMEM_LIMIT),
        name="retention",
    )(lg, proj, kt, proj, kt, proj)


def _merge_kernel(x_ref, u_ref, gb_ref, up_ref, un_ref, ga_ref, gr_ref, z_ref, gsw_ref,
                  wconv_ref, wa_ref, wr_ref, wo_ref, o_ref, *, tiles_per_seq):
    i = pl.program_id(0)
    tm = x_ref.shape[0]
    u = u_ref[...].astype(F32)
    pos = i % tiles_per_seq
    last = BF16_SUBLANES - 1
    u_before = up_ref[last:last + 1, :].astype(F32) * (pos != 0).astype(F32)
    u_after = un_ref[0:1, :].astype(F32) * (pos != tiles_per_seq - 1).astype(F32)
    row = lax.broadcasted_iota(jnp.int32, u.shape, 0)
    u_prev = jnp.where(row == 0, u_before, pltpu.roll(u, 1, axis=0))
    u_next = jnp.where(row == tm - 1, u_after, pltpu.roll(u, tm - 1, axis=0))
    conv = u_prev * wconv_ref[0:1, :] + u * wconv_ref[1:2, :] + u_next * wconv_ref[2:3, :]
    a_in = (gb_ref[...].astype(F32) * conv).astype(BF16)

    def branch_outputs(rows):
        y_r = jnp.dot(z_ref[rows, :] * gsw_ref[rows, :], wr_ref[...],
                      preferred_element_type=F32)
        y_a = jnp.dot(a_in[rows, :], wa_ref[...], preferred_element_type=F32)
        return y_a, y_r

    sub = tm // ROW_SUBBLOCKS
    blocks = [slice(s * sub, (s + 1) * sub) for s in range(ROW_SUBBLOCKS)]
    y_next = branch_outputs(blocks[0])
    for s, rows in enumerate(blocks):
        y_a, y_r = y_next
        if s + 1 < len(blocks):
            y_next = branch_outputs(blocks[s + 1])
        merged = ga_ref[rows, :].astype(F32) * y_a + gr_ref[rows, :].astype(F32) * y_r
        o_ref[rows, :] = x_ref[rows, :] + jnp.dot(merged.astype(BF16), wo_ref[...],
                                                  preferred_element_type=F32)


def _merge(x2, proj, z, w_conv, w_a, w_r, w_o, seq, tm):
    t, d = x2.shape
    hb = tm // BF16_SUBLANES
    n_halo_blocks = t // BF16_SUBLANES
    halo = (BF16_SUBLANES, SEC)
    return pl.pallas_call(
        functools.partial(_merge_kernel, tiles_per_seq=seq // tm),
        grid=(t // tm,),
        in_specs=[
            pl.BlockSpec((tm, d), lambda i: (i, 0)),
            pl.BlockSpec((tm, SEC), lambda i: (i, SEC_U)),
            pl.BlockSpec((tm, SEC), lambda i: (i, SEC_GB)),
            pl.BlockSpec(halo, lambda i: (jnp.maximum(i * hb - 1, 0), SEC_U)),
            pl.BlockSpec(halo, lambda i: (jnp.minimum((i + 1) * hb, n_halo_blocks - 1), SEC_U)),
            pl.BlockSpec((tm, SEC), lambda i: (i, SEC_GA)),
            pl.BlockSpec((tm, SEC), lambda i: (i, SEC_GR)),
            pl.BlockSpec((tm, RET_V), lambda i: (i, 0)),
            pl.BlockSpec((tm, RET_V), lambda i: (i, (SEC_GSW * SEC) // RET_V)),
            _const_spec(w_conv.shape),
            _const_spec(w_a.shape),
            _const_spec(w_r.shape),
            _const_spec(w_o.shape),
        ],
        out_specs=pl.BlockSpec((tm, d), lambda i: (i, 0)),
        out_shape=jax.ShapeDtypeStruct((t, d), F32),
        compiler_params=pltpu.CompilerParams(
            dimension_semantics=("arbitrary",),
            vmem_limit_bytes=VMEM_LIMIT),
        name="merge",
    )(x2, proj, proj, proj, proj, proj, proj, z, proj, w_conv, w_a, w_r, w_o)


def _ffn_kernel(x_ref, gffn_ref, wg_ref, wu_ref, wd_ref, gfin_ref, o_ref, *, final_norm):
    def gate_up(rows):
        h2 = (_rms_scale(x_ref[rows, :]) * gffn_ref[...]).astype(BF16)
        return (jnp.dot(h2, wg_ref[...], preferred_element_type=F32),
                jnp.dot(h2, wu_ref[...], preferred_element_type=F32))

    sub = x_ref.shape[0] // ROW_SUBBLOCKS
    blocks = [slice(s * sub, (s + 1) * sub) for s in range(ROW_SUBBLOCKS)]
    gu_next = gate_up(blocks[0])
    for s, rows in enumerate(blocks):
        gate, up = gu_next
        if s + 1 < len(blocks):
            gu_next = gate_up(blocks[s + 1])
        act = (gate * _sigmoid(gate) * up).astype(BF16)
        y = x_ref[rows, :] + jnp.dot(act, wd_ref[...], preferred_element_type=F32)
        o_ref[rows, :] = _rms_scale(y) * gfin_ref[...] if final_norm else y


def _ffn(x1, g_ffn, w_gate, w_up, w_down, g_final, final_norm, tm):
    t, d = x1.shape
    return pl.pallas_call(
        functools.partial(_ffn_kernel, final_norm=final_norm),
        grid=(t // tm,),
        in_specs=[
            pl.BlockSpec((tm, d), lambda i: (i, 0)),
            _const_spec(g_ffn.shape),
            _const_spec(w_gate.shape),
            _const_spec(w_up.shape),
            _const_spec(w_down.shape),
            _const_spec(g_final.shape),
        ],
        out_specs=pl.BlockSpec((tm, d), lambda i: (i, 0)),
        out_shape=jax.ShapeDtypeStruct((t, d), F32),
        compiler_params=pltpu.CompilerParams(
            dimension_semantics=("arbitrary",),
            vmem_limit_bytes=VMEM_LIMIT),
        name="ffn",
    )(x1, g_ffn, w_gate, w_up, w_down, g_final)


def _rotary_tables(seq, tm):
    freqs = ROPE_BASE ** (-jnp.arange(0, QK_DIM, 2, dtype=F32) / QK_DIM)
    base = (jnp.arange(seq // tm, dtype=F32) * tm)[:, None] * freqs[None, :]
    off = jnp.arange(tm, dtype=F32)[:, None] * freqs[None, :]
    return jnp.cos(base), jnp.sin(base), jnp.cos(off), jnp.sin(off)


def kernel(x, g_mix, w_in, w_conv, dec_f, dec_b, g_ret, w_a_out, w_r_out, w_o,
           g_ffn, w_ff_gate, w_ff_up, w_ff_down, g_final):
    batch, seq, d = x.shape
    depth = w_in.shape[0]
    assert d == D_MODEL and seq % RET_CHUNK == 0
    x2 = x.reshape(batch * seq, d)
    tm_in = 256
    rot = _rotary_tables(seq, tm_in)
    for l in range(depth):
        lg = jnp.stack([jax.nn.log_sigmoid(dec_f[l].astype(F32)),
                        jax.nn.log_sigmoid(dec_b[l].astype(F32))])
        side = [w_a_out[l], w_r_out[l], w_o[l], w_ff_gate[l], w_ff_up[l], w_ff_down[l]]
        proj, kt, (w_a, w_r, w_ob, w_gate, w_up, w_down) = _inproj(
            x2, g_mix[l][None, :], w_in[l].astype(BF16), rot,
            g_ret[l][None, :].astype(F32), side, tm=tm_in)
        z = _retention(lg, proj, kt, batch, seq, block_tokens=2048)
        x1 = _merge(x2, proj, z, w_conv[l], w_a, w_r, w_ob, seq, tm=512)
        x2 = _ffn(x1, g_ffn[l][None, :], w_gate, w_up, w_down,
                  g_final[None, :], final_norm=(l == depth - 1), tm=512)
    return x2.reshape(batch, seq, d)
```

```python
import functools

import jax
import jax.numpy as jnp
from jax import lax
from jax.experimental import pallas as pl
from jax.experimental.pallas import tpu as pltpu

F32 = jnp.float32
BF16 = jnp.bfloat16

D_MODEL = 1024
RET_HEADS = 4
QK_DIM = D_MODEL // RET_HEADS
V_DIM = 2 * QK_DIM
RET_V = RET_HEADS * V_DIM
ROPE_BASE = 10000.0
EPS = 1e-6
CONV_K = 3

SEC = 1024
W_XC, W_GB, W_GC, W_Q, W_K, W_V, W_GSW, W_GA, W_GR, N_W_SEC = 0, 1, 2, 3, 4, 5, 7, 9, 10, 11
SEC_U, SEC_GB, SEC_GSW, SEC_V, SEC_Q, SEC_GA, SEC_GR, N_OUT_SEC = 0, 1, 2, 4, 6, 7, 8, 9

RET_CHUNK = 256
SIDE_CAST_STEPS = 16
ROW_SUBBLOCKS = 2
LANES = 128
BF16_SUBLANES = 16

VMEM_LIMIT = 56 * 1024 * 1024


def _sigmoid(x):
    return 1.0 / (1.0 + jnp.exp(-x))


def _rms_scale(x):
    return x * lax.rsqrt(jnp.mean(x * x, axis=-1, keepdims=True) + EPS)


def _const_spec(shape):
    return pl.BlockSpec(shape, lambda *_: (0,) * len(shape), pipeline_mode=pl.Buffered(1))


def _inproj_kernel(x_ref, g_ref, w_ref, cos_base_ref, sin_base_ref, cos_off_ref, sin_off_ref,
                   gret_ref, *rest, n_pos_blocks, n_side):
    side_in = rest[:n_side]
    o_ref, kt_ref = rest[n_side:n_side + 2]
    side_out = rest[n_side + 2:2 * n_side + 2]
    w_bf16_ref = rest[2 * n_side + 2]
    step = pl.program_id(0) - N_W_SEC

    @pl.when(step < 0)
    def _():
        w_bf16_ref[pl.program_id(0)] = w_ref[...].astype(BF16)

    @pl.when(step >= 0)
    def _():
        h = (_rms_scale(x_ref[...]) * g_ref[...]).astype(BF16)
        half = QK_DIM // 2
        pos_block = pl.ds(step % n_pos_blocks, 1)
        cos_b = cos_base_ref[pos_block, :]
        sin_b = sin_base_ref[pos_block, :]
        cos = cos_b * cos_off_ref[...] - sin_b * sin_off_ref[...]
        sin = sin_b * cos_off_ref[...] + cos_b * sin_off_ref[...]

        def project(first_sec, n_sec):
            return jnp.concatenate(
                [jnp.dot(h, w_bf16_ref[sec], preferred_element_type=F32)
                 for sec in range(first_sec, first_sec + n_sec)], axis=1)

        def out_cols(sec, n_sec=1):
            return slice(sec * SEC, (sec + n_sec) * SEC)

        qk = project(W_Q, 2)
        for sec in range(2):
            for hh in range(RET_HEADS):
                lo = sec * SEC + hh * QK_DIM
                t1 = qk[:, lo:lo + half]
                t2 = qk[:, lo + half:lo + QK_DIM]
                r1 = t1 * cos - t2 * sin
                r2 = t1 * sin + t2 * cos
                if sec == 0:
                    scale = QK_DIM ** -0.5
                    out = SEC_Q * SEC + hh * QK_DIM
                    o_ref[:, out:out + half] = (r1 * scale).astype(BF16)
                    o_ref[:, out + half:out + QK_DIM] = (r2 * scale).astype(BF16)
                else:
                    out = hh * QK_DIM
                    kt_ref[out:out + half, :] = r1.T.astype(BF16)
                    kt_ref[out + half:out + QK_DIM, :] = r2.T.astype(BF16)

        gsw = project(W_GSW, 2)
        o_ref[:, out_cols(SEC_GSW, 2)] = (gsw * _sigmoid(gsw) * gret_ref[...]).astype(BF16)

        gates = project(W_GA, 2)
        o_ref[:, out_cols(SEC_GA, 2)] = _sigmoid(gates).astype(BF16)

        conv_in = project(W_XC, 3)
        o_ref[:, out_cols(SEC_U)] = (conv_in[:, out_cols(W_GC)] * conv_in[:, out_cols(W_XC)]
                                     ).astype(BF16)
        o_ref[:, out_cols(SEC_GB)] = conv_in[:, out_cols(W_GB)].astype(BF16)

        o_ref[:, out_cols(SEC_V, 2)] = project(W_V, 2).astype(BF16)

    @pl.when(jnp.logical_and(step >= 0, step < SIDE_CAST_STEPS))
    def _():
        for src, dst in zip(side_in, side_out):
            dst[...] = src[...].astype(BF16)


def _inproj(x2, g_mix, w_in, rot, g_ret, side_weights, tm):
    t, d = x2.shape
    cos_base, sin_base, cos_off, sin_off = rot
    assert cos_off.shape[0] == tm and w_in.shape[1] == N_W_SEC * SEC
    n_out = N_OUT_SEC * SEC
    n_steps = t // tm
    assert n_steps >= SIDE_CAST_STEPS

    def tile(i):
        return jnp.maximum(i - N_W_SEC, 0)

    def slab_spec(w):
        rows = w.shape[0] // SIDE_CAST_STEPS
        assert rows * SIDE_CAST_STEPS == w.shape[0] and rows % BF16_SUBLANES == 0
        return pl.BlockSpec((rows, w.shape[1]),
                            lambda i: (jnp.minimum(tile(i), SIDE_CAST_STEPS - 1), 0))

    side_specs = [slab_spec(w) for w in side_weights]
    outs = pl.pallas_call(
        functools.partial(_inproj_kernel, n_pos_blocks=cos_base.shape[0],
                          n_side=len(side_weights)),
        grid=(N_W_SEC + n_steps,),
        in_specs=[
            pl.BlockSpec((tm, d), lambda i: (tile(i), 0)),
            _const_spec(g_mix.shape),
            pl.BlockSpec((d, SEC), lambda i: (0, jnp.minimum(i, N_W_SEC - 1))),
            _const_spec(cos_base.shape),
            _const_spec(sin_base.shape),
            _const_spec(cos_off.shape),
            _const_spec(sin_off.shape),
            _const_spec(g_ret.shape),
        ] + side_specs,
        out_specs=[
            pl.BlockSpec((tm, n_out), lambda i: (tile(i), 0)),
            pl.BlockSpec((RET_HEADS * QK_DIM, tm), lambda i: (0, tile(i))),
        ] + side_specs,
        out_shape=[
            jax.ShapeDtypeStruct((t, n_out), BF16),
            jax.ShapeDtypeStruct((RET_HEADS * QK_DIM, t), BF16),
        ] + [jax.ShapeDtypeStruct(w.shape, BF16) for w in side_weights],
        scratch_shapes=[pltpu.VMEM((N_W_SEC, d, SEC), BF16)],
        compiler_params=pltpu.CompilerParams(
            dimension_semantics=("arbitrary",),
            vmem_limit_bytes=VMEM_LIMIT),
        name="inproj",
    )(x2, g_mix, w_in, cos_base, sin_base, cos_off, sin_off, g_ret, *side_weights)
    return outs[0], outs[1], outs[2:]


def _retention_kernel(lg_ref, q_ref, ktf_ref, vf_ref, ktb_ref, vb_ref, z_ref,
                      decay_ref, xif_ref, xib_ref, zetaf_ref, zetab_ref,
                      fstate_ref, bstate_ref, rall_ref,
                      *, n_pairs, n_blocks, chunks_per_block):
    c = RET_CHUNK
    cpb = chunks_per_block
    n_chunks = n_blocks * cpb
    p = pl.program_id(0)
    j = pl.program_id(1)
    head_f = jnp.maximum(p - 1, 0) % RET_HEADS
    head_b = jnp.minimum(p, n_pairs - 1) % RET_HEADS
    lg_f = lg_ref[0, head_f]
    lg_fb = lg_ref[1, head_f]
    lg_b = lg_ref[1, head_b]
    has_fwd = p >= 1
    has_bwd = p < n_pairs

    @pl.when(jnp.logical_and(j == 0, has_fwd))
    def _():
        row = lax.broadcasted_iota(jnp.int32, (c, c), 0).astype(F32)
        col = lax.broadcasted_iota(jnp.int32, (c, c), 1).astype(F32)
        diff = row - col
        decay_ref[...] = jnp.where(diff >= 0.0,
                                   jnp.exp(jnp.maximum(diff, 0.0) * lg_f),
                                   jnp.exp(jnp.maximum(-diff, 0.0) * lg_fb))
        zetaf_ref[...] = jnp.exp((c - 1.0 - col) * lg_f).astype(BF16)
        idx = lax.broadcasted_iota(jnp.int32, (c, LANES), 0).astype(F32)
        xif_ref[...] = jnp.exp((idx + 1.0) * lg_f).astype(BF16)
        xib_ref[...] = jnp.exp((c - idx) * lg_fb).astype(BF16)
        fstate_ref[...] = jnp.zeros_like(fstate_ref)

    @pl.when(jnp.logical_and(j == 0, has_bwd))
    def _():
        col = lax.broadcasted_iota(jnp.int32, (c, c), 1).astype(F32)
        zetab_ref[...] = jnp.exp(col * lg_b).astype(BF16)
        bstate_ref[...] = jnp.zeros_like(bstate_ref)

    def row_scaled(a, scale_ref):
        scale = scale_ref[...]
        return jnp.concatenate(
            [a[:, n * LANES:(n + 1) * LANES] * scale
             for n in range(a.shape[1] // LANES)], axis=1)

    def decayed_kv(kt_ref, v_ref, zeta_ref, ci):
        rows = slice(ci * c, (ci + 1) * c)
        return jnp.dot(kt_ref[:, rows] * zeta_ref[...], v_ref[rows, :],
                       preferred_element_type=F32)

    def bwd_sweep():
        blk = n_blocks - 1 - j
        slot = (p % 2) * n_chunks
        chunk_decay = jnp.exp(jnp.zeros((1, V_DIM), F32) + c * lg_b)
        state = bstate_ref[...]
        for ci in reversed(range(cpb)):
            rall_ref[slot + blk * cpb + ci] = state.astype(BF16)
            state = chunk_decay * state + decayed_kv(ktb_ref, vb_ref, zetab_ref, ci)
        bstate_ref[...] = state

    def fwd_sweep():
        slot = ((p - 1) % 2) * n_chunks
        chunk_decay = jnp.exp(jnp.zeros((1, V_DIM), F32) + c * lg_f)

        def decayed_scores(ci):
            rows = slice(ci * c, (ci + 1) * c)
            scores = jnp.dot(q_ref[rows, :], ktf_ref[:, rows], preferred_element_type=F32)
            return (scores * decay_ref[...]).astype(BF16)

        state = fstate_ref[...]
        pmat_next = decayed_scores(0)
        kv_next = decayed_kv(ktf_ref, vf_ref, zetaf_ref, 0)
        for ci in range(cpb):
            rows = slice(ci * c, (ci + 1) * c)
            pmat, kv = pmat_next, kv_next
            if ci + 1 < cpb:
                pmat_next = decayed_scores(ci + 1)
                kv_next = decayed_kv(ktf_ref, vf_ref, zetaf_ref, ci + 1)
            q = q_ref[rows, :]
            lhs = jnp.concatenate([pmat, row_scaled(q, xib_ref), row_scaled(q, xif_ref)],
                                  axis=1)
            rhs = jnp.concatenate([vf_ref[rows, :], rall_ref[slot + j * cpb + ci],
                                   state.astype(BF16)], axis=0)
            ret = jnp.dot(lhs, rhs, preferred_element_type=F32)
            mu = jnp.mean(ret, axis=-1, keepdims=True)
            dev = ret - mu
            var = jnp.mean(dev * dev, axis=-1, keepdims=True)
            z_ref[rows, :] = (dev * lax.rsqrt(var + EPS)).astype(BF16)
            state = chunk_decay * state + kv
        fstate_ref[...] = state

    @pl.when(jnp.logical_not(has_fwd))
    def _():
        bwd_sweep()

    @pl.when(jnp.logical_and(has_fwd, has_bwd))
    def _():
        fwd_sweep()
        bwd_sweep()

    @pl.when(jnp.logical_not(has_bwd))
    def _():
        fwd_sweep()


def _retention(lg, proj, kt, batch, seq, block_tokens):
    c = RET_CHUNK
    nb = seq // block_tokens
    cpb = block_tokens // c
    nc = seq // c
    t = batch * seq
    n_pairs = batch * RET_HEADS
    q_col = (SEC_Q * SEC) // QK_DIM
    v_col = (SEC_V * SEC) // V_DIM

    def fwd_idx(p, j):
        pair = jnp.maximum(p - 1, 0)
        return pair // RET_HEADS, pair % RET_HEADS, jnp.where(p == 0, 0, j)

    def bwd_idx(p, j):
        pair = jnp.minimum(p, n_pairs - 1)
        return pair // RET_HEADS, pair % RET_HEADS, jnp.where(p == n_pairs, 0, nb - 1 - j)

    def tok_major(idx_fn, col0):
        def index_map(p, j):
            b, h, blk = idx_fn(p, j)
            return b * nb + blk, col0 + h
        return index_map

    def head_major(idx_fn):
        def index_map(p, j):
            b, h, blk = idx_fn(p, j)
            return h, b * nb + blk
        return index_map

    return pl.pallas_call(
        functools.partial(_retention_kernel, n_pairs=n_pairs, n_blocks=nb,
                          chunks_per_block=cpb),
        grid=(n_pairs + 1, nb),
        in_specs=[
            pl.BlockSpec(memory_space=pltpu.SMEM),
            pl.BlockSpec((block_tokens, QK_DIM), tok_major(fwd_idx, q_col)),
            pl.BlockSpec((QK_DIM, block_tokens), head_major(fwd_idx)),
            pl.BlockSpec((block_tokens, V_DIM), tok_major(fwd_idx, v_col)),
            pl.BlockSpec((QK_DIM, block_tokens), head_major(bwd_idx)),
            pl.BlockSpec((block_tokens, V_DIM), tok_major(bwd_idx, v_col)),
        ],
        out_specs=pl.BlockSpec((block_tokens, V_DIM), tok_major(fwd_idx, 0)),
        out_shape=jax.ShapeDtypeStruct((t, RET_V), BF16),
        scratch_shapes=[
            pltpu.VMEM((c, c), F32),
            pltpu.VMEM((c, LANES), BF16),
            pltpu.VMEM((c, LANES), BF16),
            pltpu.VMEM((QK_DIM, c), BF16),
            pltpu.VMEM((QK_DIM, c), BF16),
            pltpu.VMEM((QK_DIM, V_DIM), F32),
            pltpu.VMEM((QK_DIM, V_DIM), F32),
            pltpu.VMEM((2 * nc, QK_DIM, V_DIM), BF16),
        ],
        compiler_params=pltpu.CompilerParams(
            dimension_semantics=("arbitrary", "arbitrary"),
            vmem_limit_bytes=VMEM_LIMIT),
        name="retention",
    )(lg, proj, kt, proj, kt, proj)


def _merge_kernel(x_ref, u_ref, gb_ref, up_ref, un_ref, ga_ref, gr_ref, z_ref, gsw_ref,
                  wconv_ref, wa_ref, wr_ref, wo_ref, o_ref, *, tiles_per_seq):
    i = pl.program_id(0)
    tm = x_ref.shape[0]
    u = u_ref[...].astype(F32)
    pos = i % tiles_per_seq
    last = BF16_SUBLANES - 1
    u_before = up_ref[last:last + 1, :].astype(F32) * (pos != 0).astype(F32)
    u_after = un_ref[0:1, :].astype(F32) * (pos != tiles_per_seq - 1).astype(F32)
    row = lax.broadcasted_iota(jnp.int32, u.shape, 0)
    u_prev = jnp.where(row == 0, u_before, pltpu.roll(u, 1, axis=0))
    u_next = jnp.where(row == tm - 1, u_after, pltpu.roll(u, tm - 1, axis=0))
    conv = u_prev * wconv_ref[0:1, :] + u * wconv_ref[1:2, :] + u_next * wconv_ref[2:3, :]
    a_in = (gb_ref[...].astype(F32) * conv).astype(BF16)

    def branch_outputs(rows):
        y_r = jnp.dot(z_ref[rows, :] * gsw_ref[rows, :], wr_ref[...],
                      preferred_element_type=F32)
        y_a = jnp.dot(a_in[rows, :], wa_ref[...], preferred_element_type=F32)
        return y_a, y_r

    sub = tm // ROW_SUBBLOCKS
    blocks = [slice(s * sub, (s + 1) * sub) for s in range(ROW_SUBBLOCKS)]
    y_next = branch_outputs(blocks[0])
    for s, rows in enumerate(blocks):
        y_a, y_r = y_next
        if s + 1 < len(blocks):
            y_next = branch_outputs(blocks[s + 1])
        merged = ga_ref[rows, :].astype(F32) * y_a + gr_ref[rows, :].astype(F32) * y_r
        o_ref[rows, :] = x_ref[rows, :] + jnp.dot(merged.astype(BF16), wo_ref[...],
                                                  preferred_element_type=F32)


def _merge(x2, proj, z, w_conv, w_a, w_r, w_o, seq, tm):
    t, d = x2.shape
    hb = tm // BF16_SUBLANES
    n_halo_blocks = t // BF16_SUBLANES
    halo = (BF16_SUBLANES, SEC)
    return pl.pallas_call(
        functools.partial(_merge_kernel, tiles_per_seq=seq // tm),
        grid=(t // tm,),
        in_specs=[
            pl.BlockSpec((tm, d), lambda i: (i, 0)),
            pl.BlockSpec((tm, SEC), lambda i: (i, SEC_U)),
            pl.BlockSpec((tm, SEC), lambda i: (i, SEC_GB)),
            pl.BlockSpec(halo, lambda i: (jnp.maximum(i * hb - 1, 0), SEC_U)),
            pl.BlockSpec(halo, lambda i: (jnp.minimum((i + 1) * hb, n_halo_blocks - 1), SEC_U)),
            pl.BlockSpec((tm, SEC), lambda i: (i, SEC_GA)),
            pl.BlockSpec((tm, SEC), lambda i: (i, SEC_GR)),
            pl.BlockSpec((tm, RET_V), lambda i: (i, 0)),
            pl.BlockSpec((tm, RET_V), lambda i: (i, (SEC_GSW * SEC) // RET_V)),
            _const_spec(w_conv.shape),
            _const_spec(w_a.shape),
            _const_spec(w_r.shape),
            _const_spec(w_o.shape),
        ],
        out_specs=pl.BlockSpec((tm, d), lambda i: (i, 0)),
        out_shape=jax.ShapeDtypeStruct((t, d), F32),
        compiler_params=pltpu.CompilerParams(
            dimension_semantics=("arbitrary",),
            vmem_limit_bytes=VMEM_LIMIT),
        name="merge",
    )(x2, proj, proj, proj, proj, proj, proj, z, proj, w_conv, w_a, w_r, w_o)


def _ffn_kernel(x_ref, gffn_ref, wg_ref, wu_ref, wd_ref, gfin_ref, o_ref, *, final_norm):
    def gate_up(rows):
        h2 = (_rms_scale(x_ref[rows, :]) * gffn_ref[...]).astype(BF16)
        return (jnp.dot(h2, wg_ref[...], preferred_element_type=F32),
                jnp.dot(h2, wu_ref[...], preferred_element_type=F32))

    sub = x_ref.shape[0] // ROW_SUBBLOCKS
    blocks = [slice(s * sub, (s + 1) * sub) for s in range(ROW_SUBBLOCKS)]
    gu_next = gate_up(blocks[0])
    for s, rows in enumerate(blocks):
        gate, up = gu_next
        if s + 1 < len(blocks):
            gu_next = gate_up(blocks[s + 1])
        act = (gate * _sigmoid(gate) * up).astype(BF16)
        y = x_ref[rows, :] + jnp.dot(act, wd_ref[...], preferred_element_type=F32)
        o_ref[rows, :] = _rms_scale(y) * gfin_ref[...] if final_norm else y


def _ffn(x1, g_ffn, w_gate, w_up, w_down, g_final, final_norm, tm):
    t, d = x1.shape
    return pl.pallas_call(
        functools.partial(_ffn_kernel, final_norm=final_norm),
        grid=(t // tm,),
        in_specs=[
            pl.BlockSpec((tm, d), lambda i: (i, 0)),
            _const_spec(g_ffn.shape),
            _const_spec(w_gate.shape),
            _const_spec(w_up.shape),
            _const_spec(w_down.shape),
            _const_spec(g_final.shape),
        ],
        out_specs=pl.BlockSpec((tm, d), lambda i: (i, 0)),
        out_shape=jax.ShapeDtypeStruct((t, d), F32),
        compiler_params=pltpu.CompilerParams(
            dimension_semantics=("arbitrary",),
            vmem_limit_bytes=VMEM_LIMIT),
        name="ffn",
    )(x1, g_ffn, w_gate, w_up, w_down, g_final)


def _rotary_tables(seq, tm):
    freqs = ROPE_BASE ** (-jnp.arange(0, QK_DIM, 2, dtype=F32) / QK_DIM)
    base = (jnp.arange(seq // tm, dtype=F32) * tm)[:, None] * freqs[None, :]
    off = jnp.arange(tm, dtype=F32)[:, None] * freqs[None, :]
    return jnp.cos(base), jnp.sin(base), jnp.cos(off), jnp.sin(off)


def kernel(x, g_mix, w_in, w_conv, dec_f, dec_b, g_ret, w_a_out, w_r_out, w_o,
           g_ffn, w_ff_gate, w_ff_up, w_ff_down, g_final):
    batch, seq, d = x.shape
    depth = w_in.shape[0]
    assert d == D_MODEL and seq % RET_CHUNK == 0
    x2 = x.reshape(batch * seq, d)
    tm_in = 256
    rot = _rotary_tables(seq, tm_in)
    for l in range(depth):
        lg = jnp.stack([jax.nn.log_sigmoid(dec_f[l].astype(F32)),
                        jax.nn.log_sigmoid(dec_b[l].astype(F32))])
        side = [w_a_out[l], w_r_out[l], w_o[l], w_ff_gate[l], w_ff_up[l], w_ff_down[l]]
        proj, kt, (w_a, w_r, w_ob, w_gate, w_up, w_down) = _inproj(
            x2, g_mix[l][None, :], w_in[l], rot,
            g_ret[l][None, :].astype(F32), side, tm=tm_in)
        z = _retention(lg, proj, kt, batch, seq, block_tokens=2048)
        x1 = _merge(x2, proj, z, w_conv[l], w_a, w_r, w_ob, seq, tm=512)
        x2 = _ffn(x1, g_ffn[l][None, :], w_gate, w_up, w_down,
                  g_final[None, :], final_norm=(l == depth - 1), tm=512)
    return x2.reshape(batch, seq, d)
```

```python
import functools

import jax
import jax.numpy as jnp
from jax import lax
from jax.experimental import pallas as pl
from jax.experimental.pallas import tpu as pltpu

F32 = jnp.float32
BF16 = jnp.bfloat16

D_MODEL = 1024
RET_HEADS = 4
QK_DIM = D_MODEL // RET_HEADS
V_DIM = 2 * QK_DIM
RET_V = RET_HEADS * V_DIM
ROPE_BASE = 10000.0
EPS = 1e-6
CONV_K = 3

SEC = 1024
W_XC, W_GB, W_GC, W_Q, W_K, W_V, W_GSW, W_GA, W_GR, N_W_SEC = 0, 1, 2, 3, 4, 5, 7, 9, 10, 11
SEC_U, SEC_GB, SEC_GSW, SEC_V, SEC_Q, SEC_GA, SEC_GR, N_OUT_SEC = 0, 1, 2, 4, 6, 7, 8, 9

RET_CHUNK = 256
SIDE_CAST_STEPS = 16
SUBBLOCK_ROWS = 256
LANES = 128
BF16_SUBLANES = 16

VMEM_LIMIT = 56 * 1024 * 1024


def _sigmoid(x):
    return 1.0 / (1.0 + jnp.exp(-x))


def _rms_scale(x):
    return x * lax.rsqrt(jnp.mean(x * x, axis=-1, keepdims=True) + EPS)


def _const_spec(shape):
    return pl.BlockSpec(shape, lambda *_: (0,) * len(shape), pipeline_mode=pl.Buffered(1))


def _inproj_kernel(x_ref, g_ref, w_ref, cos_base_ref, sin_base_ref, cos_off_ref, sin_off_ref,
                   gret_ref, *rest, n_pos_blocks, n_side):
    side_in = rest[:n_side]
    o_ref, kt_ref = rest[n_side:n_side + 2]
    side_out = rest[n_side + 2:2 * n_side + 2]
    w_bf16_ref = rest[2 * n_side + 2]
    step = pl.program_id(0) - N_W_SEC

    @pl.when(step < 0)
    def _():
        w_bf16_ref[pl.program_id(0)] = w_ref[...].astype(BF16)

    @pl.when(step >= 0)
    def _():
        h = (_rms_scale(x_ref[...]) * g_ref[...]).astype(BF16)
        half = QK_DIM // 2
        pos_block = pl.ds(step % n_pos_blocks, 1)
        cos_b = cos_base_ref[pos_block, :]
        sin_b = sin_base_ref[pos_block, :]
        cos = cos_b * cos_off_ref[...] - sin_b * sin_off_ref[...]
        sin = sin_b * cos_off_ref[...] + cos_b * sin_off_ref[...]

        def project(first_sec, n_sec):
            return jnp.concatenate(
                [jnp.dot(h, w_bf16_ref[sec], preferred_element_type=F32)
                 for sec in range(first_sec, first_sec + n_sec)], axis=1)

        def out_cols(sec, n_sec=1):
            return slice(sec * SEC, (sec + n_sec) * SEC)

        qk = project(W_Q, 2)
        for sec in range(2):
            for hh in range(RET_HEADS):
                lo = sec * SEC + hh * QK_DIM
                t1 = qk[:, lo:lo + half]
                t2 = qk[:, lo + half:lo + QK_DIM]
                r1 = t1 * cos - t2 * sin
                r2 = t1 * sin + t2 * cos
                if sec == 0:
                    scale = QK_DIM ** -0.5
                    out = SEC_Q * SEC + hh * QK_DIM
                    o_ref[:, out:out + half] = (r1 * scale).astype(BF16)
                    o_ref[:, out + half:out + QK_DIM] = (r2 * scale).astype(BF16)
                else:
                    out = hh * QK_DIM
                    kt_ref[out:out + half, :] = r1.T.astype(BF16)
                    kt_ref[out + half:out + QK_DIM, :] = r2.T.astype(BF16)

        gsw = project(W_GSW, 2)
        o_ref[:, out_cols(SEC_GSW, 2)] = (gsw * _sigmoid(gsw) * gret_ref[...]).astype(BF16)

        gates = project(W_GA, 2)
        o_ref[:, out_cols(SEC_GA, 2)] = _sigmoid(gates).astype(BF16)

        conv_in = project(W_XC, 3)
        o_ref[:, out_cols(SEC_U)] = (conv_in[:, out_cols(W_GC)] * conv_in[:, out_cols(W_XC)]
                                     ).astype(BF16)
        o_ref[:, out_cols(SEC_GB)] = conv_in[:, out_cols(W_GB)].astype(BF16)

        o_ref[:, out_cols(SEC_V, 2)] = project(W_V, 2).astype(BF16)

    @pl.when(jnp.logical_and(step >= 0, step < SIDE_CAST_STEPS))
    def _():
        for src, dst in zip(side_in, side_out):
            dst[...] = src[...].astype(BF16)


def _inproj(x2, g_mix, w_in, rot, g_ret, side_weights, tm):
    t, d = x2.shape
    cos_base, sin_base, cos_off, sin_off = rot
    assert cos_off.shape[0] == tm and w_in.shape[1] == N_W_SEC * SEC
    n_out = N_OUT_SEC * SEC
    n_steps = t // tm
    assert n_steps >= SIDE_CAST_STEPS

    def tile(i):
        return jnp.maximum(i - N_W_SEC, 0)

    def slab_spec(w):
        rows = w.shape[0] // SIDE_CAST_STEPS
        assert rows * SIDE_CAST_STEPS == w.shape[0] and rows % BF16_SUBLANES == 0
        return pl.BlockSpec((rows, w.shape[1]),
                            lambda i: (jnp.minimum(tile(i), SIDE_CAST_STEPS - 1), 0))

    side_specs = [slab_spec(w) for w in side_weights]
    outs = pl.pallas_call(
        functools.partial(_inproj_kernel, n_pos_blocks=cos_base.shape[0],
                          n_side=len(side_weights)),
        grid=(N_W_SEC + n_steps,),
        in_specs=[
            pl.BlockSpec((tm, d), lambda i: (tile(i), 0)),
            _const_spec(g_mix.shape),
            pl.BlockSpec((d, SEC), lambda i: (0, jnp.minimum(i, N_W_SEC - 1))),
            _const_spec(cos_base.shape),
            _const_spec(sin_base.shape),
            _const_spec(cos_off.shape),
            _const_spec(sin_off.shape),
            _const_spec(g_ret.shape),
        ] + side_specs,
        out_specs=[
            pl.BlockSpec((tm, n_out), lambda i: (tile(i), 0)),
            pl.BlockSpec((RET_HEADS * QK_DIM, tm), lambda i: (0, tile(i))),
        ] + side_specs,
        out_shape=[
            jax.ShapeDtypeStruct((t, n_out), BF16),
            jax.ShapeDtypeStruct((RET_HEADS * QK_DIM, t), BF16),
        ] + [jax.ShapeDtypeStruct(w.shape, BF16) for w in side_weights],
        scratch_shapes=[pltpu.VMEM((N_W_SEC, d, SEC), BF16)],
        compiler_params=pltpu.CompilerParams(
            dimension_semantics=("arbitrary",),
            vmem_limit_bytes=VMEM_LIMIT),
        name="inproj",
    )(x2, g_mix, w_in, cos_base, sin_base, cos_off, sin_off, g_ret, *side_weights)
    return outs[0], outs[1], outs[2:]


def _retention_kernel(lg_ref, q_ref, ktf_ref, ktb_ref, vb_ref, z_ref,
                      decay_ref, xif_ref, xib_ref, zetaf_ref, zetab_ref,
                      fstate_ref, bstate_ref, rall_ref, vres_ref,
                      *, n_pairs, n_blocks, chunks_per_block):
    c = RET_CHUNK
    cpb = chunks_per_block
    n_chunks = n_blocks * cpb
    p = pl.program_id(0)
    j = pl.program_id(1)
    head_f = jnp.maximum(p - 1, 0) % RET_HEADS
    head_b = jnp.minimum(p, n_pairs - 1) % RET_HEADS
    lg_f = lg_ref[0, head_f]
    lg_fb = lg_ref[1, head_f]
    lg_b = lg_ref[1, head_b]
    has_fwd = p >= 1
    has_bwd = p < n_pairs

    @pl.when(jnp.logical_and(j == 0, has_fwd))
    def _():
        row = lax.broadcasted_iota(jnp.int32, (c, c), 0).astype(F32)
        col = lax.broadcasted_iota(jnp.int32, (c, c), 1).astype(F32)
        diff = row - col
        decay_ref[...] = jnp.where(diff >= 0.0,
                                   jnp.exp(jnp.maximum(diff, 0.0) * lg_f),
                                   jnp.exp(jnp.maximum(-diff, 0.0) * lg_fb))
        zetaf_ref[...] = jnp.exp((c - 1.0 - col) * lg_f).astype(BF16)
        idx = lax.broadcasted_iota(jnp.int32, (c, LANES), 0).astype(F32)
        xif_ref[...] = jnp.exp((idx + 1.0) * lg_f).astype(BF16)
        xib_ref[...] = jnp.exp((c - idx) * lg_fb).astype(BF16)
        fstate_ref[...] = jnp.zeros_like(fstate_ref)

    @pl.when(jnp.logical_and(j == 0, has_bwd))
    def _():
        col = lax.broadcasted_iota(jnp.int32, (c, c), 1).astype(F32)
        zetab_ref[...] = jnp.exp(col * lg_b).astype(BF16)
        bstate_ref[...] = jnp.zeros_like(bstate_ref)

    def row_scaled(a, scale_ref):
        scale = scale_ref[...]
        return jnp.concatenate(
            [a[:, n * LANES:(n + 1) * LANES] * scale
             for n in range(a.shape[1] // LANES)], axis=1)

    def decayed_kv(kt_ref, v, zeta_ref, ci):
        return jnp.dot(kt_ref[:, ci * c:(ci + 1) * c] * zeta_ref[...], v,
                       preferred_element_type=F32)

    def bwd_sweep():
        blk = n_blocks - 1 - j
        slot = (p % 2) * n_chunks
        chunk_decay = jnp.exp(jnp.zeros((1, V_DIM), F32) + c * lg_b)
        vres_ref[p % 2, pl.ds(pl.multiple_of(blk * cpb * c, c), cpb * c), :] = vb_ref[...]
        state = bstate_ref[...]
        for ci in reversed(range(cpb)):
            rall_ref[slot + blk * cpb + ci] = state.astype(BF16)
            state = chunk_decay * state + decayed_kv(
                ktb_ref, vb_ref[ci * c:(ci + 1) * c, :], zetab_ref, ci)
        bstate_ref[...] = state

    def fwd_sweep():
        slot = ((p - 1) % 2) * n_chunks
        chunk_decay = jnp.exp(jnp.zeros((1, V_DIM), F32) + c * lg_f)

        def decayed_scores(ci):
            rows = slice(ci * c, (ci + 1) * c)
            scores = jnp.dot(q_ref[rows, :], ktf_ref[:, rows], preferred_element_type=F32)
            return (scores * decay_ref[...]).astype(BF16)

        def values(ci):
            start = pl.multiple_of((j * cpb + ci) * c, c)
            return vres_ref[(p - 1) % 2, pl.ds(start, c), :]

        state = fstate_ref[...]
        pmat_next = decayed_scores(0)
        kv_next = decayed_kv(ktf_ref, values(0), zetaf_ref, 0)
        for ci in range(cpb):
            rows = slice(ci * c, (ci + 1) * c)
            pmat, kv = pmat_next, kv_next
            if ci + 1 < cpb:
                pmat_next = decayed_scores(ci + 1)
                kv_next = decayed_kv(ktf_ref, values(ci + 1), zetaf_ref, ci + 1)
            q = q_ref[rows, :]
            lhs = jnp.concatenate([pmat, row_scaled(q, xib_ref), row_scaled(q, xif_ref)],
                                  axis=1)
            rhs = jnp.concatenate([values(ci), rall_ref[slot + j * cpb + ci],
                                   state.astype(BF16)], axis=0)
            ret = jnp.dot(lhs, rhs, preferred_element_type=F32)
            mu = jnp.mean(ret, axis=-1, keepdims=True)
            dev = ret - mu
            var = jnp.mean(dev * dev, axis=-1, keepdims=True)
            z_ref[rows, :] = (dev * lax.rsqrt(var + EPS)).astype(BF16)
            state = chunk_decay * state + kv
        fstate_ref[...] = state

    @pl.when(jnp.logical_not(has_fwd))
    def _():
        bwd_sweep()

    @pl.when(jnp.logical_and(has_fwd, has_bwd))
    def _():
        fwd_sweep()
        bwd_sweep()

    @pl.when(jnp.logical_not(has_bwd))
    def _():
        fwd_sweep()


def _retention(lg, proj, kt, batch, seq, block_tokens):
    c = RET_CHUNK
    nb = seq // block_tokens
    cpb = block_tokens // c
    nc = seq // c
    t = batch * seq
    n_pairs = batch * RET_HEADS
    q_col = (SEC_Q * SEC) // QK_DIM
    v_col = (SEC_V * SEC) // V_DIM

    def fwd_idx(p, j):
        pair = jnp.maximum(p - 1, 0)
        return pair // RET_HEADS, pair % RET_HEADS, jnp.where(p == 0, 0, j)

    def bwd_idx(p, j):
        pair = jnp.minimum(p, n_pairs - 1)
        return pair // RET_HEADS, pair % RET_HEADS, jnp.where(p == n_pairs, 0, nb - 1 - j)

    def tok_major(idx_fn, col0):
        def index_map(p, j):
            b, h, blk = idx_fn(p, j)
            return b * nb + blk, col0 + h
        return index_map

    def head_major(idx_fn):
        def index_map(p, j):
            b, h, blk = idx_fn(p, j)
            return h, b * nb + blk
        return index_map

    return pl.pallas_call(
        functools.partial(_retention_kernel, n_pairs=n_pairs, n_blocks=nb,
                          chunks_per_block=cpb),
        grid=(n_pairs + 1, nb),
        in_specs=[
            pl.BlockSpec(memory_space=pltpu.SMEM),
            pl.BlockSpec((block_tokens, QK_DIM), tok_major(fwd_idx, q_col)),
            pl.BlockSpec((QK_DIM, block_tokens), head_major(fwd_idx)),
            pl.BlockSpec((QK_DIM, block_tokens), head_major(bwd_idx)),
            pl.BlockSpec((block_tokens, V_DIM), tok_major(bwd_idx, v_col)),
        ],
        out_specs=pl.BlockSpec((block_tokens, V_DIM), tok_major(fwd_idx, 0)),
        out_shape=jax.ShapeDtypeStruct((t, RET_V), BF16),
        scratch_shapes=[
            pltpu.VMEM((c, c), F32),
            pltpu.VMEM((c, LANES), BF16),
            pltpu.VMEM((c, LANES), BF16),
            pltpu.VMEM((QK_DIM, c), BF16),
            pltpu.VMEM((QK_DIM, c), BF16),
            pltpu.VMEM((QK_DIM, V_DIM), F32),
            pltpu.VMEM((QK_DIM, V_DIM), F32),
            pltpu.VMEM((2 * nc, QK_DIM, V_DIM), BF16),
            pltpu.VMEM((2, seq, V_DIM), BF16),
        ],
        compiler_params=pltpu.CompilerParams(
            dimension_semantics=("arbitrary", "arbitrary"),
            vmem_limit_bytes=VMEM_LIMIT),
        name="retention",
    )(lg, proj, kt, kt, proj)


def _merge_kernel(x_ref, u_ref, gb_ref, up_ref, un_ref, ga_ref, gr_ref, z_ref, gsw_ref,
                  wconv_ref, wa_ref, wr_ref, wo_ref, o_ref, *, tiles_per_seq):
    i = pl.program_id(0)
    tm = x_ref.shape[0]
    u = u_ref[...].astype(F32)
    pos = i % tiles_per_seq
    last = BF16_SUBLANES - 1
    u_before = up_ref[last:last + 1, :].astype(F32) * (pos != 0).astype(F32)
    u_after = un_ref[0:1, :].astype(F32) * (pos != tiles_per_seq - 1).astype(F32)
    row = lax.broadcasted_iota(jnp.int32, u.shape, 0)
    u_prev = jnp.where(row == 0, u_before, pltpu.roll(u, 1, axis=0))
    u_next = jnp.where(row == tm - 1, u_after, pltpu.roll(u, tm - 1, axis=0))
    conv = u_prev * wconv_ref[0:1, :] + u * wconv_ref[1:2, :] + u_next * wconv_ref[2:3, :]
    a_in = (gb_ref[...].astype(F32) * conv).astype(BF16)

    def branch_outputs(rows):
        y_r = jnp.dot(z_ref[rows, :] * gsw_ref[rows, :], wr_ref[...],
                      preferred_element_type=F32)
        y_a = jnp.dot(a_in[rows, :], wa_ref[...], preferred_element_type=F32)
        return y_a, y_r

    blocks = [slice(r, r + SUBBLOCK_ROWS) for r in range(0, tm, SUBBLOCK_ROWS)]
    y_next = branch_outputs(blocks[0])
    for s, rows in enumerate(blocks):
        y_a, y_r = y_next
        if s + 1 < len(blocks):
            y_next = branch_outputs(blocks[s + 1])
        merged = ga_ref[rows, :].astype(F32) * y_a + gr_ref[rows, :].astype(F32) * y_r
        o_ref[rows, :] = x_ref[rows, :] + jnp.dot(merged.astype(BF16), wo_ref[...],
                                                  preferred_element_type=F32)


def _merge(x2, proj, z, w_conv, w_a, w_r, w_o, seq, tm):
    t, d = x2.shape
    hb = tm // BF16_SUBLANES
    n_halo_blocks = t // BF16_SUBLANES
    halo = (BF16_SUBLANES, SEC)
    return pl.pallas_call(
        functools.partial(_merge_kernel, tiles_per_seq=seq // tm),
        grid=(t // tm,),
        in_specs=[
            pl.BlockSpec((tm, d), lambda i: (i, 0)),
            pl.BlockSpec((tm, SEC), lambda i: (i, SEC_U)),
            pl.BlockSpec((tm, SEC), lambda i: (i, SEC_GB)),
            pl.BlockSpec(halo, lambda i: (jnp.maximum(i * hb - 1, 0), SEC_U)),
            pl.BlockSpec(halo, lambda i: (jnp.minimum((i + 1) * hb, n_halo_blocks - 1), SEC_U)),
            pl.BlockSpec((tm, SEC), lambda i: (i, SEC_GA)),
            pl.BlockSpec((tm, SEC), lambda i: (i, SEC_GR)),
            pl.BlockSpec((tm, RET_V), lambda i: (i, 0)),
            pl.BlockSpec((tm, RET_V), lambda i: (i, (SEC_GSW * SEC) // RET_V)),
            _const_spec(w_conv.shape),
            _const_spec(w_a.shape),
            _const_spec(w_r.shape),
            _const_spec(w_o.shape),
        ],
        out_specs=pl.BlockSpec((tm, d), lambda i: (i, 0)),
        out_shape=jax.ShapeDtypeStruct((t, d), F32),
        compiler_params=pltpu.CompilerParams(
            dimension_semantics=("arbitrary",),
            vmem_limit_bytes=VMEM_LIMIT),
        name="merge",
    )(x2, proj, proj, proj, proj, proj, proj, z, proj, w_conv, w_a, w_r, w_o)


def _ffn_kernel(x_ref, gffn_ref, wg_ref, wu_ref, wd_ref, gfin_ref, o_ref, *, final_norm):
    def gate_up(rows):
        h2 = (_rms_scale(x_ref[rows, :]) * gffn_ref[...]).astype(BF16)
        return (jnp.dot(h2, wg_ref[...], preferred_element_type=F32),
                jnp.dot(h2, wu_ref[...], preferred_element_type=F32))

    blocks = [slice(r, r + SUBBLOCK_ROWS) for r in range(0, x_ref.shape[0], SUBBLOCK_ROWS)]
    gu_next = gate_up(blocks[0])
    for s, rows in enumerate(blocks):
        gate, up = gu_next
        if s + 1 < len(blocks):
            gu_next = gate_up(blocks[s + 1])
        act = (gate * _sigmoid(gate) * up).astype(BF16)
        y = x_ref[rows, :] + jnp.dot(act, wd_ref[...], preferred_element_type=F32)
        o_ref[rows, :] = _rms_scale(y) * gfin_ref[...] if final_norm else y


def _ffn(x1, g_ffn, w_gate, w_up, w_down, g_final, final_norm, tm):
    t, d = x1.shape
    return pl.pallas_call(
        functools.partial(_ffn_kernel, final_norm=final_norm),
        grid=(t // tm,),
        in_specs=[
            pl.BlockSpec((tm, d), lambda i: (i, 0)),
            _const_spec(g_ffn.shape),
            _const_spec(w_gate.shape),
            _const_spec(w_up.shape),
            _const_spec(w_down.shape),
            _const_spec(g_final.shape),
        ],
        out_specs=pl.BlockSpec((tm, d), lambda i: (i, 0)),
        out_shape=jax.ShapeDtypeStruct((t, d), F32),
        compiler_params=pltpu.CompilerParams(
            dimension_semantics=("arbitrary",),
            vmem_limit_bytes=VMEM_LIMIT),
        name="ffn",
    )(x1, g_ffn, w_gate, w_up, w_down, g_final)


def _rotary_tables(seq, tm):
    freqs = ROPE_BASE ** (-jnp.arange(0, QK_DIM, 2, dtype=F32) / QK_DIM)
    base = (jnp.arange(seq // tm, dtype=F32) * tm)[:, None] * freqs[None, :]
    off = jnp.arange(tm, dtype=F32)[:, None] * freqs[None, :]
    return jnp.cos(base), jnp.sin(base), jnp.cos(off), jnp.sin(off)


def kernel(x, g_mix, w_in, w_conv, dec_f, dec_b, g_ret, w_a_out, w_r_out, w_o,
           g_ffn, w_ff_gate, w_ff_up, w_ff_down, g_final):
    batch, seq, d = x.shape
    depth = w_in.shape[0]
    assert d == D_MODEL and seq % RET_CHUNK == 0
    x2 = x.reshape(batch * seq, d)
    tm_in = 256
    rot = _rotary_tables(seq, tm_in)
    for l in range(depth):
        lg = jnp.stack([jax.nn.log_sigmoid(dec_f[l].astype(F32)),
                        jax.nn.log_sigmoid(dec_b[l].astype(F32))])
        side = [w_a_out[l], w_r_out[l], w_o[l], w_ff_gate[l], w_ff_up[l], w_ff_down[l]]
        proj, kt, (w_a, w_r, w_ob, w_gate, w_up, w_down) = _inproj(
            x2, g_mix[l][None, :], w_in[l], rot,
            g_ret[l][None, :].astype(F32), side, tm=tm_in)
        z = _retention(lg, proj, kt, batch, seq, block_tokens=2048)
        x1 = _merge(x2, proj, z, w_conv[l], w_a, w_r, w_ob, seq, tm=512)
        x2 = _ffn(x1, g_ffn[l][None, :], w_gate, w_up, w_down,
                  g_final[None, :], final_norm=(l == depth - 1), tm=1024)
    return x2.reshape(batch, seq, d)
```

```python
import functools

import jax
import jax.numpy as jnp
from jax import lax
from jax.experimental import pallas as pl
from jax.experimental.pallas import tpu as pltpu

F32 = jnp.float32
BF16 = jnp.bfloat16

D_MODEL = 1024
RET_HEADS = 4
QK_DIM = D_MODEL // RET_HEADS
V_DIM = 2 * QK_DIM
RET_V = RET_HEADS * V_DIM
ROPE_BASE = 10000.0
EPS = 1e-6
CONV_K = 3

SEC = 1024
W_XC, W_GB, W_GC, W_Q, W_K, W_V, W_GSW, W_GA, W_GR, N_W_SEC = 0, 1, 2, 3, 4, 5, 7, 9, 10, 11
SEC_U, SEC_GB, SEC_GSW, SEC_V, SEC_Q, SEC_GA, SEC_GR, N_OUT_SEC = 0, 1, 2, 4, 6, 7, 8, 9

RET_CHUNK = 256
SIDE_CAST_STEPS = 16
SUBBLOCK_ROWS = 256
LANES = 128
BF16_SUBLANES = 16

VMEM_LIMIT = 56 * 1024 * 1024


def _sigmoid(x):
    return 1.0 / (1.0 + jnp.exp(-x))


def _rms_scale(x):
    return x * lax.rsqrt(jnp.mean(x * x, axis=-1, keepdims=True) + EPS)


def _const_spec(shape):
    return pl.BlockSpec(shape, lambda *_: (0,) * len(shape), pipeline_mode=pl.Buffered(1))


def _inproj_kernel(x_ref, g_ref, w_ref, cos_base_ref, sin_base_ref, cos_off_ref, sin_off_ref,
                   gret_ref, *rest, n_pos_blocks, n_side):
    side_in = rest[:n_side]
    o_ref, kt_ref = rest[n_side:n_side + 2]
    side_out = rest[n_side + 2:2 * n_side + 2]
    w_bf16_ref = rest[2 * n_side + 2]
    step = pl.program_id(0) - N_W_SEC

    @pl.when(step < 0)
    def _():
        w_bf16_ref[pl.program_id(0)] = w_ref[...].astype(BF16)

    @pl.when(step >= 0)
    def _():
        h = (_rms_scale(x_ref[...]) * g_ref[...]).astype(BF16)
        half = QK_DIM // 2
        pos_block = pl.ds(step % n_pos_blocks, 1)
        cos_b = cos_base_ref[pos_block, :]
        sin_b = sin_base_ref[pos_block, :]
        cos = cos_b * cos_off_ref[...] - sin_b * sin_off_ref[...]
        sin = sin_b * cos_off_ref[...] + cos_b * sin_off_ref[...]

        def project(first_sec, n_sec):
            return jnp.concatenate(
                [jnp.dot(h, w_bf16_ref[sec], preferred_element_type=F32)
                 for sec in range(first_sec, first_sec + n_sec)], axis=1)

        def out_cols(sec, n_sec=1):
            return slice(sec * SEC, (sec + n_sec) * SEC)

        qk = project(W_Q, 2)
        for sec in range(2):
            for hh in range(RET_HEADS):
                lo = sec * SEC + hh * QK_DIM
                t1 = qk[:, lo:lo + half]
                t2 = qk[:, lo + half:lo + QK_DIM]
                r1 = t1 * cos - t2 * sin
                r2 = t1 * sin + t2 * cos
                if sec == 0:
                    scale = QK_DIM ** -0.5
                    out = SEC_Q * SEC + hh * QK_DIM
                    o_ref[:, out:out + half] = (r1 * scale).astype(BF16)
                    o_ref[:, out + half:out + QK_DIM] = (r2 * scale).astype(BF16)
                else:
                    out = hh * QK_DIM
                    kt_ref[out:out + half, :] = r1.T.astype(BF16)
                    kt_ref[out + half:out + QK_DIM, :] = r2.T.astype(BF16)

        gsw = project(W_GSW, 2)
        o_ref[:, out_cols(SEC_GSW, 2)] = (gsw * _sigmoid(gsw) * gret_ref[...]).astype(BF16)

        gates = project(W_GA, 2)
        o_ref[:, out_cols(SEC_GA, 2)] = _sigmoid(gates).astype(BF16)

        conv_in = project(W_XC, 3)
        o_ref[:, out_cols(SEC_U)] = (conv_in[:, out_cols(W_GC)] * conv_in[:, out_cols(W_XC)]
                                     ).astype(BF16)
        o_ref[:, out_cols(SEC_GB)] = conv_in[:, out_cols(W_GB)].astype(BF16)

        o_ref[:, out_cols(SEC_V, 2)] = project(W_V, 2).astype(BF16)

    @pl.when(jnp.logical_and(step >= 0, step < SIDE_CAST_STEPS))
    def _():
        for src, dst in zip(side_in, side_out):
            dst[...] = src[...].astype(BF16)


def _inproj(x2, g_mix, w_in, rot, g_ret, side_weights, tm):
    t, d = x2.shape
    cos_base, sin_base, cos_off, sin_off = rot
    assert cos_off.shape[0] == tm and w_in.shape[1] == N_W_SEC * SEC
    n_out = N_OUT_SEC * SEC
    n_steps = t // tm
    assert n_steps >= SIDE_CAST_STEPS

    def tile(i):
        return jnp.maximum(i - N_W_SEC, 0)

    def slab_spec(w):
        rows = w.shape[0] // SIDE_CAST_STEPS
        assert rows * SIDE_CAST_STEPS == w.shape[0] and rows % BF16_SUBLANES == 0
        return pl.BlockSpec((rows, w.shape[1]),
                            lambda i: (jnp.minimum(tile(i), SIDE_CAST_STEPS - 1), 0))

    side_specs = [slab_spec(w) for w in side_weights]
    outs = pl.pallas_call(
        functools.partial(_inproj_kernel, n_pos_blocks=cos_base.shape[0],
                          n_side=len(side_weights)),
        grid=(N_W_SEC + n_steps,),
        in_specs=[
            pl.BlockSpec((tm, d), lambda i: (tile(i), 0)),
            _const_spec(g_mix.shape),
            pl.BlockSpec((d, SEC), lambda i: (0, jnp.minimum(i, N_W_SEC - 1))),
            _const_spec(cos_base.shape),
            _const_spec(sin_base.shape),
            _const_spec(cos_off.shape),
            _const_spec(sin_off.shape),
            _const_spec(g_ret.shape),
        ] + side_specs,
        out_specs=[
            pl.BlockSpec((tm, n_out), lambda i: (tile(i), 0)),
            pl.BlockSpec((RET_HEADS * QK_DIM, tm), lambda i: (0, tile(i))),
        ] + side_specs,
        out_shape=[
            jax.ShapeDtypeStruct((t, n_out), BF16),
            jax.ShapeDtypeStruct((RET_HEADS * QK_DIM, t), BF16),
        ] + [jax.ShapeDtypeStruct(w.shape, BF16) for w in side_weights],
        scratch_shapes=[pltpu.VMEM((N_W_SEC, d, SEC), BF16)],
        compiler_params=pltpu.CompilerParams(
            dimension_semantics=("arbitrary",),
            vmem_limit_bytes=VMEM_LIMIT),
        name="inproj",
    )(x2, g_mix, w_in, cos_base, sin_base, cos_off, sin_off, g_ret, *side_weights)
    return outs[0], outs[1], outs[2:]


def _retention_kernel(lg_ref, q_ref, ktf_ref, vf_ref, ktb_ref, vb_ref, z_ref,
                      decay_ref, xif_ref, xib_ref, zetaf_ref, zetab_ref,
                      fstate_ref, bstate_ref, rall_ref,
                      *, n_pairs, n_blocks, chunks_per_block):
    c = RET_CHUNK
    cpb = chunks_per_block
    n_chunks = n_blocks * cpb
    p = pl.program_id(0)
    j = pl.program_id(1)
    head_f = jnp.maximum(p - 1, 0) % RET_HEADS
    head_b = jnp.minimum(p, n_pairs - 1) % RET_HEADS
    lg_f = lg_ref[0, head_f]
    lg_fb = lg_ref[1, head_f]
    lg_b = lg_ref[1, head_b]
    has_fwd = p >= 1
    has_bwd = p < n_pairs

    @pl.when(jnp.logical_and(j == 0, has_fwd))
    def _():
        row = lax.broadcasted_iota(jnp.int32, (c, c), 0).astype(F32)
        col = lax.broadcasted_iota(jnp.int32, (c, c), 1).astype(F32)
        diff = row - col
        decay_ref[...] = jnp.where(diff >= 0.0,
                                   jnp.exp(jnp.maximum(diff, 0.0) * lg_f),
                                   jnp.exp(jnp.maximum(-diff, 0.0) * lg_fb))
        zetaf_ref[...] = jnp.exp((c - 1.0 - col) * lg_f).astype(BF16)
        idx = lax.broadcasted_iota(jnp.int32, (c, LANES), 0).astype(F32)
        xif_ref[...] = jnp.exp((idx + 1.0) * lg_f).astype(BF16)
        xib_ref[...] = jnp.exp((c - idx) * lg_fb).astype(BF16)
        fstate_ref[...] = jnp.zeros_like(fstate_ref)

    @pl.when(jnp.logical_and(j == 0, has_bwd))
    def _():
        col = lax.broadcasted_iota(jnp.int32, (c, c), 1).astype(F32)
        zetab_ref[...] = jnp.exp(col * lg_b).astype(BF16)
        bstate_ref[...] = jnp.zeros_like(bstate_ref)

    def row_scaled(a, scale_ref):
        scale = scale_ref[...]
        return jnp.concatenate(
            [a[:, n * LANES:(n + 1) * LANES] * scale
             for n in range(a.shape[1] // LANES)], axis=1)

    def decayed_kv(kt_ref, v_ref, zeta_ref, ci):
        rows = slice(ci * c, (ci + 1) * c)
        return jnp.dot(kt_ref[:, rows] * zeta_ref[...], v_ref[rows, :],
                       preferred_element_type=F32)

    def bwd_sweep():
        blk = n_blocks - 1 - j
        slot = (p % 2) * n_chunks
        chunk_decay = jnp.exp(jnp.zeros((1, V_DIM), F32) + c * lg_b)
        state = bstate_ref[...]
        for ci in reversed(range(cpb)):
            rall_ref[slot + blk * cpb + ci] = state.astype(BF16)
            state = chunk_decay * state + decayed_kv(ktb_ref, vb_ref, zetab_ref, ci)
        bstate_ref[...] = state

    def fwd_sweep():
        slot = ((p - 1) % 2) * n_chunks
        chunk_decay = jnp.exp(jnp.zeros((1, V_DIM), F32) + c * lg_f)

        def decayed_scores(ci):
            rows = slice(ci * c, (ci + 1) * c)
            scores = jnp.dot(q_ref[rows, :], ktf_ref[:, rows], preferred_element_type=F32)
            return (scores * decay_ref[...]).astype(BF16)

        state = fstate_ref[...]
        pmat_next = decayed_scores(0)
        kv_next = decayed_kv(ktf_ref, vf_ref, zetaf_ref, 0)
        for ci in range(cpb):
            rows = slice(ci * c, (ci + 1) * c)
            pmat, kv = pmat_next, kv_next
            if ci + 1 < cpb:
                pmat_next = decayed_scores(ci + 1)
                kv_next = decayed_kv(ktf_ref, vf_ref, zetaf_ref, ci + 1)
            q = q_ref[rows, :]
            lhs = jnp.concatenate([pmat, row_scaled(q, xib_ref), row_scaled(q, xif_ref)],
                                  axis=1)
            rhs = jnp.concatenate([vf_ref[rows, :], rall_ref[slot + j * cpb + ci],
                                   state.astype(BF16)], axis=0)
            ret = jnp.dot(lhs, rhs, preferred_element_type=F32)
            mu = jnp.mean(ret, axis=-1, keepdims=True)
            var = jnp.mean(jnp.square(ret - mu), axis=-1, keepdims=True)
            rstd = lax.rsqrt(var + EPS)
            z_ref[rows, :] = (ret * rstd - mu * rstd).astype(BF16)
            state = chunk_decay * state + kv
        fstate_ref[...] = state

    @pl.when(jnp.logical_not(has_fwd))
    def _():
        bwd_sweep()

    @pl.when(jnp.logical_and(has_fwd, has_bwd))
    def _():
        fwd_sweep()
        bwd_sweep()

    @pl.when(jnp.logical_not(has_bwd))
    def _():
        fwd_sweep()


def _retention(lg, proj, kt, batch, seq, block_tokens):
    c = RET_CHUNK
    nb = seq // block_tokens
    cpb = block_tokens // c
    nc = seq // c
    t = batch * seq
    n_pairs = batch * RET_HEADS
    q_col = (SEC_Q * SEC) // QK_DIM
    v_col = (SEC_V * SEC) // V_DIM

    def fwd_idx(p, j):
        pair = jnp.maximum(p - 1, 0)
        return pair // RET_HEADS, pair % RET_HEADS, jnp.where(p == 0, 0, j)

    def bwd_idx(p, j):
        pair = jnp.minimum(p, n_pairs - 1)
        return pair // RET_HEADS, pair % RET_HEADS, jnp.where(p == n_pairs, 0, nb - 1 - j)

    def tok_major(idx_fn, col0):
        def index_map(p, j):
            b, h, blk = idx_fn(p, j)
            return b * nb + blk, col0 + h
        return index_map

    def head_major(idx_fn):
        def index_map(p, j):
            b, h, blk = idx_fn(p, j)
            return h, b * nb + blk
        return index_map

    return pl.pallas_call(
        functools.partial(_retention_kernel, n_pairs=n_pairs, n_blocks=nb,
                          chunks_per_block=cpb),
        grid=(n_pairs + 1, nb),
        in_specs=[
            pl.BlockSpec(memory_space=pltpu.SMEM),
            pl.BlockSpec((block_tokens, QK_DIM), tok_major(fwd_idx, q_col)),
            pl.BlockSpec((QK_DIM, block_tokens), head_major(fwd_idx)),
            pl.BlockSpec((block_tokens, V_DIM), tok_major(fwd_idx, v_col)),
            pl.BlockSpec((QK_DIM, block_tokens), head_major(bwd_idx)),
            pl.BlockSpec((block_tokens, V_DIM), tok_major(bwd_idx, v_col)),
        ],
        out_specs=pl.BlockSpec((block_tokens, V_DIM), tok_major(fwd_idx, 0)),
        out_shape=jax.ShapeDtypeStruct((t, RET_V), BF16),
        scratch_shapes=[
            pltpu.VMEM((c, c), F32),
            pltpu.VMEM((c, LANES), BF16),
            pltpu.VMEM((c, LANES), BF16),
            pltpu.VMEM((QK_DIM, c), BF16),
            pltpu.VMEM((QK_DIM, c), BF16),
            pltpu.VMEM((QK_DIM, V_DIM), F32),
            pltpu.VMEM((QK_DIM, V_DIM), F32),
            pltpu.VMEM((2 * nc, QK_DIM, V_DIM), BF16),
        ],
        compiler_params=pltpu.CompilerParams(
            dimension_semantics=("arbitrary", "arbitrary"),
            vmem_limit_bytes=VMEM_LIMIT),
        name="retention",
    )(lg, proj, kt, proj, kt, proj)


def _merge_kernel(x_ref, u_ref, gb_ref, up_ref, un_ref, ga_ref, gr_ref, z_ref, gsw_ref,
                  wconv_ref, wa_ref, wr_ref, wo_ref, o_ref, *, tiles_per_seq):
    i = pl.program_id(0)
    tm = x_ref.shape[0]
    u = u_ref[...].astype(F32)
    pos = i % tiles_per_seq
    last = BF16_SUBLANES - 1
    u_before = up_ref[last:last + 1, :].astype(F32) * (pos != 0).astype(F32)
    u_after = un_ref[0:1, :].astype(F32) * (pos != tiles_per_seq - 1).astype(F32)
    row = lax.broadcasted_iota(jnp.int32, u.shape, 0)
    u_prev = jnp.where(row == 0, u_before, pltpu.roll(u, 1, axis=0))
    u_next = jnp.where(row == tm - 1, u_after, pltpu.roll(u, tm - 1, axis=0))
    conv = u_prev * wconv_ref[0:1, :] + u * wconv_ref[1:2, :] + u_next * wconv_ref[2:3, :]
    a_in = (gb_ref[...].astype(F32) * conv).astype(BF16)

    def branch_outputs(rows):
        y_r = jnp.dot(z_ref[rows, :] * gsw_ref[rows, :], wr_ref[...],
                      preferred_element_type=F32)
        y_a = jnp.dot(a_in[rows, :], wa_ref[...], preferred_element_type=F32)
        return y_a, y_r

    blocks = [slice(r, r + SUBBLOCK_ROWS) for r in range(0, tm, SUBBLOCK_ROWS)]
    y_next = branch_outputs(blocks[0])
    for s, rows in enumerate(blocks):
        y_a, y_r = y_next
        if s + 1 < len(blocks):
            y_next = branch_outputs(blocks[s + 1])
        merged = ga_ref[rows, :].astype(F32) * y_a + gr_ref[rows, :].astype(F32) * y_r
        o_ref[rows, :] = x_ref[rows, :] + jnp.dot(merged.astype(BF16), wo_ref[...],
                                                  preferred_element_type=F32)


def _merge(x2, proj, z, w_conv, w_a, w_r, w_o, seq, tm):
    t, d = x2.shape
    hb = tm // BF16_SUBLANES
    n_halo_blocks = t // BF16_SUBLANES
    halo = (BF16_SUBLANES, SEC)
    return pl.pallas_call(
        functools.partial(_merge_kernel, tiles_per_seq=seq // tm),
        grid=(t // tm,),
        in_specs=[
            pl.BlockSpec((tm, d), lambda i: (i, 0)),
            pl.BlockSpec((tm, SEC), lambda i: (i, SEC_U)),
            pl.BlockSpec((tm, SEC), lambda i: (i, SEC_GB)),
            pl.BlockSpec(halo, lambda i: (jnp.maximum(i * hb - 1, 0), SEC_U)),
            pl.BlockSpec(halo, lambda i: (jnp.minimum((i + 1) * hb, n_halo_blocks - 1), SEC_U)),
            pl.BlockSpec((tm, SEC), lambda i: (i, SEC_GA)),
            pl.BlockSpec((tm, SEC), lambda i: (i, SEC_GR)),
            pl.BlockSpec((tm, RET_V), lambda i: (i, 0)),
            pl.BlockSpec((tm, RET_V), lambda i: (i, (SEC_GSW * SEC) // RET_V)),
            _const_spec(w_conv.shape),
            _const_spec(w_a.shape),
            _const_spec(w_r.shape),
            _const_spec(w_o.shape),
        ],
        out_specs=pl.BlockSpec((tm, d), lambda i: (i, 0)),
        out_shape=jax.ShapeDtypeStruct((t, d), F32),
        compiler_params=pltpu.CompilerParams(
            dimension_semantics=("arbitrary",),
            vmem_limit_bytes=VMEM_LIMIT),
        name="merge",
    )(x2, proj, proj, proj, proj, proj, proj, z, proj, w_conv, w_a, w_r, w_o)


def _ffn_kernel(x_ref, gffn_ref, wg_ref, wu_ref, wd_ref, gfin_ref, o_ref, *, final_norm):
    def gate_up(rows):
        h2 = (_rms_scale(x_ref[rows, :]) * gffn_ref[...]).astype(BF16)
        return (jnp.dot(h2, wg_ref[...], preferred_element_type=F32),
                jnp.dot(h2, wu_ref[...], preferred_element_type=F32))

    blocks = [slice(r, r + SUBBLOCK_ROWS) for r in range(0, x_ref.shape[0], SUBBLOCK_ROWS)]
    gu_next = gate_up(blocks[0])
    for s, rows in enumerate(blocks):
        gate, up = gu_next
        if s + 1 < len(blocks):
            gu_next = gate_up(blocks[s + 1])
        act = (gate * _sigmoid(gate) * up).astype(BF16)
        y = x_ref[rows, :] + jnp.dot(act, wd_ref[...], preferred_element_type=F32)
        o_ref[rows, :] = _rms_scale(y) * gfin_ref[...] if final_norm else y


def _ffn(x1, g_ffn, w_gate, w_up, w_down, g_final, final_norm, tm):
    t, d = x1.shape
    return pl.pallas_call(
        functools.partial(_ffn_kernel, final_norm=final_norm),
        grid=(t // tm,),
        in_specs=[
            pl.BlockSpec((tm, d), lambda i: (i, 0)),
            _const_spec(g_ffn.shape),
            _const_spec(w_gate.shape),
            _const_spec(w_up.shape),
            _const_spec(w_down.shape),
            _const_spec(g_final.shape),
        ],
        out_specs=pl.BlockSpec((tm, d), lambda i: (i, 0)),
        out_shape=jax.ShapeDtypeStruct((t, d), F32),
        compiler_params=pltpu.CompilerParams(
            dimension_semantics=("arbitrary",),
            vmem_limit_bytes=VMEM_LIMIT),
        name="ffn",
    )(x1, g_ffn, w_gate, w_up, w_down, g_final)


def _rotary_tables(seq, tm):
    freqs = ROPE_BASE ** (-jnp.arange(0, QK_DIM, 2, dtype=F32) / QK_DIM)
    base = (jnp.arange(seq // tm, dtype=F32) * tm)[:, None] * freqs[None, :]
    off = jnp.arange(tm, dtype=F32)[:, None] * freqs[None, :]
    return jnp.cos(base), jnp.sin(base), jnp.cos(off), jnp.sin(off)


def kernel(x, g_mix, w_in, w_conv, dec_f, dec_b, g_ret, w_a_out, w_r_out, w_o,
           g_ffn, w_ff_gate, w_ff_up, w_ff_down, g_final):
    batch, seq, d = x.shape
    depth = w_in.shape[0]
    assert d == D_MODEL and seq % RET_CHUNK == 0
    x2 = x.reshape(batch * seq, d)
    tm_in = 256
    rot = _rotary_tables(seq, tm_in)
    for l in range(depth):
        lg = jnp.stack([jax.nn.log_sigmoid(dec_f[l].astype(F32)),
                        jax.nn.log_sigmoid(dec_b[l].astype(F32))])
        side = [w_a_out[l], w_r_out[l], w_o[l], w_ff_gate[l], w_ff_up[l], w_ff_down[l]]
        proj, kt, (w_a, w_r, w_ob, w_gate, w_up, w_down) = _inproj(
            x2, g_mix[l][None, :], w_in[l], rot,
            g_ret[l][None, :].astype(F32), side, tm=tm_in)
        z = _retention(lg, proj, kt, batch, seq, block_tokens=2048)
        x1 = _merge(x2, proj, z, w_conv[l], w_a, w_r, w_ob, seq, tm=512)
        x2 = _ffn(x1, g_ffn[l][None, :], w_gate, w_up, w_down,
                  g_final[None, :], final_norm=(l == depth - 1), tm=512)
    return x2.reshape(batch, seq, d)
```

```python
import functools

import jax
import jax.numpy as jnp
from jax import lax
from jax.experimental import pallas as pl
from jax.experimental.pallas import tpu as pltpu

F32 = jnp.float32
BF16 = jnp.bfloat16

D_MODEL = 1024
RET_HEADS = 4
QK_DIM = D_MODEL // RET_HEADS
V_DIM = 2 * QK_DIM
RET_V = RET_HEADS * V_DIM
ROPE_BASE = 10000.0
EPS = 1e-6
CONV_K = 3

SEC = 1024
W_XC, W_GB, W_GC, W_Q, W_K, W_V, W_GSW, W_GA, W_GR, N_W_SEC = 0, 1, 2, 3, 4, 5, 7, 9, 10, 11
SEC_U, SEC_GB, SEC_GSW, SEC_V, SEC_Q, SEC_GA, SEC_GR, N_OUT_SEC = 0, 1, 2, 4, 6, 7, 8, 9

RET_CHUNK = 256
SIDE_CAST_STEPS = 16
SUBBLOCK_ROWS = 256
LANES = 128
BF16_SUBLANES = 16

VMEM_LIMIT = 56 * 1024 * 1024


def _sigmoid(x):
    return 1.0 / (1.0 + jnp.exp(-x))


def _rms_scale(x):
    return x * lax.rsqrt(jnp.mean(x * x, axis=-1, keepdims=True) + EPS)


def _const_spec(shape):
    return pl.BlockSpec(shape, lambda *_: (0,) * len(shape), pipeline_mode=pl.Buffered(1))


def _inproj_kernel(x_ref, g_ref, w_ref, cos_base_ref, sin_base_ref, cos_off_ref, sin_off_ref,
                   gret_ref, *rest, n_pos_blocks, n_side):
    side_in = rest[:n_side]
    o_ref, kt_ref = rest[n_side:n_side + 2]
    side_out = rest[n_side + 2:2 * n_side + 2]
    w_bf16_ref = rest[2 * n_side + 2]
    step = pl.program_id(0) - N_W_SEC

    @pl.when(step < 0)
    def _():
        w_bf16_ref[pl.program_id(0)] = w_ref[...].astype(BF16)

    @pl.when(step >= 0)
    def _():
        h = (_rms_scale(x_ref[...]) * g_ref[...]).astype(BF16)
        half = QK_DIM // 2
        pos_block = pl.ds(step % n_pos_blocks, 1)
        cos_b = cos_base_ref[pos_block, :]
        sin_b = sin_base_ref[pos_block, :]
        cos = cos_b * cos_off_ref[...] - sin_b * sin_off_ref[...]
        sin = sin_b * cos_off_ref[...] + cos_b * sin_off_ref[...]

        def project(first_sec, n_sec):
            return jnp.concatenate(
                [jnp.dot(h, w_bf16_ref[sec], preferred_element_type=F32)
                 for sec in range(first_sec, first_sec + n_sec)], axis=1)

        def out_cols(sec, n_sec=1):
            return slice(sec * SEC, (sec + n_sec) * SEC)

        qk = project(W_Q, 2)
        for sec in range(2):
            for hh in range(RET_HEADS):
                lo = sec * SEC + hh * QK_DIM
                t1 = qk[:, lo:lo + half]
                t2 = qk[:, lo + half:lo + QK_DIM]
                r1 = t1 * cos - t2 * sin
                r2 = t1 * sin + t2 * cos
                if sec == 0:
                    scale = QK_DIM ** -0.5
                    out = SEC_Q * SEC + hh * QK_DIM
                    o_ref[:, out:out + half] = (r1 * scale).astype(BF16)
                    o_ref[:, out + half:out + QK_DIM] = (r2 * scale).astype(BF16)
                else:
                    out = hh * QK_DIM
                    kt_ref[out:out + half, :] = r1.T.astype(BF16)
                    kt_ref[out + half:out + QK_DIM, :] = r2.T.astype(BF16)

        gsw = project(W_GSW, 2)
        o_ref[:, out_cols(SEC_GSW, 2)] = (gsw * _sigmoid(gsw) * gret_ref[...]).astype(BF16)

        gates = project(W_GA, 2)
        o_ref[:, out_cols(SEC_GA, 2)] = _sigmoid(gates).astype(BF16)

        conv_in = project(W_XC, 3)
        o_ref[:, out_cols(SEC_U)] = (conv_in[:, out_cols(W_GC)] * conv_in[:, out_cols(W_XC)]
                                     ).astype(BF16)
        o_ref[:, out_cols(SEC_GB)] = conv_in[:, out_cols(W_GB)].astype(BF16)

        o_ref[:, out_cols(SEC_V, 2)] = project(W_V, 2).astype(BF16)

    @pl.when(jnp.logical_and(step >= 0, step < SIDE_CAST_STEPS))
    def _():
        for src, dst in zip(side_in, side_out):
            dst[...] = src[...].astype(BF16)


def _inproj(x2, g_mix, w_in, rot, g_ret, side_weights, tm):
    t, d = x2.shape
    cos_base, sin_base, cos_off, sin_off = rot
    assert cos_off.shape[0] == tm and w_in.shape[1] == N_W_SEC * SEC
    n_out = N_OUT_SEC * SEC
    n_steps = t // tm
    assert n_steps >= SIDE_CAST_STEPS

    def tile(i):
        return jnp.maximum(i - N_W_SEC, 0)

    def slab_spec(w):
        rows = w.shape[0] // SIDE_CAST_STEPS
        assert rows * SIDE_CAST_STEPS == w.shape[0] and rows % BF16_SUBLANES == 0
        return pl.BlockSpec((rows, w.shape[1]),
                            lambda i: (jnp.minimum(tile(i), SIDE_CAST_STEPS - 1), 0))

    side_specs = [slab_spec(w) for w in side_weights]
    outs = pl.pallas_call(
        functools.partial(_inproj_kernel, n_pos_blocks=cos_base.shape[0],
                          n_side=len(side_weights)),
        grid=(N_W_SEC + n_steps,),
        in_specs=[
            pl.BlockSpec((tm, d), lambda i: (tile(i), 0)),
            _const_spec(g_mix.shape),
            pl.BlockSpec((d, SEC), lambda i: (0, jnp.minimum(i, N_W_SEC - 1))),
            _const_spec(cos_base.shape),
            _const_spec(sin_base.shape),
            _const_spec(cos_off.shape),
            _const_spec(sin_off.shape),
            _const_spec(g_ret.shape),
        ] + side_specs,
        out_specs=[
            pl.BlockSpec((tm, n_out), lambda i: (tile(i), 0)),
            pl.BlockSpec((RET_HEADS * QK_DIM, tm), lambda i: (0, tile(i))),
        ] + side_specs,
        out_shape=[
            jax.ShapeDtypeStruct((t, n_out), BF16),
            jax.ShapeDtypeStruct((RET_HEADS * QK_DIM, t), BF16),
        ] + [jax.ShapeDtypeStruct(w.shape, BF16) for w in side_weights],
        scratch_shapes=[pltpu.VMEM((N_W_SEC, d, SEC), BF16)],
        compiler_params=pltpu.CompilerParams(
            dimension_semantics=("arbitrary",),
            vmem_limit_bytes=VMEM_LIMIT),
        name="inproj",
    )(x2, g_mix, w_in, cos_base, sin_base, cos_off, sin_off, g_ret, *side_weights)
    return outs[0], outs[1], outs[2:]


def _retention_kernel(lg_ref, q_ref, ktf_ref, vf_ref, gsw_ref, ktb_ref, vb_ref, z_ref,
                      decay_ref, xif_ref, xib_ref, zetaf_ref, zetab_ref,
                      fstate_ref, bstate_ref, rall_ref,
                      *, n_pairs, n_blocks, chunks_per_block):
    c = RET_CHUNK
    cpb = chunks_per_block
    n_chunks = n_blocks * cpb
    p = pl.program_id(0)
    j = pl.program_id(1)
    head_f = jnp.maximum(p - 1, 0) % RET_HEADS
    head_b = jnp.minimum(p, n_pairs - 1) % RET_HEADS
    lg_f = lg_ref[0, head_f]
    lg_fb = lg_ref[1, head_f]
    lg_b = lg_ref[1, head_b]
    has_fwd = p >= 1
    has_bwd = p < n_pairs

    @pl.when(jnp.logical_and(j == 0, has_fwd))
    def _():
        row = lax.broadcasted_iota(jnp.int32, (c, c), 0).astype(F32)
        col = lax.broadcasted_iota(jnp.int32, (c, c), 1).astype(F32)
        diff = row - col
        decay_ref[...] = jnp.where(diff >= 0.0,
                                   jnp.exp(jnp.maximum(diff, 0.0) * lg_f),
                                   jnp.exp(jnp.maximum(-diff, 0.0) * lg_fb))
        zetaf_ref[...] = jnp.exp((c - 1.0 - col) * lg_f).astype(BF16)
        idx = lax.broadcasted_iota(jnp.int32, (c, LANES), 0).astype(F32)
        xif_ref[...] = jnp.exp((idx + 1.0) * lg_f).astype(BF16)
        xib_ref[...] = jnp.exp((c - idx) * lg_fb).astype(BF16)
        fstate_ref[...] = jnp.zeros_like(fstate_ref)

    @pl.when(jnp.logical_and(j == 0, has_bwd))
    def _():
        col = lax.broadcasted_iota(jnp.int32, (c, c), 1).astype(F32)
        zetab_ref[...] = jnp.exp(col * lg_b).astype(BF16)
        bstate_ref[...] = jnp.zeros_like(bstate_ref)

    def row_scaled(a, scale_ref):
        scale = scale_ref[...]
        return jnp.concatenate(
            [a[:, n * LANES:(n + 1) * LANES] * scale
             for n in range(a.shape[1] // LANES)], axis=1)

    def decayed_kv(kt_ref, v_ref, zeta_ref, ci):
        rows = slice(ci * c, (ci + 1) * c)
        return jnp.dot(kt_ref[:, rows] * zeta_ref[...], v_ref[rows, :],
                       preferred_element_type=F32)

    def bwd_sweep():
        blk = n_blocks - 1 - j
        slot = (p % 2) * n_chunks
        chunk_decay = jnp.exp(jnp.zeros((1, V_DIM), F32) + c * lg_b)
        state = bstate_ref[...]
        for ci in reversed(range(cpb)):
            rall_ref[slot + blk * cpb + ci] = state.astype(BF16)
            state = chunk_decay * state + decayed_kv(ktb_ref, vb_ref, zetab_ref, ci)
        bstate_ref[...] = state

    def fwd_sweep():
        slot = ((p - 1) % 2) * n_chunks
        chunk_decay = jnp.exp(jnp.zeros((1, V_DIM), F32) + c * lg_f)

        def decayed_scores(ci):
            rows = slice(ci * c, (ci + 1) * c)
            scores = jnp.dot(q_ref[rows, :], ktf_ref[:, rows], preferred_element_type=F32)
            return (scores * decay_ref[...]).astype(BF16)

        state = fstate_ref[...]
        pmat_next = decayed_scores(0)
        kv_next = decayed_kv(ktf_ref, vf_ref, zetaf_ref, 0)
        for ci in range(cpb):
            rows = slice(ci * c, (ci + 1) * c)
            pmat, kv = pmat_next, kv_next
            if ci + 1 < cpb:
                pmat_next = decayed_scores(ci + 1)
                kv_next = decayed_kv(ktf_ref, vf_ref, zetaf_ref, ci + 1)
            q = q_ref[rows, :]
            lhs = jnp.concatenate([pmat, row_scaled(q, xib_ref), row_scaled(q, xif_ref)],
                                  axis=1)
            rhs = jnp.concatenate([vf_ref[rows, :], rall_ref[slot + j * cpb + ci],
                                   state.astype(BF16)], axis=0)
            ret = jnp.dot(lhs, rhs, preferred_element_type=F32)
            mu = jnp.mean(ret, axis=-1, keepdims=True)
            var = jnp.mean(jnp.square(ret - mu), axis=-1, keepdims=True)
            rstd = lax.rsqrt(var + EPS)
            z_ref[rows, :] = (ret * rstd - mu * rstd).astype(BF16) * gsw_ref[rows, :]
            state = chunk_decay * state + kv
        fstate_ref[...] = state

    @pl.when(jnp.logical_not(has_fwd))
    def _():
        bwd_sweep()

    @pl.when(jnp.logical_and(has_fwd, has_bwd))
    def _():
        fwd_sweep()
        bwd_sweep()

    @pl.when(jnp.logical_not(has_bwd))
    def _():
        fwd_sweep()


def _retention(lg, proj, kt, batch, seq, block_tokens):
    c = RET_CHUNK
    nb = seq // block_tokens
    cpb = block_tokens // c
    nc = seq // c
    t = batch * seq
    n_pairs = batch * RET_HEADS
    q_col = (SEC_Q * SEC) // QK_DIM
    v_col = (SEC_V * SEC) // V_DIM
    gsw_col = (SEC_GSW * SEC) // V_DIM

    def fwd_idx(p, j):
        pair = jnp.maximum(p - 1, 0)
        return pair // RET_HEADS, pair % RET_HEADS, jnp.where(p == 0, 0, j)

    def bwd_idx(p, j):
        pair = jnp.minimum(p, n_pairs - 1)
        return pair // RET_HEADS, pair % RET_HEADS, jnp.where(p == n_pairs, 0, nb - 1 - j)

    def tok_major(idx_fn, col0):
        def index_map(p, j):
            b, h, blk = idx_fn(p, j)
            return b * nb + blk, col0 + h
        return index_map

    def head_major(idx_fn):
        def index_map(p, j):
            b, h, blk = idx_fn(p, j)
            return h, b * nb + blk
        return index_map

    return pl.pallas_call(
        functools.partial(_retention_kernel, n_pairs=n_pairs, n_blocks=nb,
                          chunks_per_block=cpb),
        grid=(n_pairs + 1, nb),
        in_specs=[
            pl.BlockSpec(memory_space=pltpu.SMEM),
            pl.BlockSpec((block_tokens, QK_DIM), tok_major(fwd_idx, q_col)),
            pl.BlockSpec((QK_DIM, block_tokens), head_major(fwd_idx)),
            pl.BlockSpec((block_tokens, V_DIM), tok_major(fwd_idx, v_col)),
            pl.BlockSpec((block_tokens, V_DIM), tok_major(fwd_idx, gsw_col)),
            pl.BlockSpec((QK_DIM, block_tokens), head_major(bwd_idx)),
            pl.BlockSpec((block_tokens, V_DIM), tok_major(bwd_idx, v_col)),
        ],
        out_specs=pl.BlockSpec((block_tokens, V_DIM), tok_major(fwd_idx, 0)),
        out_shape=jax.ShapeDtypeStruct((t, RET_V), BF16),
        scratch_shapes=[
            pltpu.VMEM((c, c), F32),
            pltpu.VMEM((c, LANES), BF16),
            pltpu.VMEM((c, LANES), BF16),
            pltpu.VMEM((QK_DIM, c), BF16),
            pltpu.VMEM((QK_DIM, c), BF16),
            pltpu.VMEM((QK_DIM, V_DIM), F32),
            pltpu.VMEM((QK_DIM, V_DIM), F32),
            pltpu.VMEM((2 * nc, QK_DIM, V_DIM), BF16),
        ],
        compiler_params=pltpu.CompilerParams(
            dimension_semantics=("arbitrary", "arbitrary"),
            vmem_limit_bytes=VMEM_LIMIT),
        name="retention",
    )(lg, proj, kt, proj, proj, kt, proj)


def _merge_kernel(x_ref, u_ref, gb_ref, up_ref, un_ref, ga_ref, gr_ref, z_ref,
                  wconv_ref, wa_ref, wr_ref, wo_ref, o_ref, *, tiles_per_seq):
    i = pl.program_id(0)
    tm = x_ref.shape[0]
    u = u_ref[...].astype(F32)
    pos = i % tiles_per_seq
    last = BF16_SUBLANES - 1
    u_before = up_ref[last:last + 1, :].astype(F32) * (pos != 0).astype(F32)
    u_after = un_ref[0:1, :].astype(F32) * (pos != tiles_per_seq - 1).astype(F32)
    row = lax.broadcasted_iota(jnp.int32, u.shape, 0)
    u_prev = jnp.where(row == 0, u_before, pltpu.roll(u, 1, axis=0))
    u_next = jnp.where(row == tm - 1, u_after, pltpu.roll(u, tm - 1, axis=0))
    conv = u_prev * wconv_ref[0:1, :] + u * wconv_ref[1:2, :] + u_next * wconv_ref[2:3, :]
    a_in = (gb_ref[...].astype(F32) * conv).astype(BF16)

    def branch_outputs(rows):
        y_r = jnp.dot(z_ref[rows, :], wr_ref[...], preferred_element_type=F32)
        y_a = jnp.dot(a_in[rows, :], wa_ref[...], preferred_element_type=F32)
        return y_a, y_r

    blocks = [slice(r, r + SUBBLOCK_ROWS) for r in range(0, tm, SUBBLOCK_ROWS)]
    y_next = branch_outputs(blocks[0])
    for s, rows in enumerate(blocks):
        y_a, y_r = y_next
        if s + 1 < len(blocks):
            y_next = branch_outputs(blocks[s + 1])
        merged = ga_ref[rows, :].astype(F32) * y_a + gr_ref[rows, :].astype(F32) * y_r
        o_ref[rows, :] = x_ref[rows, :] + jnp.dot(merged.astype(BF16), wo_ref[...],
                                                  preferred_element_type=F32)


def _merge(x2, proj, z, w_conv, w_a, w_r, w_o, seq, tm):
    t, d = x2.shape
    hb = tm // BF16_SUBLANES
    n_halo_blocks = t // BF16_SUBLANES
    halo = (BF16_SUBLANES, SEC)
    return pl.pallas_call(
        functools.partial(_merge_kernel, tiles_per_seq=seq // tm),
        grid=(t // tm,),
        in_specs=[
            pl.BlockSpec((tm, d), lambda i: (i, 0)),
            pl.BlockSpec((tm, SEC), lambda i: (i, SEC_U)),
            pl.BlockSpec((tm, SEC), lambda i: (i, SEC_GB)),
            pl.BlockSpec(halo, lambda i: (jnp.maximum(i * hb - 1, 0), SEC_U)),
            pl.BlockSpec(halo, lambda i: (jnp.minimum((i + 1) * hb, n_halo_blocks - 1), SEC_U)),
            pl.BlockSpec((tm, SEC), lambda i: (i, SEC_GA)),
            pl.BlockSpec((tm, SEC), lambda i: (i, SEC_GR)),
            pl.BlockSpec((tm, RET_V), lambda i: (i, 0)),
            _const_spec(w_conv.shape),
            _const_spec(w_a.shape),
            _const_spec(w_r.shape),
            _const_spec(w_o.shape),
        ],
        out_specs=pl.BlockSpec((tm, d), lambda i: (i, 0)),
        out_shape=jax.ShapeDtypeStruct((t, d), F32),
        compiler_params=pltpu.CompilerParams(
            dimension_semantics=("arbitrary",),
            vmem_limit_bytes=VMEM_LIMIT),
        name="merge",
    )(x2, proj, proj, proj, proj, proj, proj, z, w_conv, w_a, w_r, w_o)


def _ffn_kernel(x_ref, gffn_ref, wg_ref, wu_ref, wd_ref, gfin_ref, o_ref, *, final_norm):
    def gate_up(rows):
        h2 = (_rms_scale(x_ref[rows, :]) * gffn_ref[...]).astype(BF16)
        return (jnp.dot(h2, wg_ref[...], preferred_element_type=F32),
                jnp.dot(h2, wu_ref[...], preferred_element_type=F32))

    blocks = [slice(r, r + SUBBLOCK_ROWS) for r in range(0, x_ref.shape[0], SUBBLOCK_ROWS)]
    gu_next = gate_up(blocks[0])
    for s, rows in enumerate(blocks):
        gate, up = gu_next
        if s + 1 < len(blocks):
            gu_next = gate_up(blocks[s + 1])
        act = (gate * _sigmoid(gate) * up).astype(BF16)
        y = x_ref[rows, :] + jnp.dot(act, wd_ref[...], preferred_element_type=F32)
        o_ref[rows, :] = _rms_scale(y) * gfin_ref[...] if final_norm else y


def _ffn(x1, g_ffn, w_gate, w_up, w_down, g_final, final_norm, tm):
    t, d = x1.shape
    return pl.pallas_call(
        functools.partial(_ffn_kernel, final_norm=final_norm),
        grid=(t // tm,),
        in_specs=[
            pl.BlockSpec((tm, d), lambda i: (i, 0)),
            _const_spec(g_ffn.shape),
            _const_spec(w_gate.shape),
            _const_spec(w_up.shape),
            _const_spec(w_down.shape),
            _const_spec(g_final.shape),
        ],
        out_specs=pl.BlockSpec((tm, d), lambda i: (i, 0)),
        out_shape=jax.ShapeDtypeStruct((t, d), F32),
        compiler_params=pltpu.CompilerParams(
            dimension_semantics=("arbitrary",),
            vmem_limit_bytes=VMEM_LIMIT),
        name="ffn",
    )(x1, g_ffn, w_gate, w_up, w_down, g_final)


def _rotary_tables(seq, tm):
    freqs = ROPE_BASE ** (-jnp.arange(0, QK_DIM, 2, dtype=F32) / QK_DIM)
    base = (jnp.arange(seq // tm, dtype=F32) * tm)[:, None] * freqs[None, :]
    off = jnp.arange(tm, dtype=F32)[:, None] * freqs[None, :]
    return jnp.cos(base), jnp.sin(base), jnp.cos(off), jnp.sin(off)


def kernel(x, g_mix, w_in, w_conv, dec_f, dec_b, g_ret, w_a_out, w_r_out, w_o,
           g_ffn, w_ff_gate, w_ff_up, w_ff_down, g_final):
    batch, seq, d = x.shape
    depth = w_in.shape[0]
    assert d == D_MODEL and seq % RET_CHUNK == 0
    x2 = x.reshape(batch * seq, d)
    tm_in = 256
    rot = _rotary_tables(seq, tm_in)
    for l in range(depth):
        lg = jnp.stack([jax.nn.log_sigmoid(dec_f[l].astype(F32)),
                        jax.nn.log_sigmoid(dec_b[l].astype(F32))])
        side = [w_a_out[l], w_r_out[l], w_o[l], w_ff_gate[l], w_ff_up[l], w_ff_down[l]]
        proj, kt, (w_a, w_r, w_ob, w_gate, w_up, w_down) = _inproj(
            x2, g_mix[l][None, :], w_in[l], rot,
            g_ret[l][None, :].astype(F32), side, tm=tm_in)
        z = _retention(lg, proj, kt, batch, seq, block_tokens=2048)
        x1 = _merge(x2, proj, z, w_conv[l], w_a, w_r, w_ob, seq, tm=512)
        x2 = _ffn(x1, g_ffn[l][None, :], w_gate, w_up, w_down,
                  g_final[None, :], final_norm=(l == depth - 1), tm=512)
    return x2.reshape(batch, seq, d)
```

```python
import functools

import jax
import jax.numpy as jnp
from jax import lax
from jax.experimental import pallas as pl
from jax.experimental.pallas import tpu as pltpu

F32 = jnp.float32
BF16 = jnp.bfloat16

D_MODEL = 1024
RET_HEADS = 4
QK_DIM = D_MODEL // RET_HEADS
V_DIM = 2 * QK_DIM
RET_V = RET_HEADS * V_DIM
ROPE_BASE = 10000.0
EPS = 1e-6
CONV_K = 3

SEC = 1024
W_XC, W_GB, W_GC, W_Q, W_K, W_V, W_GSW, W_GA, W_GR, N_W_SEC = 0, 1, 2, 3, 4, 5, 7, 9, 10, 11
SEC_U, SEC_GB, SEC_GSW, SEC_GA, SEC_GR, N_OUT_SEC = 0, 1, 2, 4, 5, 6
VQ_DIM = V_DIM + QK_DIM

RET_CHUNK = 256
SIDE_CAST_STEPS = 16
SUBBLOCK_ROWS = 256
LANES = 128
BF16_SUBLANES = 16

VMEM_LIMIT = 56 * 1024 * 1024


def _sigmoid(x):
    return 1.0 / (1.0 + jnp.exp(-x))


def _rms_scale(x):
    return x * lax.rsqrt(jnp.mean(x * x, axis=-1, keepdims=True) + EPS)


def _const_spec(shape):
    return pl.BlockSpec(shape, lambda *_: (0,) * len(shape), pipeline_mode=pl.Buffered(1))


def _inproj_kernel(x_ref, g_ref, w_ref, cos_base_ref, sin_base_ref, cos_off_ref, sin_off_ref,
                   gret_ref, *rest, n_pos_blocks, n_side):
    side_in = rest[:n_side]
    o_ref, kt_ref, vq_ref = rest[n_side:n_side + 3]
    side_out = rest[n_side + 3:2 * n_side + 3]
    w_bf16_ref = rest[2 * n_side + 3]
    step = pl.program_id(0) - N_W_SEC

    @pl.when(step < 0)
    def _():
        w_bf16_ref[pl.program_id(0)] = w_ref[...].astype(BF16)

    @pl.when(step >= 0)
    def _():
        h = (_rms_scale(x_ref[...]) * g_ref[...]).astype(BF16)
        half = QK_DIM // 2
        pos_block = pl.ds(step % n_pos_blocks, 1)
        cos_b = cos_base_ref[pos_block, :]
        sin_b = sin_base_ref[pos_block, :]
        cos = cos_b * cos_off_ref[...] - sin_b * sin_off_ref[...]
        sin = sin_b * cos_off_ref[...] + cos_b * sin_off_ref[...]

        def project(first_sec, n_sec):
            return jnp.concatenate(
                [jnp.dot(h, w_bf16_ref[sec], preferred_element_type=F32)
                 for sec in range(first_sec, first_sec + n_sec)], axis=1)

        def out_cols(sec, n_sec=1):
            return slice(sec * SEC, (sec + n_sec) * SEC)

        qk = project(W_Q, 2)
        for sec in range(2):
            for hh in range(RET_HEADS):
                lo = sec * SEC + hh * QK_DIM
                t1 = qk[:, lo:lo + half]
                t2 = qk[:, lo + half:lo + QK_DIM]
                r1 = t1 * cos - t2 * sin
                r2 = t1 * sin + t2 * cos
                if sec == 0:
                    scale = QK_DIM ** -0.5
                    vq_ref[hh, :, V_DIM:V_DIM + half] = (r1 * scale).astype(BF16)
                    vq_ref[hh, :, V_DIM + half:VQ_DIM] = (r2 * scale).astype(BF16)
                else:
                    out = hh * QK_DIM
                    kt_ref[out:out + half, :] = r1.T.astype(BF16)
                    kt_ref[out + half:out + QK_DIM, :] = r2.T.astype(BF16)

        gsw = project(W_GSW, 2)
        o_ref[:, out_cols(SEC_GSW, 2)] = (gsw * _sigmoid(gsw) * gret_ref[...]).astype(BF16)

        gates = project(W_GA, 2)
        o_ref[:, out_cols(SEC_GA, 2)] = _sigmoid(gates).astype(BF16)

        conv_in = project(W_XC, 3)
        o_ref[:, out_cols(SEC_U)] = (conv_in[:, out_cols(W_GC)] * conv_in[:, out_cols(W_XC)]
                                     ).astype(BF16)
        o_ref[:, out_cols(SEC_GB)] = conv_in[:, out_cols(W_GB)].astype(BF16)

        values = project(W_V, 2)
        for hh in range(RET_HEADS):
            vq_ref[hh, :, :V_DIM] = values[:, hh * V_DIM:(hh + 1) * V_DIM].astype(BF16)

    @pl.when(jnp.logical_and(step >= 0, step < SIDE_CAST_STEPS))
    def _():
        for src, dst in zip(side_in, side_out):
            dst[...] = src[...].astype(BF16)


def _inproj(x2, g_mix, w_in, rot, g_ret, side_weights, tm):
    t, d = x2.shape
    cos_base, sin_base, cos_off, sin_off = rot
    assert cos_off.shape[0] == tm and w_in.shape[1] == N_W_SEC * SEC
    n_out = N_OUT_SEC * SEC
    n_steps = t // tm
    assert n_steps >= SIDE_CAST_STEPS

    def tile(i):
        return jnp.maximum(i - N_W_SEC, 0)

    def slab_spec(w):
        rows = w.shape[0] // SIDE_CAST_STEPS
        assert rows * SIDE_CAST_STEPS == w.shape[0] and rows % BF16_SUBLANES == 0
        return pl.BlockSpec((rows, w.shape[1]),
                            lambda i: (jnp.minimum(tile(i), SIDE_CAST_STEPS - 1), 0))

    side_specs = [slab_spec(w) for w in side_weights]
    outs = pl.pallas_call(
        functools.partial(_inproj_kernel, n_pos_blocks=cos_base.shape[0],
                          n_side=len(side_weights)),
        grid=(N_W_SEC + n_steps,),
        in_specs=[
            pl.BlockSpec((tm, d), lambda i: (tile(i), 0)),
            _const_spec(g_mix.shape),
            pl.BlockSpec((d, SEC), lambda i: (0, jnp.minimum(i, N_W_SEC - 1))),
            _const_spec(cos_base.shape),
            _const_spec(sin_base.shape),
            _const_spec(cos_off.shape),
            _const_spec(sin_off.shape),
            _const_spec(g_ret.shape),
        ] + side_specs,
        out_specs=[
            pl.BlockSpec((tm, n_out), lambda i: (tile(i), 0)),
            pl.BlockSpec((RET_HEADS * QK_DIM, tm), lambda i: (0, tile(i))),
            pl.BlockSpec((RET_HEADS, tm, VQ_DIM), lambda i: (0, tile(i), 0)),
        ] + side_specs,
        out_shape=[
            jax.ShapeDtypeStruct((t, n_out), BF16),
            jax.ShapeDtypeStruct((RET_HEADS * QK_DIM, t), BF16),
            jax.ShapeDtypeStruct((RET_HEADS, t, VQ_DIM), BF16),
        ] + [jax.ShapeDtypeStruct(w.shape, BF16) for w in side_weights],
        scratch_shapes=[pltpu.VMEM((N_W_SEC, d, SEC), BF16)],
        compiler_params=pltpu.CompilerParams(
            dimension_semantics=("arbitrary",),
            vmem_limit_bytes=VMEM_LIMIT),
        name="inproj",
    )(x2, g_mix, w_in, cos_base, sin_base, cos_off, sin_off, g_ret, *side_weights)
    return outs[0], outs[1], outs[2], outs[3:]


def _retention_kernel(lg_ref, vq_ref, ktf_ref, ktb_ref, vb_ref, z_ref,
                      decay_ref, xif_ref, xib_ref, zetaf_ref, zetab_ref,
                      fstate_ref, bstate_ref, rall_ref,
                      *, n_pairs, n_blocks, chunks_per_block):
    c = RET_CHUNK
    cpb = chunks_per_block
    n_chunks = n_blocks * cpb
    p = pl.program_id(0)
    j = pl.program_id(1)
    head_f = jnp.maximum(p - 1, 0) % RET_HEADS
    head_b = jnp.minimum(p, n_pairs - 1) % RET_HEADS
    lg_f = lg_ref[0, head_f]
    lg_fb = lg_ref[1, head_f]
    lg_b = lg_ref[1, head_b]
    has_fwd = p >= 1
    has_bwd = p < n_pairs

    @pl.when(jnp.logical_and(j == 0, has_fwd))
    def _():
        row = lax.broadcasted_iota(jnp.int32, (c, c), 0).astype(F32)
        col = lax.broadcasted_iota(jnp.int32, (c, c), 1).astype(F32)
        diff = row - col
        decay_ref[...] = jnp.where(diff >= 0.0,
                                   jnp.exp(jnp.maximum(diff, 0.0) * lg_f),
                                   jnp.exp(jnp.maximum(-diff, 0.0) * lg_fb))
        zetaf_ref[...] = jnp.exp((c - 1.0 - col) * lg_f).astype(BF16)
        idx = lax.broadcasted_iota(jnp.int32, (c, LANES), 0).astype(F32)
        xif_ref[...] = jnp.exp((idx + 1.0) * lg_f).astype(BF16)
        xib_ref[...] = jnp.exp((c - idx) * lg_fb).astype(BF16)
        fstate_ref[...] = jnp.zeros_like(fstate_ref)

    @pl.when(jnp.logical_and(j == 0, has_bwd))
    def _():
        col = lax.broadcasted_iota(jnp.int32, (c, c), 1).astype(F32)
        zetab_ref[...] = jnp.exp(col * lg_b).astype(BF16)
        bstate_ref[...] = jnp.zeros_like(bstate_ref)

    def row_scaled(a, scale_ref):
        scale = scale_ref[...]
        return jnp.concatenate(
            [a[:, n * LANES:(n + 1) * LANES] * scale
             for n in range(a.shape[1] // LANES)], axis=1)

    def decayed_kv(kt_ref, v_ref, zeta_ref, ci):
        rows = slice(ci * c, (ci + 1) * c)
        return jnp.dot(kt_ref[:, rows] * zeta_ref[...], v_ref[rows, :V_DIM],
                       preferred_element_type=F32)

    def bwd_sweep():
        blk = n_blocks - 1 - j
        slot = (p % 2) * n_chunks
        chunk_decay = jnp.exp(jnp.zeros((1, V_DIM), F32) + c * lg_b)
        state = bstate_ref[...]
        for ci in reversed(range(cpb)):
            rall_ref[slot + blk * cpb + ci] = state.astype(BF16)
            state = chunk_decay * state + decayed_kv(ktb_ref, vb_ref, zetab_ref, ci)
        bstate_ref[...] = state

    def fwd_sweep():
        slot = ((p - 1) % 2) * n_chunks
        chunk_decay = jnp.exp(jnp.zeros((1, V_DIM), F32) + c * lg_f)

        def decayed_scores(ci):
            rows = slice(ci * c, (ci + 1) * c)
            scores = jnp.dot(vq_ref[rows, V_DIM:], ktf_ref[:, rows],
                             preferred_element_type=F32)
            return (scores * decay_ref[...]).astype(BF16)

        state = fstate_ref[...]
        pmat_next = decayed_scores(0)
        kv_next = decayed_kv(ktf_ref, vq_ref, zetaf_ref, 0)
        for ci in range(cpb):
            rows = slice(ci * c, (ci + 1) * c)
            pmat, kv = pmat_next, kv_next
            if ci + 1 < cpb:
                pmat_next = decayed_scores(ci + 1)
                kv_next = decayed_kv(ktf_ref, vq_ref, zetaf_ref, ci + 1)
            q = vq_ref[rows, V_DIM:]
            lhs = jnp.concatenate([pmat, row_scaled(q, xib_ref), row_scaled(q, xif_ref)],
                                  axis=1)
            rhs = jnp.concatenate([vq_ref[rows, :V_DIM], rall_ref[slot + j * cpb + ci],
                                   state.astype(BF16)], axis=0)
            ret = jnp.dot(lhs, rhs, preferred_element_type=F32)
            mu = jnp.mean(ret, axis=-1, keepdims=True)
            var = jnp.mean(jnp.square(ret - mu), axis=-1, keepdims=True)
            rstd = lax.rsqrt(var + EPS)
            z_ref[rows, :] = (ret * rstd - mu * rstd).astype(BF16)
            state = chunk_decay * state + kv
        fstate_ref[...] = state

    @pl.when(jnp.logical_not(has_fwd))
    def _():
        bwd_sweep()

    @pl.when(jnp.logical_and(has_fwd, has_bwd))
    def _():
        fwd_sweep()
        bwd_sweep()

    @pl.when(jnp.logical_not(has_bwd))
    def _():
        fwd_sweep()


def _retention(lg, vq, kt, batch, seq, block_tokens):
    c = RET_CHUNK
    nb = seq // block_tokens
    cpb = block_tokens // c
    nc = seq // c
    t = batch * seq
    n_pairs = batch * RET_HEADS

    def fwd_idx(p, j):
        pair = jnp.maximum(p - 1, 0)
        return pair // RET_HEADS, pair % RET_HEADS, jnp.where(p == 0, 0, j)

    def bwd_idx(p, j):
        pair = jnp.minimum(p, n_pairs - 1)
        return pair // RET_HEADS, pair % RET_HEADS, jnp.where(p == n_pairs, 0, nb - 1 - j)

    def head_rows(idx_fn):
        def index_map(p, j):
            b, h, blk = idx_fn(p, j)
            return h, b * nb + blk, 0
        return index_map

    def head_major(idx_fn):
        def index_map(p, j):
            b, h, blk = idx_fn(p, j)
            return h, b * nb + blk
        return index_map

    return pl.pallas_call(
        functools.partial(_retention_kernel, n_pairs=n_pairs, n_blocks=nb,
                          chunks_per_block=cpb),
        grid=(n_pairs + 1, nb),
        in_specs=[
            pl.BlockSpec(memory_space=pltpu.SMEM),
            pl.BlockSpec((None, block_tokens, VQ_DIM), head_rows(fwd_idx)),
            pl.BlockSpec((QK_DIM, block_tokens), head_major(fwd_idx)),
            pl.BlockSpec((QK_DIM, block_tokens), head_major(bwd_idx)),
            pl.BlockSpec((None, block_tokens, V_DIM), head_rows(bwd_idx)),
        ],
        out_specs=pl.BlockSpec((None, block_tokens, V_DIM), head_rows(fwd_idx)),
        out_shape=jax.ShapeDtypeStruct((RET_HEADS, t, V_DIM), BF16),
        scratch_shapes=[
            pltpu.VMEM((c, c), F32),
            pltpu.VMEM((c, LANES), BF16),
            pltpu.VMEM((c, LANES), BF16),
            pltpu.VMEM((QK_DIM, c), BF16),
            pltpu.VMEM((QK_DIM, c), BF16),
            pltpu.VMEM((QK_DIM, V_DIM), F32),
            pltpu.VMEM((QK_DIM, V_DIM), F32),
            pltpu.VMEM((2 * nc, QK_DIM, V_DIM), BF16),
        ],
        compiler_params=pltpu.CompilerParams(
            dimension_semantics=("arbitrary", "arbitrary"),
            vmem_limit_bytes=VMEM_LIMIT),
        name="retention",
    )(lg, vq, kt, kt, vq)


def _merge_kernel(x_ref, u_ref, gb_ref, up_ref, un_ref, ga_ref, gr_ref, z_ref, gsw_ref,
                  wconv_ref, wa_ref, wr_ref, wo_ref, o_ref, *, tiles_per_seq):
    i = pl.program_id(0)
    tm = x_ref.shape[0]
    u = u_ref[...].astype(F32)
    pos = i % tiles_per_seq
    last = BF16_SUBLANES - 1
    u_before = up_ref[last:last + 1, :].astype(F32) * (pos != 0).astype(F32)
    u_after = un_ref[0:1, :].astype(F32) * (pos != tiles_per_seq - 1).astype(F32)
    row = lax.broadcasted_iota(jnp.int32, u.shape, 0)
    u_prev = jnp.where(row == 0, u_before, pltpu.roll(u, 1, axis=0))
    u_next = jnp.where(row == tm - 1, u_after, pltpu.roll(u, tm - 1, axis=0))
    conv = u_prev * wconv_ref[0:1, :] + u * wconv_ref[1:2, :] + u_next * wconv_ref[2:3, :]
    a_in = (gb_ref[...].astype(F32) * conv).astype(BF16)

    def branch_outputs(rows):
        z = jnp.concatenate([z_ref[hh, rows, :] for hh in range(RET_HEADS)], axis=1)
        y_r = jnp.dot(z * gsw_ref[rows, :], wr_ref[...], preferred_element_type=F32)
        y_a = jnp.dot(a_in[rows, :], wa_ref[...], preferred_element_type=F32)
        return y_a, y_r

    blocks = [slice(r, r + SUBBLOCK_ROWS) for r in range(0, tm, SUBBLOCK_ROWS)]
    y_next = branch_outputs(blocks[0])
    for s, rows in enumerate(blocks):
        y_a, y_r = y_next
        if s + 1 < len(blocks):
            y_next = branch_outputs(blocks[s + 1])
        merged = ga_ref[rows, :].astype(F32) * y_a + gr_ref[rows, :].astype(F32) * y_r
        o_ref[rows, :] = x_ref[rows, :] + jnp.dot(merged.astype(BF16), wo_ref[...],
                                                  preferred_element_type=F32)


def _merge(x2, proj, z, w_conv, w_a, w_r, w_o, seq, tm):
    t, d = x2.shape
    hb = tm // BF16_SUBLANES
    n_halo_blocks = t // BF16_SUBLANES
    halo = (BF16_SUBLANES, SEC)
    return pl.pallas_call(
        functools.partial(_merge_kernel, tiles_per_seq=seq // tm),
        grid=(t // tm,),
        in_specs=[
            pl.BlockSpec((tm, d), lambda i: (i, 0)),
            pl.BlockSpec((tm, SEC), lambda i: (i, SEC_U)),
            pl.BlockSpec((tm, SEC), lambda i: (i, SEC_GB)),
            pl.BlockSpec(halo, lambda i: (jnp.maximum(i * hb - 1, 0), SEC_U)),
            pl.BlockSpec(halo, lambda i: (jnp.minimum((i + 1) * hb, n_halo_blocks - 1), SEC_U)),
            pl.BlockSpec((tm, SEC), lambda i: (i, SEC_GA)),
            pl.BlockSpec((tm, SEC), lambda i: (i, SEC_GR)),
            pl.BlockSpec((RET_HEADS, tm, V_DIM), lambda i: (0, i, 0)),
            pl.BlockSpec((tm, RET_V), lambda i: (i, (SEC_GSW * SEC) // RET_V)),
            _const_spec(w_conv.shape),
            _const_spec(w_a.shape),
            _const_spec(w_r.shape),
            _const_spec(w_o.shape),
        ],
        out_specs=pl.BlockSpec((tm, d), lambda i: (i, 0)),
        out_shape=jax.ShapeDtypeStruct((t, d), F32),
        compiler_params=pltpu.CompilerParams(
            dimension_semantics=("arbitrary",),
            vmem_limit_bytes=VMEM_LIMIT),
        name="merge",
    )(x2, proj, proj, proj, proj, proj, proj, z, proj, w_conv, w_a, w_r, w_o)


def _ffn_kernel(x_ref, gffn_ref, wg_ref, wu_ref, wd_ref, gfin_ref, o_ref, *, final_norm):
    def gate_up(rows):
        h2 = (_rms_scale(x_ref[rows, :]) * gffn_ref[...]).astype(BF16)
        return (jnp.dot(h2, wg_ref[...], preferred_element_type=F32),
                jnp.dot(h2, wu_ref[...], preferred_element_type=F32))

    blocks = [slice(r, r + SUBBLOCK_ROWS) for r in range(0, x_ref.shape[0], SUBBLOCK_ROWS)]
    gu_next = gate_up(blocks[0])
    for s, rows in enumerate(blocks):
        gate, up = gu_next
        if s + 1 < len(blocks):
            gu_next = gate_up(blocks[s + 1])
        act = (gate * _sigmoid(gate) * up).astype(BF16)
        y = x_ref[rows, :] + jnp.dot(act, wd_ref[...], preferred_element_type=F32)
        o_ref[rows, :] = _rms_scale(y) * gfin_ref[...] if final_norm else y


def _ffn(x1, g_ffn, w_gate, w_up, w_down, g_final, final_norm, tm):
    t, d = x1.shape
    return pl.pallas_call(
        functools.partial(_ffn_kernel, final_norm=final_norm),
        grid=(t // tm,),
        in_specs=[
            pl.BlockSpec((tm, d), lambda i: (i, 0)),
            _const_spec(g_ffn.shape),
            _const_spec(w_gate.shape),
            _const_spec(w_up.shape),
            _const_spec(w_down.shape),
            _const_spec(g_final.shape),
        ],
        out_specs=pl.BlockSpec((tm, d), lambda i: (i, 0)),
        out_shape=jax.ShapeDtypeStruct((t, d), F32),
        compiler_params=pltpu.CompilerParams(
            dimension_semantics=("arbitrary",),
            vmem_limit_bytes=VMEM_LIMIT),
        name="ffn",
    )(x1, g_ffn, w_gate, w_up, w_down, g_final)


def _rotary_tables(seq, tm):
    freqs = ROPE_BASE ** (-jnp.arange(0, QK_DIM, 2, dtype=F32) / QK_DIM)
    base = (jnp.arange(seq // tm, dtype=F32) * tm)[:, None] * freqs[None, :]
    off = jnp.arange(tm, dtype=F32)[:, None] * freqs[None, :]
    return jnp.cos(base), jnp.sin(base), jnp.cos(off), jnp.sin(off)


def kernel(x, g_mix, w_in, w_conv, dec_f, dec_b, g_ret, w_a_out, w_r_out, w_o,
           g_ffn, w_ff_gate, w_ff_up, w_ff_down, g_final):
    batch, seq, d = x.shape
    depth = w_in.shape[0]
    assert d == D_MODEL and seq % RET_CHUNK == 0
    x2 = x.reshape(batch * seq, d)
    tm_in = 256
    rot = _rotary_tables(seq, tm_in)
    for l in range(depth):
        lg = jnp.stack([jax.nn.log_sigmoid(dec_f[l].astype(F32)),
                        jax.nn.log_sigmoid(dec_b[l].astype(F32))])
        side = [w_a_out[l], w_r_out[l], w_o[l], w_ff_gate[l], w_ff_up[l], w_ff_down[l]]
        proj, kt, vq, (w_a, w_r, w_ob, w_gate, w_up, w_down) = _inproj(
            x2, g_mix[l][None, :], w_in[l], rot,
            g_ret[l][None, :].astype(F32), side, tm=tm_in)
        z = _retention(lg, vq, kt, batch, seq, block_tokens=2048)
        x1 = _merge(x2, proj, z, w_conv[l], w_a, w_r, w_ob, seq, tm=512)
        x2 = _ffn(x1, g_ffn[l][None, :], w_gate, w_up, w_down,
                  g_final[None, :], final_norm=(l == depth - 1), tm=512)
    return x2.reshape(batch, seq, d)
```

```python
import functools

import jax
import jax.numpy as jnp
from jax import lax
from jax.experimental import pallas as pl
from jax.experimental.pallas import tpu as pltpu

F32 = jnp.float32
BF16 = jnp.bfloat16

D_MODEL = 1024
RET_HEADS = 4
QK_DIM = D_MODEL // RET_HEADS
V_DIM = 2 * QK_DIM
RET_V = RET_HEADS * V_DIM
ROPE_BASE = 10000.0
EPS = 1e-6

SEC = 1024
W_XC, W_GB, W_GC, W_Q, W_K, W_V, W_GSW, W_GA, W_GR, N_W_SEC = 0, 1, 2, 3, 4, 5, 7, 9, 10, 11
SEC_U, SEC_GB, SEC_GSW, SEC_GA, SEC_GR, N_OUT_SEC = 0, 1, 2, 4, 5, 6
VQ_DIM = V_DIM + QK_DIM

RET_CHUNK = 256
SIDE_CAST_STEPS = 16
SUBBLOCK_ROWS = 256
INPROJ_ROWS = 256
RET_BLOCK_TOKENS = 2048
MERGE_ROWS = 512
FFN_ROWS = 512
LANES = 128
BF16_SUBLANES = 16

VMEM_LIMIT = 56 * 1024 * 1024


def _sigmoid(x):
    return 1.0 / (1.0 + jnp.exp(-x))


def _rms_scale(x):
    return x * lax.rsqrt(jnp.mean(x * x, axis=-1, keepdims=True) + EPS)


def _const_spec(shape):
    return pl.BlockSpec(shape, lambda *_: (0,) * len(shape), pipeline_mode=pl.Buffered(1))


def _inproj_kernel(x_ref, g_ref, w_ref, cos_base_ref, sin_base_ref, cos_off_ref, sin_off_ref,
                   gret_ref, *rest, n_pos_blocks, n_side):
    side_in = rest[:n_side]
    o_ref, kt_ref, vq_ref = rest[n_side:n_side + 3]
    side_out = rest[n_side + 3:2 * n_side + 3]
    w_bf16_ref = rest[2 * n_side + 3]
    step = pl.program_id(0) - N_W_SEC

    @pl.when(step < 0)
    def _():
        w_bf16_ref[pl.program_id(0)] = w_ref[...].astype(BF16)

    @pl.when(step >= 0)
    def _():
        h = (_rms_scale(x_ref[...]) * g_ref[...]).astype(BF16)
        half = QK_DIM // 2
        pos_block = pl.ds(step % n_pos_blocks, 1)
        cos_b = cos_base_ref[pos_block, :]
        sin_b = sin_base_ref[pos_block, :]
        cos = cos_b * cos_off_ref[...] - sin_b * sin_off_ref[...]
        sin = sin_b * cos_off_ref[...] + cos_b * sin_off_ref[...]

        def project(first_sec, n_sec):
            return jnp.concatenate(
                [jnp.dot(h, w_bf16_ref[sec], preferred_element_type=F32)
                 for sec in range(first_sec, first_sec + n_sec)], axis=1)

        def out_cols(sec, n_sec=1):
            return slice(sec * SEC, (sec + n_sec) * SEC)

        qk = project(W_Q, 2)
        for sec in range(2):
            for hh in range(RET_HEADS):
                lo = sec * SEC + hh * QK_DIM
                t1 = qk[:, lo:lo + half]
                t2 = qk[:, lo + half:lo + QK_DIM]
                r1 = t1 * cos - t2 * sin
                r2 = t1 * sin + t2 * cos
                if sec == 0:
                    scale = QK_DIM ** -0.5
                    vq_ref[hh, :, V_DIM:V_DIM + half] = (r1 * scale).astype(BF16)
                    vq_ref[hh, :, V_DIM + half:VQ_DIM] = (r2 * scale).astype(BF16)
                else:
                    out = hh * QK_DIM
                    kt_ref[out:out + half, :] = r1.T.astype(BF16)
                    kt_ref[out + half:out + QK_DIM, :] = r2.T.astype(BF16)

        gsw = project(W_GSW, 2)
        o_ref[:, out_cols(SEC_GSW, 2)] = (gsw * _sigmoid(gsw) * gret_ref[...]).astype(BF16)

        gates = project(W_GA, 2)
        o_ref[:, out_cols(SEC_GA, 2)] = _sigmoid(gates).astype(BF16)

        conv_in = project(W_XC, 3)
        o_ref[:, out_cols(SEC_U)] = (conv_in[:, out_cols(W_GC)] * conv_in[:, out_cols(W_XC)]
                                     ).astype(BF16)
        o_ref[:, out_cols(SEC_GB)] = conv_in[:, out_cols(W_GB)].astype(BF16)

        values = project(W_V, 2)
        for hh in range(RET_HEADS):
            vq_ref[hh, :, :V_DIM] = values[:, hh * V_DIM:(hh + 1) * V_DIM].astype(BF16)

    @pl.when(jnp.logical_and(step >= 0, step < SIDE_CAST_STEPS))
    def _():
        for src, dst in zip(side_in, side_out):
            dst[...] = src[...].astype(BF16)


def _inproj(x2, g_mix, w_in, rot, g_ret, side_weights, tm):
    t, d = x2.shape
    cos_base, sin_base, cos_off, sin_off = rot
    assert cos_off.shape[0] == tm and w_in.shape[1] == N_W_SEC * SEC
    n_out = N_OUT_SEC * SEC
    n_steps = t // tm
    assert n_steps >= SIDE_CAST_STEPS

    def tile(i):
        return jnp.maximum(i - N_W_SEC, 0)

    def slab_spec(w):
        rows = w.shape[0] // SIDE_CAST_STEPS
        assert rows * SIDE_CAST_STEPS == w.shape[0] and rows % BF16_SUBLANES == 0
        return pl.BlockSpec((rows, w.shape[1]),
                            lambda i: (jnp.minimum(tile(i), SIDE_CAST_STEPS - 1), 0))

    side_specs = [slab_spec(w) for w in side_weights]
    outs = pl.pallas_call(
        functools.partial(_inproj_kernel, n_pos_blocks=cos_base.shape[0],
                          n_side=len(side_weights)),
        grid=(N_W_SEC + n_steps,),
        in_specs=[
            pl.BlockSpec((tm, d), lambda i: (tile(i), 0)),
            _const_spec(g_mix.shape),
            pl.BlockSpec((d, SEC), lambda i: (0, jnp.minimum(i, N_W_SEC - 1))),
            _const_spec(cos_base.shape),
            _const_spec(sin_base.shape),
            _const_spec(cos_off.shape),
            _const_spec(sin_off.shape),
            _const_spec(g_ret.shape),
        ] + side_specs,
        out_specs=[
            pl.BlockSpec((tm, n_out), lambda i: (tile(i), 0)),
            pl.BlockSpec((RET_HEADS * QK_DIM, tm), lambda i: (0, tile(i))),
            pl.BlockSpec((RET_HEADS, tm, VQ_DIM), lambda i: (0, tile(i), 0)),
        ] + side_specs,
        out_shape=[
            jax.ShapeDtypeStruct((t, n_out), BF16),
            jax.ShapeDtypeStruct((RET_HEADS * QK_DIM, t), BF16),
            jax.ShapeDtypeStruct((RET_HEADS, t, VQ_DIM), BF16),
        ] + [jax.ShapeDtypeStruct(w.shape, BF16) for w in side_weights],
        scratch_shapes=[pltpu.VMEM((N_W_SEC, d, SEC), BF16)],
        compiler_params=pltpu.CompilerParams(
            dimension_semantics=("arbitrary",),
            vmem_limit_bytes=VMEM_LIMIT),
        name="inproj",
    )(x2, g_mix, w_in, cos_base, sin_base, cos_off, sin_off, g_ret, *side_weights)
    return outs[0], outs[1], outs[2], outs[3:]


def _retention_kernel(lg_ref, vq_ref, ktf_ref, ktb_ref, vb_ref, z_ref,
                      decay_ref, xif_ref, xib_ref, zetaf_ref, zetab_ref,
                      fstate_ref, bstate_ref, rall_ref,
                      *, n_pairs, n_blocks, chunks_per_block):
    c = RET_CHUNK
    cpb = chunks_per_block
    n_chunks = n_blocks * cpb
    p = pl.program_id(0)
    j = pl.program_id(1)
    head_f = jnp.maximum(p - 1, 0) % RET_HEADS
    head_b = jnp.minimum(p, n_pairs - 1) % RET_HEADS
    lg_f = lg_ref[0, head_f]
    lg_fb = lg_ref[1, head_f]
    lg_b = lg_ref[1, head_b]
    has_fwd = p >= 1
    has_bwd = p < n_pairs

    @pl.when(jnp.logical_and(j == 0, has_fwd))
    def _():
        row = lax.broadcasted_iota(jnp.int32, (c, c), 0).astype(F32)
        col = lax.broadcasted_iota(jnp.int32, (c, c), 1).astype(F32)
        diff = row - col
        decay_ref[...] = jnp.where(diff >= 0.0,
                                   jnp.exp(jnp.maximum(diff, 0.0) * lg_f),
                                   jnp.exp(jnp.maximum(-diff, 0.0) * lg_fb))
        zetaf_ref[...] = jnp.exp((c - 1.0 - col) * lg_f).astype(BF16)
        idx = lax.broadcasted_iota(jnp.int32, (c, LANES), 0).astype(F32)
        xif_ref[...] = jnp.exp((idx + 1.0) * lg_f).astype(BF16)
        xib_ref[...] = jnp.exp((c - idx) * lg_fb).astype(BF16)
        fstate_ref[...] = jnp.zeros_like(fstate_ref)

    @pl.when(jnp.logical_and(j == 0, has_bwd))
    def _():
        col = lax.broadcasted_iota(jnp.int32, (c, c), 1).astype(F32)
        zetab_ref[...] = jnp.exp(col * lg_b).astype(BF16)
        bstate_ref[...] = jnp.zeros_like(bstate_ref)

    def row_scaled(a, scale_ref):
        scale = scale_ref[...]
        return jnp.concatenate(
            [a[:, n * LANES:(n + 1) * LANES] * scale
             for n in range(a.shape[1] // LANES)], axis=1)

    def decayed_kv(kt_ref, v_ref, zeta_ref, ci):
        rows = slice(ci * c, (ci + 1) * c)
        return jnp.dot(kt_ref[:, rows] * zeta_ref[...], v_ref[rows, :V_DIM],
                       preferred_element_type=F32)

    def bwd_sweep():
        blk = n_blocks - 1 - j
        slot = (p % 2) * n_chunks
        chunk_decay = jnp.exp(jnp.zeros((1, V_DIM), F32) + c * lg_b)
        state = bstate_ref[...]
        for ci in reversed(range(cpb)):
            rall_ref[slot + blk * cpb + ci] = state.astype(BF16)
            state = chunk_decay * state + decayed_kv(ktb_ref, vb_ref, zetab_ref, ci)
        bstate_ref[...] = state

    def fwd_sweep():
        slot = ((p - 1) % 2) * n_chunks
        chunk_decay = jnp.exp(jnp.zeros((1, V_DIM), F32) + c * lg_f)

        def decayed_scores(ci):
            rows = slice(ci * c, (ci + 1) * c)
            scores = jnp.dot(vq_ref[rows, V_DIM:], ktf_ref[:, rows],
                             preferred_element_type=F32)
            return (scores * decay_ref[...]).astype(BF16)

        state = fstate_ref[...]
        pmat_next = decayed_scores(0)
        kv_next = decayed_kv(ktf_ref, vq_ref, zetaf_ref, 0)
        for ci in range(cpb):
            rows = slice(ci * c, (ci + 1) * c)
            pmat, kv = pmat_next, kv_next
            if ci + 1 < cpb:
                pmat_next = decayed_scores(ci + 1)
                kv_next = decayed_kv(ktf_ref, vq_ref, zetaf_ref, ci + 1)
            q = vq_ref[rows, V_DIM:]
            lhs = jnp.concatenate([pmat, row_scaled(q, xib_ref), row_scaled(q, xif_ref)],
                                  axis=1)
            rhs = jnp.concatenate([vq_ref[rows, :V_DIM], rall_ref[slot + j * cpb + ci],
                                   state.astype(BF16)], axis=0)
            ret = jnp.dot(lhs, rhs, preferred_element_type=F32)
            mu = jnp.mean(ret, axis=-1, keepdims=True)
            var = jnp.mean(jnp.square(ret - mu), axis=-1, keepdims=True)
            rstd = lax.rsqrt(var + EPS)
            z_ref[rows, :] = (ret * rstd - mu * rstd).astype(BF16)
            state = chunk_decay * state + kv
        fstate_ref[...] = state

    @pl.when(jnp.logical_not(has_fwd))
    def _():
        bwd_sweep()

    @pl.when(jnp.logical_and(has_fwd, has_bwd))
    def _():
        fwd_sweep()
        bwd_sweep()

    @pl.when(jnp.logical_not(has_bwd))
    def _():
        fwd_sweep()


def _retention(lg, vq, kt, batch, seq, block_tokens):
    c = RET_CHUNK
    nb = seq // block_tokens
    cpb = block_tokens // c
    nc = seq // c
    t = batch * seq
    n_pairs = batch * RET_HEADS

    def fwd_idx(p, j):
        pair = jnp.maximum(p - 1, 0)
        return pair // RET_HEADS, pair % RET_HEADS, jnp.where(p == 0, 0, j)

    def bwd_idx(p, j):
        pair = jnp.minimum(p, n_pairs - 1)
        return pair // RET_HEADS, pair % RET_HEADS, jnp.where(p == n_pairs, 0, nb - 1 - j)

    def head_rows(idx_fn):
        def index_map(p, j):
            b, h, blk = idx_fn(p, j)
            return h, b * nb + blk, 0
        return index_map

    def head_major(idx_fn):
        def index_map(p, j):
            b, h, blk = idx_fn(p, j)
            return h, b * nb + blk
        return index_map

    return pl.pallas_call(
        functools.partial(_retention_kernel, n_pairs=n_pairs, n_blocks=nb,
                          chunks_per_block=cpb),
        grid=(n_pairs + 1, nb),
        in_specs=[
            pl.BlockSpec(memory_space=pltpu.SMEM),
            pl.BlockSpec((None, block_tokens, VQ_DIM), head_rows(fwd_idx)),
            pl.BlockSpec((QK_DIM, block_tokens), head_major(fwd_idx)),
            pl.BlockSpec((QK_DIM, block_tokens), head_major(bwd_idx)),
            pl.BlockSpec((None, block_tokens, V_DIM), head_rows(bwd_idx)),
        ],
        out_specs=pl.BlockSpec((None, block_tokens, V_DIM), head_rows(fwd_idx)),
        out_shape=jax.ShapeDtypeStruct((RET_HEADS, t, V_DIM), BF16),
        scratch_shapes=[
            pltpu.VMEM((c, c), F32),
            pltpu.VMEM((c, LANES), BF16),
            pltpu.VMEM((c, LANES), BF16),
            pltpu.VMEM((QK_DIM, c), BF16),
            pltpu.VMEM((QK_DIM, c), BF16),
            pltpu.VMEM((QK_DIM, V_DIM), F32),
            pltpu.VMEM((QK_DIM, V_DIM), F32),
            pltpu.VMEM((2 * nc, QK_DIM, V_DIM), BF16),
        ],
        compiler_params=pltpu.CompilerParams(
            dimension_semantics=("arbitrary", "arbitrary"),
            vmem_limit_bytes=VMEM_LIMIT),
        name="retention",
    )(lg, vq, kt, kt, vq)


def _merge_kernel(x_ref, u_ref, gb_ref, up_ref, un_ref, ga_ref, gr_ref, z_ref, gsw_ref,
                  wconv_ref, wa_ref, wr_ref, wo_ref, o_ref, *, tiles_per_seq):
    i = pl.program_id(0)
    tm = x_ref.shape[0]
    u = u_ref[...].astype(F32)
    pos = i % tiles_per_seq
    last = BF16_SUBLANES - 1
    u_before = up_ref[last:last + 1, :].astype(F32) * (pos != 0).astype(F32)
    u_after = un_ref[0:1, :].astype(F32) * (pos != tiles_per_seq - 1).astype(F32)
    row = lax.broadcasted_iota(jnp.int32, u.shape, 0)
    u_prev = jnp.where(row == 0, u_before, pltpu.roll(u, 1, axis=0))
    u_next = jnp.where(row == tm - 1, u_after, pltpu.roll(u, tm - 1, axis=0))
    conv = u_prev * wconv_ref[0:1, :] + u * wconv_ref[1:2, :] + u_next * wconv_ref[2:3, :]
    a_in = (gb_ref[...].astype(F32) * conv).astype(BF16)

    def branch_outputs(rows):
        z = jnp.concatenate([z_ref[hh, rows, :] for hh in range(RET_HEADS)], axis=1)
        y_r = jnp.dot(z * gsw_ref[rows, :], wr_ref[...], preferred_element_type=F32)
        y_a = jnp.dot(a_in[rows, :], wa_ref[...], preferred_element_type=F32)
        return y_a, y_r

    blocks = [slice(r, r + SUBBLOCK_ROWS) for r in range(0, tm, SUBBLOCK_ROWS)]
    y_next = branch_outputs(blocks[0])
    for s, rows in enumerate(blocks):
        y_a, y_r = y_next
        if s + 1 < len(blocks):
            y_next = branch_outputs(blocks[s + 1])
        merged = ga_ref[rows, :].astype(F32) * y_a + gr_ref[rows, :].astype(F32) * y_r
        o_ref[rows, :] = x_ref[rows, :] + jnp.dot(merged.astype(BF16), wo_ref[...],
                                                  preferred_element_type=F32)


def _merge(x2, proj, z, w_conv, w_a, w_r, w_o, seq, tm):
    t, d = x2.shape
    hb = tm // BF16_SUBLANES
    n_halo_blocks = t // BF16_SUBLANES
    halo = (BF16_SUBLANES, SEC)
    return pl.pallas_call(
        functools.partial(_merge_kernel, tiles_per_seq=seq // tm),
        grid=(t // tm,),
        in_specs=[
            pl.BlockSpec((tm, d), lambda i: (i, 0)),
            pl.BlockSpec((tm, SEC), lambda i: (i, SEC_U)),
            pl.BlockSpec((tm, SEC), lambda i: (i, SEC_GB)),
            pl.BlockSpec(halo, lambda i: (jnp.maximum(i * hb - 1, 0), SEC_U)),
            pl.BlockSpec(halo, lambda i: (jnp.minimum((i + 1) * hb, n_halo_blocks - 1), SEC_U)),
            pl.BlockSpec((tm, SEC), lambda i: (i, SEC_GA)),
            pl.BlockSpec((tm, SEC), lambda i: (i, SEC_GR)),
            pl.BlockSpec((RET_HEADS, tm, V_DIM), lambda i: (0, i, 0)),
            pl.BlockSpec((tm, RET_V), lambda i: (i, (SEC_GSW * SEC) // RET_V)),
            _const_spec(w_conv.shape),
            _const_spec(w_a.shape),
            _const_spec(w_r.shape),
            _const_spec(w_o.shape),
        ],
        out_specs=pl.BlockSpec((tm, d), lambda i: (i, 0)),
        out_shape=jax.ShapeDtypeStruct((t, d), F32),
        compiler_params=pltpu.CompilerParams(
            dimension_semantics=("arbitrary",),
            vmem_limit_bytes=VMEM_LIMIT),
        name="merge",
    )(x2, proj, proj, proj, proj, proj, proj, z, proj, w_conv, w_a, w_r, w_o)


def _ffn_kernel(x_ref, gffn_ref, wg_ref, wu_ref, wd_ref, gfin_ref, o_ref, *, final_norm):
    def gate_up(rows):
        h2 = (_rms_scale(x_ref[rows, :]) * gffn_ref[...]).astype(BF16)
        return (jnp.dot(h2, wg_ref[...], preferred_element_type=F32),
                jnp.dot(h2, wu_ref[...], preferred_element_type=F32))

    blocks = [slice(r, r + SUBBLOCK_ROWS) for r in range(0, x_ref.shape[0], SUBBLOCK_ROWS)]
    gu_next = gate_up(blocks[0])
    for s, rows in enumerate(blocks):
        gate, up = gu_next
        if s + 1 < len(blocks):
            gu_next = gate_up(blocks[s + 1])
        act = (gate * _sigmoid(gate) * up).astype(BF16)
        y = x_ref[rows, :] + jnp.dot(act, wd_ref[...], preferred_element_type=F32)
        o_ref[rows, :] = _rms_scale(y) * gfin_ref[...] if final_norm else y


def _ffn(x1, g_ffn, w_gate, w_up, w_down, g_final, final_norm, tm):
    t, d = x1.shape
    return pl.pallas_call(
        functools.partial(_ffn_kernel, final_norm=final_norm),
        grid=(t // tm,),
        in_specs=[
            pl.BlockSpec((tm, d), lambda i: (i, 0)),
            _const_spec(g_ffn.shape),
            _const_spec(w_gate.shape),
            _const_spec(w_up.shape),
            _const_spec(w_down.shape),
            _const_spec(g_final.shape),
        ],
        out_specs=pl.BlockSpec((tm, d), lambda i: (i, 0)),
        out_shape=jax.ShapeDtypeStruct((t, d), F32),
        compiler_params=pltpu.CompilerParams(
            dimension_semantics=("arbitrary",),
            vmem_limit_bytes=VMEM_LIMIT),
        name="ffn",
    )(x1, g_ffn, w_gate, w_up, w_down, g_final)


def _rotary_tables(seq, tm):
    freqs = ROPE_BASE ** (-jnp.arange(0, QK_DIM, 2, dtype=F32) / QK_DIM)
    base = (jnp.arange(seq // tm, dtype=F32) * tm)[:, None] * freqs[None, :]
    off = jnp.arange(tm, dtype=F32)[:, None] * freqs[None, :]
    return jnp.cos(base), jnp.sin(base), jnp.cos(off), jnp.sin(off)


def kernel(x, g_mix, w_in, w_conv, dec_f, dec_b, g_ret, w_a_out, w_r_out, w_o,
           g_ffn, w_ff_gate, w_ff_up, w_ff_down, g_final):
    batch, seq, d = x.shape
    depth = w_in.shape[0]
    assert d == D_MODEL and seq % RET_BLOCK_TOKENS == 0 and seq % MERGE_ROWS == 0
    x2 = x.reshape(batch * seq, d)
    rot = _rotary_tables(seq, INPROJ_ROWS)
    for l in range(depth):
        lg = jnp.stack([jax.nn.log_sigmoid(dec_f[l].astype(F32)),
                        jax.nn.log_sigmoid(dec_b[l].astype(F32))])
        side = [w_a_out[l], w_r_out[l], w_o[l], w_ff_gate[l], w_ff_up[l], w_ff_down[l]]
        proj, kt, vq, (w_a, w_r, w_ob, w_gate, w_up, w_down) = _inproj(
            x2, g_mix[l][None, :], w_in[l], rot,
            g_ret[l][None, :].astype(F32), side, tm=INPROJ_ROWS)
        z = _retention(lg, vq, kt, batch, seq, block_tokens=RET_BLOCK_TOKENS)
        x1 = _merge(x2, proj, z, w_conv[l], w_a, w_r, w_ob, seq, tm=MERGE_ROWS)
        x2 = _ffn(x1, g_ffn[l][None, :], w_gate, w_up, w_down,
                  g_final[None, :], final_norm=(l == depth - 1), tm=FFN_ROWS)
    return x2.reshape(batch, seq, d)
```

```python
import functools

import jax
import jax.numpy as jnp
from jax import lax
from jax.experimental import pallas as pl
from jax.experimental.pallas import tpu as pltpu

F32 = jnp.float32
BF16 = jnp.bfloat16

D_MODEL = 1024
RET_HEADS = 4
QK_DIM = D_MODEL // RET_HEADS
V_DIM = 2 * QK_DIM
RET_V = RET_HEADS * V_DIM
ROPE_BASE = 10000.0
EPS = 1e-6

SEC = 1024
W_XC, W_GB, W_GC, W_Q, W_K, W_V, W_GSW, W_GA, W_GR, N_W_SEC = 0, 1, 2, 3, 4, 5, 7, 9, 10, 11
SEC_U, SEC_GB, SEC_GSW, SEC_GA, SEC_GR, N_OUT_SEC = 0, 1, 2, 4, 5, 6
VQ_DIM = V_DIM + QK_DIM

RET_CHUNK = 256
SIDE_CAST_STEPS = 16
SUBBLOCK_ROWS = 256
INPROJ_ROWS = 256
RET_BLOCK_TOKENS = 4096
MERGE_ROWS = 512
FFN_ROWS = 512
LANES = 128
BF16_SUBLANES = 16

VMEM_LIMIT = 56 * 1024 * 1024


def _sigmoid(x):
    return 1.0 / (1.0 + jnp.exp(-x))


def _rms_scale(x):
    return x * lax.rsqrt(jnp.mean(x * x, axis=-1, keepdims=True) + EPS)


def _const_spec(shape):
    return pl.BlockSpec(shape, lambda *_: (0,) * len(shape), pipeline_mode=pl.Buffered(1))


def _inproj_kernel(x_ref, g_ref, w_top_ref, w_bot_ref, cos_base_ref, sin_base_ref,
                   cos_off_ref, sin_off_ref, gret_ref, *rest, n_pos_blocks, n_side):
    side_in = rest[:n_side]
    o_ref, kt_ref, vq_ref = rest[n_side:n_side + 3]
    side_out = rest[n_side + 3:2 * n_side + 3]
    w_bf16_ref = rest[2 * n_side + 3]
    step = pl.program_id(0) - N_W_SEC

    @pl.when(step < 0)
    def _():
        half_rows = w_top_ref.shape[0]
        w_bf16_ref[pl.program_id(0), :half_rows, :] = w_top_ref[...].astype(BF16)
        w_bf16_ref[pl.program_id(0), half_rows:, :] = w_bot_ref[...].astype(BF16)

    @pl.when(step >= 0)
    def _():
        h = (_rms_scale(x_ref[...]) * g_ref[...]).astype(BF16)
        half = QK_DIM // 2
        pos_block = pl.ds(step % n_pos_blocks, 1)
        cos_b = cos_base_ref[pos_block, :]
        sin_b = sin_base_ref[pos_block, :]
        cos = cos_b * cos_off_ref[...] - sin_b * sin_off_ref[...]
        sin = sin_b * cos_off_ref[...] + cos_b * sin_off_ref[...]

        def project(first_sec, n_sec):
            return jnp.concatenate(
                [jnp.dot(h, w_bf16_ref[sec], preferred_element_type=F32)
                 for sec in range(first_sec, first_sec + n_sec)], axis=1)

        def out_cols(sec, n_sec=1):
            return slice(sec * SEC, (sec + n_sec) * SEC)

        qk = project(W_Q, 2)
        for sec in range(2):
            for hh in range(RET_HEADS):
                lo = sec * SEC + hh * QK_DIM
                t1 = qk[:, lo:lo + half]
                t2 = qk[:, lo + half:lo + QK_DIM]
                r1 = t1 * cos - t2 * sin
                r2 = t1 * sin + t2 * cos
                if sec == 0:
                    scale = QK_DIM ** -0.5
                    vq_ref[hh, :, V_DIM:V_DIM + half] = (r1 * scale).astype(BF16)
                    vq_ref[hh, :, V_DIM + half:VQ_DIM] = (r2 * scale).astype(BF16)
                else:
                    out = hh * QK_DIM
                    kt_ref[out:out + half, :] = r1.T.astype(BF16)
                    kt_ref[out + half:out + QK_DIM, :] = r2.T.astype(BF16)

        gsw = project(W_GSW, 2)
        o_ref[:, out_cols(SEC_GSW, 2)] = (gsw * _sigmoid(gsw) * gret_ref[...]).astype(BF16)

        gates = project(W_GA, 2)
        o_ref[:, out_cols(SEC_GA, 2)] = _sigmoid(gates).astype(BF16)

        conv_in = project(W_XC, 3)
        o_ref[:, out_cols(SEC_U)] = (conv_in[:, out_cols(W_GC)] * conv_in[:, out_cols(W_XC)]
                                     ).astype(BF16)
        o_ref[:, out_cols(SEC_GB)] = conv_in[:, out_cols(W_GB)].astype(BF16)

        values = project(W_V, 2)
        for hh in range(RET_HEADS):
            vq_ref[hh, :, :V_DIM] = values[:, hh * V_DIM:(hh + 1) * V_DIM].astype(BF16)

    @pl.when(jnp.logical_and(step >= 0, step < SIDE_CAST_STEPS))
    def _():
        for src, dst in zip(side_in, side_out):
            dst[...] = src[...].astype(BF16)


def _inproj(x2, g_mix, w_in, rot, g_ret, side_weights, tm):
    t, d = x2.shape
    cos_base, sin_base, cos_off, sin_off = rot
    assert cos_off.shape[0] == tm and w_in.shape[1] == N_W_SEC * SEC
    n_out = N_OUT_SEC * SEC
    n_steps = t // tm
    assert n_steps >= SIDE_CAST_STEPS

    def tile(i):
        return jnp.maximum(i - N_W_SEC, 0)

    def slab_spec(w):
        rows = w.shape[0] // SIDE_CAST_STEPS
        assert rows * SIDE_CAST_STEPS == w.shape[0] and rows % BF16_SUBLANES == 0
        return pl.BlockSpec((rows, w.shape[1]),
                            lambda i: (jnp.minimum(tile(i), SIDE_CAST_STEPS - 1), 0))

    side_specs = [slab_spec(w) for w in side_weights]
    outs = pl.pallas_call(
        functools.partial(_inproj_kernel, n_pos_blocks=cos_base.shape[0],
                          n_side=len(side_weights)),
        grid=(N_W_SEC + n_steps,),
        in_specs=[
            pl.BlockSpec((tm, d), lambda i: (tile(i), 0)),
            _const_spec(g_mix.shape),
            pl.BlockSpec((d // 2, SEC), lambda i: (0, jnp.minimum(i, N_W_SEC - 1))),
            pl.BlockSpec((d // 2, SEC), lambda i: (1, jnp.minimum(i, N_W_SEC - 1))),
            _const_spec(cos_base.shape),
            _const_spec(sin_base.shape),
            _const_spec(cos_off.shape),
            _const_spec(sin_off.shape),
            _const_spec(g_ret.shape),
        ] + side_specs,
        out_specs=[
            pl.BlockSpec((tm, n_out), lambda i: (tile(i), 0)),
            pl.BlockSpec((RET_HEADS * QK_DIM, tm), lambda i: (0, tile(i))),
            pl.BlockSpec((RET_HEADS, tm, VQ_DIM), lambda i: (0, tile(i), 0)),
        ] + side_specs,
        out_shape=[
            jax.ShapeDtypeStruct((t, n_out), BF16),
            jax.ShapeDtypeStruct((RET_HEADS * QK_DIM, t), BF16),
            jax.ShapeDtypeStruct((RET_HEADS, t, VQ_DIM), BF16),
        ] + [jax.ShapeDtypeStruct(w.shape, BF16) for w in side_weights],
        scratch_shapes=[pltpu.VMEM((N_W_SEC, d, SEC), BF16)],
        compiler_params=pltpu.CompilerParams(
            dimension_semantics=("arbitrary",),
            vmem_limit_bytes=VMEM_LIMIT),
        name="inproj",
    )(x2, g_mix, w_in, w_in, cos_base, sin_base, cos_off, sin_off, g_ret, *side_weights)
    return outs[0], outs[1], outs[2], outs[3:]


def _retention_kernel(lg_ref, vq_ref, ktf_ref, ktb_ref, vb_ref, z_ref,
                      decay_ref, xif_ref, xib_ref, zetaf_ref, zetab_ref,
                      fstate_ref, bstate_ref, rall_ref,
                      *, n_pairs, n_blocks, chunks_per_block):
    c = RET_CHUNK
    cpb = chunks_per_block
    n_chunks = n_blocks * cpb
    p = pl.program_id(0)
    j = pl.program_id(1)
    head_f = jnp.maximum(p - 1, 0) % RET_HEADS
    head_b = jnp.minimum(p, n_pairs - 1) % RET_HEADS
    lg_f = lg_ref[0, head_f]
    lg_fb = lg_ref[1, head_f]
    lg_b = lg_ref[1, head_b]
    has_fwd = p >= 1
    has_bwd = p < n_pairs

    @pl.when(jnp.logical_and(j == 0, has_fwd))
    def _():
        row = lax.broadcasted_iota(jnp.int32, (c, c), 0).astype(F32)
        col = lax.broadcasted_iota(jnp.int32, (c, c), 1).astype(F32)
        diff = row - col
        decay_ref[...] = jnp.where(diff >= 0.0,
                                   jnp.exp(jnp.maximum(diff, 0.0) * lg_f),
                                   jnp.exp(jnp.maximum(-diff, 0.0) * lg_fb))
        zetaf_ref[...] = jnp.exp((c - 1.0 - col) * lg_f).astype(BF16)
        idx = lax.broadcasted_iota(jnp.int32, (c, LANES), 0).astype(F32)
        xif_ref[...] = jnp.exp((idx + 1.0) * lg_f).astype(BF16)
        xib_ref[...] = jnp.exp((c - idx) * lg_fb).astype(BF16)
        fstate_ref[...] = jnp.zeros_like(fstate_ref)

    @pl.when(jnp.logical_and(j == 0, has_bwd))
    def _():
        col = lax.broadcasted_iota(jnp.int32, (c, c), 1).astype(F32)
        zetab_ref[...] = jnp.exp(col * lg_b).astype(BF16)
        bstate_ref[...] = jnp.zeros_like(bstate_ref)

    def row_scaled(a, scale_ref):
        scale = scale_ref[...]
        return jnp.concatenate(
            [a[:, n * LANES:(n + 1) * LANES] * scale
             for n in range(a.shape[1] // LANES)], axis=1)

    def decayed_kv(kt_ref, v_ref, zeta_ref, ci):
        rows = slice(ci * c, (ci + 1) * c)
        return jnp.dot(kt_ref[:, rows] * zeta_ref[...], v_ref[rows, :V_DIM],
                       preferred_element_type=F32)

    def bwd_sweep():
        blk = n_blocks - 1 - j
        slot = (p % 2) * n_chunks
        chunk_decay = jnp.exp(jnp.zeros((1, V_DIM), F32) + c * lg_b)
        state = bstate_ref[...]
        for ci in reversed(range(cpb)):
            rall_ref[slot + blk * cpb + ci] = state.astype(BF16)
            state = chunk_decay * state + decayed_kv(ktb_ref, vb_ref, zetab_ref, ci)
        bstate_ref[...] = state

    def fwd_sweep():
        slot = ((p - 1) % 2) * n_chunks
        chunk_decay = jnp.exp(jnp.zeros((1, V_DIM), F32) + c * lg_f)

        def decayed_scores(ci):
            rows = slice(ci * c, (ci + 1) * c)
            scores = jnp.dot(vq_ref[rows, V_DIM:], ktf_ref[:, rows],
                             preferred_element_type=F32)
            return (scores * decay_ref[...]).astype(BF16)

        state = fstate_ref[...]
        pmat_next = decayed_scores(0)
        kv_next = decayed_kv(ktf_ref, vq_ref, zetaf_ref, 0)
        for ci in range(cpb):
            rows = slice(ci * c, (ci + 1) * c)
            pmat, kv = pmat_next, kv_next
            if ci + 1 < cpb:
                pmat_next = decayed_scores(ci + 1)
                kv_next = decayed_kv(ktf_ref, vq_ref, zetaf_ref, ci + 1)
            q = vq_ref[rows, V_DIM:]
            lhs = jnp.concatenate([pmat, row_scaled(q, xib_ref), row_scaled(q, xif_ref)],
                                  axis=1)
            rhs = jnp.concatenate([vq_ref[rows, :V_DIM], rall_ref[slot + j * cpb + ci],
                                   state.astype(BF16)], axis=0)
            ret = jnp.dot(lhs, rhs, preferred_element_type=F32)
            mu = jnp.mean(ret, axis=-1, keepdims=True)
            var = jnp.mean(jnp.square(ret - mu), axis=-1, keepdims=True)
            rstd = lax.rsqrt(var + EPS)
            z_ref[rows, :] = (ret * rstd - mu * rstd).astype(BF16)
            state = chunk_decay * state + kv
        fstate_ref[...] = state

    @pl.when(jnp.logical_not(has_fwd))
    def _():
        bwd_sweep()

    @pl.when(jnp.logical_and(has_fwd, has_bwd))
    def _():
        fwd_sweep()
        bwd_sweep()

    @pl.when(jnp.logical_not(has_bwd))
    def _():
        fwd_sweep()


def _retention(lg, vq, kt, batch, seq, block_tokens):
    c = RET_CHUNK
    nb = seq // block_tokens
    cpb = block_tokens // c
    nc = seq // c
    t = batch * seq
    n_pairs = batch * RET_HEADS

    def fwd_idx(p, j):
        pair = jnp.maximum(p - 1, 0)
        return pair // RET_HEADS, pair % RET_HEADS, jnp.where(p == 0, 0, j)

    def bwd_idx(p, j):
        pair = jnp.minimum(p, n_pairs - 1)
        return pair // RET_HEADS, pair % RET_HEADS, jnp.where(p == n_pairs, 0, nb - 1 - j)

    def head_rows(idx_fn):
        def index_map(p, j):
            b, h, blk = idx_fn(p, j)
            return h, b * nb + blk, 0
        return index_map

    def head_major(idx_fn):
        def index_map(p, j):
            b, h, blk = idx_fn(p, j)
            return h, b * nb + blk
        return index_map

    return pl.pallas_call(
        functools.partial(_retention_kernel, n_pairs=n_pairs, n_blocks=nb,
                          chunks_per_block=cpb),
        grid=(n_pairs + 1, nb),
        in_specs=[
            pl.BlockSpec(memory_space=pltpu.SMEM),
            pl.BlockSpec((None, block_tokens, VQ_DIM), head_rows(fwd_idx)),
            pl.BlockSpec((QK_DIM, block_tokens), head_major(fwd_idx)),
            pl.BlockSpec((QK_DIM, block_tokens), head_major(bwd_idx)),
            pl.BlockSpec((None, block_tokens, V_DIM), head_rows(bwd_idx)),
        ],
        out_specs=pl.BlockSpec((None, block_tokens, V_DIM), head_rows(fwd_idx)),
        out_shape=jax.ShapeDtypeStruct((RET_HEADS, t, V_DIM), BF16),
        scratch_shapes=[
            pltpu.VMEM((c, c), F32),
            pltpu.VMEM((c, LANES), BF16),
            pltpu.VMEM((c, LANES), BF16),
            pltpu.VMEM((QK_DIM, c), BF16),
            pltpu.VMEM((QK_DIM, c), BF16),
            pltpu.VMEM((QK_DIM, V_DIM), F32),
            pltpu.VMEM((QK_DIM, V_DIM), F32),
            pltpu.VMEM((2 * nc, QK_DIM, V_DIM), BF16),
        ],
        compiler_params=pltpu.CompilerParams(
            dimension_semantics=("arbitrary", "arbitrary"),
            vmem_limit_bytes=VMEM_LIMIT),
        name="retention",
    )(lg, vq, kt, kt, vq)


def _merge_kernel(x_ref, ugb_ref, up_ref, un_ref, gates_ref, z_ref, gsw_ref,
                  wconv_ref, wa_ref, wr_ref, wo_ref, o_ref, *, tiles_per_seq):
    i = pl.program_id(0)
    tm = x_ref.shape[0]
    u = ugb_ref[:, :SEC].astype(F32)
    pos = i % tiles_per_seq
    last = BF16_SUBLANES - 1
    u_before = up_ref[last:last + 1, :].astype(F32) * (pos != 0).astype(F32)
    u_after = un_ref[0:1, :].astype(F32) * (pos != tiles_per_seq - 1).astype(F32)
    row = lax.broadcasted_iota(jnp.int32, u.shape, 0)
    u_prev = jnp.where(row == 0, u_before, pltpu.roll(u, 1, axis=0))
    u_next = jnp.where(row == tm - 1, u_after, pltpu.roll(u, tm - 1, axis=0))
    conv = u_prev * wconv_ref[0:1, :] + u * wconv_ref[1:2, :] + u_next * wconv_ref[2:3, :]
    a_in = (ugb_ref[:, SEC:].astype(F32) * conv).astype(BF16)

    def branch_outputs(rows):
        z = jnp.concatenate([z_ref[hh, rows, :] for hh in range(RET_HEADS)], axis=1)
        y_r = jnp.dot(z * gsw_ref[rows, :], wr_ref[...], preferred_element_type=F32)
        y_a = jnp.dot(a_in[rows, :], wa_ref[...], preferred_element_type=F32)
        return y_a, y_r

    blocks = [slice(r, r + SUBBLOCK_ROWS) for r in range(0, tm, SUBBLOCK_ROWS)]
    y_next = branch_outputs(blocks[0])
    for s, rows in enumerate(blocks):
        y_a, y_r = y_next
        if s + 1 < len(blocks):
            y_next = branch_outputs(blocks[s + 1])
        merged = (gates_ref[rows, :SEC].astype(F32) * y_a
                  + gates_ref[rows, SEC:].astype(F32) * y_r)
        o_ref[rows, :] = x_ref[rows, :] + jnp.dot(merged.astype(BF16), wo_ref[...],
                                                  preferred_element_type=F32)


def _merge(x2, proj, z, w_conv, w_a, w_r, w_o, seq, tm):
    t, d = x2.shape
    hb = tm // BF16_SUBLANES
    n_halo_blocks = t // BF16_SUBLANES
    halo = (BF16_SUBLANES, SEC)
    return pl.pallas_call(
        functools.partial(_merge_kernel, tiles_per_seq=seq // tm),
        grid=(t // tm,),
        in_specs=[
            pl.BlockSpec((tm, d), lambda i: (i, 0)),
            pl.BlockSpec((tm, 2 * SEC), lambda i: (i, SEC_U // 2)),
            pl.BlockSpec(halo, lambda i: (jnp.maximum(i * hb - 1, 0), SEC_U)),
            pl.BlockSpec(halo, lambda i: (jnp.minimum((i + 1) * hb, n_halo_blocks - 1), SEC_U)),
            pl.BlockSpec((tm, 2 * SEC), lambda i: (i, SEC_GA // 2)),
            pl.BlockSpec((RET_HEADS, tm, V_DIM), lambda i: (0, i, 0)),
            pl.BlockSpec((tm, RET_V), lambda i: (i, (SEC_GSW * SEC) // RET_V)),
            _const_spec(w_conv.shape),
            _const_spec(w_a.shape),
            _const_spec(w_r.shape),
            _const_spec(w_o.shape),
        ],
        out_specs=pl.BlockSpec((tm, d), lambda i: (i, 0)),
        out_shape=jax.ShapeDtypeStruct((t, d), F32),
        compiler_params=pltpu.CompilerParams(
            dimension_semantics=("arbitrary",),
            vmem_limit_bytes=VMEM_LIMIT),
        name="merge",
    )(x2, proj, proj, proj, proj, z, proj, w_conv, w_a, w_r, w_o)


def _ffn_kernel(x_ref, gffn_ref, wg_ref, wu_ref, wd_ref, gfin_ref, o_ref, *, final_norm):
    def gate_up(rows):
        h2 = (_rms_scale(x_ref[rows, :]) * gffn_ref[...]).astype(BF16)
        return (jnp.dot(h2, wg_ref[...], preferred_element_type=F32),
                jnp.dot(h2, wu_ref[...], preferred_element_type=F32))

    blocks = [slice(r, r + SUBBLOCK_ROWS) for r in range(0, x_ref.shape[0], SUBBLOCK_ROWS)]
    gu_next = gate_up(blocks[0])
    for s, rows in enumerate(blocks):
        gate, up = gu_next
        if s + 1 < len(blocks):
            gu_next = gate_up(blocks[s + 1])
        act = (gate * _sigmoid(gate) * up).astype(BF16)
        y = x_ref[rows, :] + jnp.dot(act, wd_ref[...], preferred_element_type=F32)
        o_ref[rows, :] = _rms_scale(y) * gfin_ref[...] if final_norm else y


def _ffn(x1, g_ffn, w_gate, w_up, w_down, g_final, final_norm, tm):
    t, d = x1.shape
    return pl.pallas_call(
        functools.partial(_ffn_kernel, final_norm=final_norm),
        grid=(t // tm,),
        in_specs=[
            pl.BlockSpec((tm, d), lambda i: (i, 0)),
            _const_spec(g_ffn.shape),
            _const_spec(w_gate.shape),
            _const_spec(w_up.shape),
            _const_spec(w_down.shape),
            _const_spec(g_final.shape),
        ],
        out_specs=pl.BlockSpec((tm, d), lambda i: (i, 0)),
        out_shape=jax.ShapeDtypeStruct((t, d), F32),
        compiler_params=pltpu.CompilerParams(
            dimension_semantics=("arbitrary",),
            vmem_limit_bytes=VMEM_LIMIT),
        name="ffn",
    )(x1, g_ffn, w_gate, w_up, w_down, g_final)


def _rotary_tables(seq, tm):
    freqs = ROPE_BASE ** (-jnp.arange(0, QK_DIM, 2, dtype=F32) / QK_DIM)
    base = (jnp.arange(seq // tm, dtype=F32) * tm)[:, None] * freqs[None, :]
    off = jnp.arange(tm, dtype=F32)[:, None] * freqs[None, :]
    return jnp.cos(base), jnp.sin(base), jnp.cos(off), jnp.sin(off)


def kernel(x, g_mix, w_in, w_conv, dec_f, dec_b, g_ret, w_a_out, w_r_out, w_o,
           g_ffn, w_ff_gate, w_ff_up, w_ff_down, g_final):
    batch, seq, d = x.shape
    depth = w_in.shape[0]
    assert d == D_MODEL and seq % RET_BLOCK_TOKENS == 0 and seq % MERGE_ROWS == 0
    x2 = x.reshape(batch * seq, d)
    rot = _rotary_tables(seq, INPROJ_ROWS)
    for l in range(depth):
        lg = jnp.stack([jax.nn.log_sigmoid(dec_f[l].astype(F32)),
                        jax.nn.log_sigmoid(dec_b[l].astype(F32))])
        side = [w_a_out[l], w_r_out[l], w_o[l], w_ff_gate[l], w_ff_up[l], w_ff_down[l]]
        proj, kt, vq, (w_a, w_r, w_ob, w_gate, w_up, w_down) = _inproj(
            x2, g_mix[l][None, :], w_in[l], rot,
            g_ret[l][None, :].astype(F32), side, tm=INPROJ_ROWS)
        z = _retention(lg, vq, kt, batch, seq, block_tokens=RET_BLOCK_TOKENS)
        x1 = _merge(x2, proj, z, w_conv[l], w_a, w_r, w_ob, seq, tm=MERGE_ROWS)
        x2 = _ffn(x1, g_ffn[l][None, :], w_gate, w_up, w_down,
                  g_final[None, :], final_norm=(l == depth - 1), tm=FFN_ROWS)
    return x2.reshape(batch, seq, d)
```

```python
import functools

import jax
import jax.numpy as jnp
from jax import lax
from jax.experimental import pallas as pl
from jax.experimental.pallas import tpu as pltpu

F32 = jnp.float32
BF16 = jnp.bfloat16

D_MODEL = 1024
RET_HEADS = 4
QK_DIM = D_MODEL // RET_HEADS
V_DIM = 2 * QK_DIM
RET_V = RET_HEADS * V_DIM
ROPE_BASE = 10000.0
EPS = 1e-6

SEC = 1024
W_XC, W_GB, W_GC, W_Q, W_K, W_V, W_GSW, W_GA, W_GR, N_W_SEC = 0, 1, 2, 3, 4, 5, 7, 9, 10, 11
SEC_U, SEC_GB, SEC_GSW, SEC_GA, SEC_GR, N_OUT_SEC = 0, 1, 2, 4, 5, 6
VQ_DIM = V_DIM + QK_DIM

RET_CHUNK = 256
SIDE_CAST_STEPS = 16
SUBBLOCK_ROWS = 256
INPROJ_ROWS = 256
RET_BLOCK_TOKENS = 4096
MERGE_ROWS = 512
FFN_ROWS = 512
LANES = 128
BF16_SUBLANES = 16

VMEM_LIMIT = 56 * 1024 * 1024


def _sigmoid(x):
    return 1.0 / (1.0 + jnp.exp(-x))


def _rms_scale(x):
    return x * lax.rsqrt(jnp.mean(x * x, axis=-1, keepdims=True) + EPS)


def _const_spec(shape):
    return pl.BlockSpec(shape, lambda *_: (0,) * len(shape), pipeline_mode=pl.Buffered(1))


def _inproj_kernel(x_ref, g_ref, w_ref, cos_base_ref, sin_base_ref, cos_off_ref, sin_off_ref,
                   gret_ref, *rest, n_pos_blocks, n_side):
    side_in = rest[:n_side]
    o_ref, kt_ref, vq_ref = rest[n_side:n_side + 3]
    side_out = rest[n_side + 3:2 * n_side + 3]
    w_bf16_ref, h_ref = rest[2 * n_side + 3:]
    step = pl.program_id(0) - N_W_SEC

    def normalise_next_tile(slot):
        h_ref[slot] = (_rms_scale(x_ref[...]) * g_ref[...]).astype(BF16)

    @pl.when(step < 0)
    def _():
        w_bf16_ref[pl.program_id(0)] = w_ref[...].astype(BF16)

    @pl.when(step == -1)
    def _():
        normalise_next_tile(0)

    @pl.when(step >= 0)
    def _():
        h = h_ref[step % 2]
        half = QK_DIM // 2
        pos_block = pl.ds(step % n_pos_blocks, 1)
        cos_b = cos_base_ref[pos_block, :]
        sin_b = sin_base_ref[pos_block, :]
        cos = cos_b * cos_off_ref[...] - sin_b * sin_off_ref[...]
        sin = sin_b * cos_off_ref[...] + cos_b * sin_off_ref[...]

        def project(first_sec, n_sec):
            return jnp.concatenate(
                [jnp.dot(h, w_bf16_ref[sec], preferred_element_type=F32)
                 for sec in range(first_sec, first_sec + n_sec)], axis=1)

        def out_cols(sec, n_sec=1):
            return slice(sec * SEC, (sec + n_sec) * SEC)

        qk = project(W_Q, 2)
        for sec in range(2):
            for hh in range(RET_HEADS):
                lo = sec * SEC + hh * QK_DIM
                t1 = qk[:, lo:lo + half]
                t2 = qk[:, lo + half:lo + QK_DIM]
                r1 = t1 * cos - t2 * sin
                r2 = t1 * sin + t2 * cos
                if sec == 0:
                    scale = QK_DIM ** -0.5
                    vq_ref[hh, :, V_DIM:V_DIM + half] = (r1 * scale).astype(BF16)
                    vq_ref[hh, :, V_DIM + half:VQ_DIM] = (r2 * scale).astype(BF16)
                else:
                    out = hh * QK_DIM
                    kt_ref[out:out + half, :] = r1.T.astype(BF16)
                    kt_ref[out + half:out + QK_DIM, :] = r2.T.astype(BF16)

        gsw = project(W_GSW, 2)
        o_ref[:, out_cols(SEC_GSW, 2)] = (gsw * _sigmoid(gsw) * gret_ref[...]).astype(BF16)

        gates = project(W_GA, 2)
        o_ref[:, out_cols(SEC_GA, 2)] = _sigmoid(gates).astype(BF16)

        conv_in = project(W_XC, 3)
        o_ref[:, out_cols(SEC_U)] = (conv_in[:, out_cols(W_GC)] * conv_in[:, out_cols(W_XC)]
                                     ).astype(BF16)
        o_ref[:, out_cols(SEC_GB)] = conv_in[:, out_cols(W_GB)].astype(BF16)

        values = project(W_V, 2)
        for hh in range(RET_HEADS):
            vq_ref[hh, :, :V_DIM] = values[:, hh * V_DIM:(hh + 1) * V_DIM].astype(BF16)

        normalise_next_tile((step + 1) % 2)

    @pl.when(jnp.logical_and(step >= 0, step < SIDE_CAST_STEPS))
    def _():
        for src, dst in zip(side_in, side_out):
            dst[...] = src[...].astype(BF16)


def _inproj(x2, g_mix, w_in, rot, g_ret, side_weights, tm):
    t, d = x2.shape
    cos_base, sin_base, cos_off, sin_off = rot
    assert cos_off.shape[0] == tm and w_in.shape[1] == N_W_SEC * SEC
    n_out = N_OUT_SEC * SEC
    n_steps = t // tm
    assert n_steps >= SIDE_CAST_STEPS

    def tile(i):
        return jnp.maximum(i - N_W_SEC, 0)

    def slab_spec(w):
        rows = w.shape[0] // SIDE_CAST_STEPS
        assert rows * SIDE_CAST_STEPS == w.shape[0] and rows % BF16_SUBLANES == 0
        return pl.BlockSpec((rows, w.shape[1]),
                            lambda i: (jnp.minimum(tile(i), SIDE_CAST_STEPS - 1), 0))

    side_specs = [slab_spec(w) for w in side_weights]
    outs = pl.pallas_call(
        functools.partial(_inproj_kernel, n_pos_blocks=cos_base.shape[0],
                          n_side=len(side_weights)),
        grid=(N_W_SEC + n_steps,),
        in_specs=[
            pl.BlockSpec((tm, d), lambda i: (jnp.minimum(tile(i + 1), n_steps - 1), 0)),
            _const_spec(g_mix.shape),
            pl.BlockSpec((d, SEC), lambda i: (0, jnp.minimum(i, N_W_SEC - 1))),
            _const_spec(cos_base.shape),
            _const_spec(sin_base.shape),
            _const_spec(cos_off.shape),
            _const_spec(sin_off.shape),
            _const_spec(g_ret.shape),
        ] + side_specs,
        out_specs=[
            pl.BlockSpec((tm, n_out), lambda i: (tile(i), 0)),
            pl.BlockSpec((RET_HEADS * QK_DIM, tm), lambda i: (0, tile(i))),
            pl.BlockSpec((RET_HEADS, tm, VQ_DIM), lambda i: (0, tile(i), 0)),
        ] + side_specs,
        out_shape=[
            jax.ShapeDtypeStruct((t, n_out), BF16),
            jax.ShapeDtypeStruct((RET_HEADS * QK_DIM, t), BF16),
            jax.ShapeDtypeStruct((RET_HEADS, t, VQ_DIM), BF16),
        ] + [jax.ShapeDtypeStruct(w.shape, BF16) for w in side_weights],
        scratch_shapes=[pltpu.VMEM((N_W_SEC, d, SEC), BF16),
                        pltpu.VMEM((2, tm, d), BF16)],
        compiler_params=pltpu.CompilerParams(
            dimension_semantics=("arbitrary",),
            vmem_limit_bytes=VMEM_LIMIT),
        name="inproj",
    )(x2, g_mix, w_in, cos_base, sin_base, cos_off, sin_off, g_ret, *side_weights)
    return outs[0], outs[1], outs[2], outs[3:]


def _retention_kernel(lg_ref, vq_ref, ktf_ref, ktb_ref, vb_ref, z_ref,
                      decay_ref, xif_ref, xib_ref, zetaf_ref, zetab_ref,
                      fstate_ref, bstate_ref, rall_ref,
                      *, n_pairs, n_blocks, chunks_per_block):
    c = RET_CHUNK
    cpb = chunks_per_block
    n_chunks = n_blocks * cpb
    p = pl.program_id(0)
    j = pl.program_id(1)
    head_f = jnp.maximum(p - 1, 0) % RET_HEADS
    head_b = jnp.minimum(p, n_pairs - 1) % RET_HEADS
    lg_f = lg_ref[0, head_f]
    lg_fb = lg_ref[1, head_f]
    lg_b = lg_ref[1, head_b]
    has_fwd = p >= 1
    has_bwd = p < n_pairs

    @pl.when(jnp.logical_and(j == 0, has_fwd))
    def _():
        row = lax.broadcasted_iota(jnp.int32, (c, c), 0).astype(F32)
        col = lax.broadcasted_iota(jnp.int32, (c, c), 1).astype(F32)
        diff = row - col
        decay_ref[...] = jnp.where(diff >= 0.0,
                                   jnp.exp(jnp.maximum(diff, 0.0) * lg_f),
                                   jnp.exp(jnp.maximum(-diff, 0.0) * lg_fb))
        zetaf_ref[...] = jnp.exp((c - 1.0 - col) * lg_f).astype(BF16)
        idx = lax.broadcasted_iota(jnp.int32, (c, LANES), 0).astype(F32)
        xif_ref[...] = jnp.exp((idx + 1.0) * lg_f).astype(BF16)
        xib_ref[...] = jnp.exp((c - idx) * lg_fb).astype(BF16)
        fstate_ref[...] = jnp.zeros_like(fstate_ref)

    @pl.when(jnp.logical_and(j == 0, has_bwd))
    def _():
        col = lax.broadcasted_iota(jnp.int32, (c, c), 1).astype(F32)
        zetab_ref[...] = jnp.exp(col * lg_b).astype(BF16)
        bstate_ref[...] = jnp.zeros_like(bstate_ref)

    def row_scaled(a, scale_ref):
        scale = scale_ref[...]
        return jnp.concatenate(
            [a[:, n * LANES:(n + 1) * LANES] * scale
             for n in range(a.shape[1] // LANES)], axis=1)

    def decayed_kv(kt_ref, v_ref, zeta_ref, ci):
        rows = slice(ci * c, (ci + 1) * c)
        return jnp.dot(kt_ref[:, rows] * zeta_ref[...], v_ref[rows, :V_DIM],
                       preferred_element_type=F32)

    def bwd_sweep():
        blk = n_blocks - 1 - j
        slot = (p % 2) * n_chunks
        chunk_decay = jnp.exp(jnp.zeros((1, V_DIM), F32) + c * lg_b)
        state = bstate_ref[...]
        for ci in reversed(range(cpb)):
            rall_ref[slot + blk * cpb + ci] = state.astype(BF16)
            state = chunk_decay * state + decayed_kv(ktb_ref, vb_ref, zetab_ref, ci)
        bstate_ref[...] = state

    def fwd_sweep():
        slot = ((p - 1) % 2) * n_chunks
        chunk_decay = jnp.exp(jnp.zeros((1, V_DIM), F32) + c * lg_f)

        def decayed_scores(ci):
            rows = slice(ci * c, (ci + 1) * c)
            scores = jnp.dot(vq_ref[rows, V_DIM:], ktf_ref[:, rows],
                             preferred_element_type=F32)
            return (scores * decay_ref[...]).astype(BF16)

        state = fstate_ref[...]
        pmat_next = decayed_scores(0)
        kv_next = decayed_kv(ktf_ref, vq_ref, zetaf_ref, 0)
        for ci in range(cpb):
            rows = slice(ci * c, (ci + 1) * c)
            pmat, kv = pmat_next, kv_next
            if ci + 1 < cpb:
                pmat_next = decayed_scores(ci + 1)
                kv_next = decayed_kv(ktf_ref, vq_ref, zetaf_ref, ci + 1)
            q = vq_ref[rows, V_DIM:]
            lhs = jnp.concatenate([pmat, row_scaled(q, xib_ref), row_scaled(q, xif_ref)],
                                  axis=1)
            rhs = jnp.concatenate([vq_ref[rows, :V_DIM], rall_ref[slot + j * cpb + ci],
                                   state.astype(BF16)], axis=0)
            ret = jnp.dot(lhs, rhs, preferred_element_type=F32)
            mu = jnp.mean(ret, axis=-1, keepdims=True)
            var = jnp.mean(jnp.square(ret - mu), axis=-1, keepdims=True)
            rstd = lax.rsqrt(var + EPS)
            z_ref[rows, :] = (ret * rstd - mu * rstd).astype(BF16)
            state = chunk_decay * state + kv
        fstate_ref[...] = state

    @pl.when(jnp.logical_not(has_fwd))
    def _():
        bwd_sweep()

    @pl.when(jnp.logical_and(has_fwd, has_bwd))
    def _():
        fwd_sweep()
        bwd_sweep()

    @pl.when(jnp.logical_not(has_bwd))
    def _():
        fwd_sweep()


def _retention(lg, vq, kt, batch, seq, block_tokens):
    c = RET_CHUNK
    nb = seq // block_tokens
    cpb = block_tokens // c
    nc = seq // c
    t = batch * seq
    n_pairs = batch * RET_HEADS

    def fwd_idx(p, j):
        pair = jnp.maximum(p - 1, 0)
        return pair // RET_HEADS, pair % RET_HEADS, jnp.where(p == 0, 0, j)

    def bwd_idx(p, j):
        pair = jnp.minimum(p, n_pairs - 1)
        return pair // RET_HEADS, pair % RET_HEADS, jnp.where(p == n_pairs, 0, nb - 1 - j)

    def head_rows(idx_fn):
        def index_map(p, j):
            b, h, blk = idx_fn(p, j)
            return h, b * nb + blk, 0
        return index_map

    def head_major(idx_fn):
        def index_map(p, j):
            b, h, blk = idx_fn(p, j)
            return h, b * nb + blk
        return index_map

    return pl.pallas_call(
        functools.partial(_retention_kernel, n_pairs=n_pairs, n_blocks=nb,
                          chunks_per_block=cpb),
        grid=(n_pairs + 1, nb),
        in_specs=[
            pl.BlockSpec(memory_space=pltpu.SMEM),
            pl.BlockSpec((None, block_tokens, VQ_DIM), head_rows(fwd_idx)),
            pl.BlockSpec((QK_DIM, block_tokens), head_major(fwd_idx)),
            pl.BlockSpec((QK_DIM, block_tokens), head_major(bwd_idx)),
            pl.BlockSpec((None, block_tokens, V_DIM), head_rows(bwd_idx)),
        ],
        out_specs=pl.BlockSpec((None, block_tokens, V_DIM), head_rows(fwd_idx)),
        out_shape=jax.ShapeDtypeStruct((RET_HEADS, t, V_DIM), BF16),
        scratch_shapes=[
            pltpu.VMEM((c, c), F32),
            pltpu.VMEM((c, LANES), BF16),
            pltpu.VMEM((c, LANES), BF16),
            pltpu.VMEM((QK_DIM, c), BF16),
            pltpu.VMEM((QK_DIM, c), BF16),
            pltpu.VMEM((QK_DIM, V_DIM), F32),
            pltpu.VMEM((QK_DIM, V_DIM), F32),
            pltpu.VMEM((2 * nc, QK_DIM, V_DIM), BF16),
        ],
        compiler_params=pltpu.CompilerParams(
            dimension_semantics=("arbitrary", "arbitrary"),
            vmem_limit_bytes=VMEM_LIMIT),
        name="retention",
    )(lg, vq, kt, kt, vq)


def _merge_kernel(x_ref, u_ref, gb_ref, up_ref, un_ref, ga_ref, gr_ref, z_ref, gsw_ref,
                  wconv_ref, wa_ref, wr_ref, wo_ref, o_ref, *, tiles_per_seq):
    i = pl.program_id(0)
    tm = x_ref.shape[0]
    u = u_ref[...].astype(F32)
    pos = i % tiles_per_seq
    last = BF16_SUBLANES - 1
    u_before = up_ref[last:last + 1, :].astype(F32) * (pos != 0).astype(F32)
    u_after = un_ref[0:1, :].astype(F32) * (pos != tiles_per_seq - 1).astype(F32)
    row = lax.broadcasted_iota(jnp.int32, u.shape, 0)
    u_prev = jnp.where(row == 0, u_before, pltpu.roll(u, 1, axis=0))
    u_next = jnp.where(row == tm - 1, u_after, pltpu.roll(u, tm - 1, axis=0))
    conv = u_prev * wconv_ref[0:1, :] + u * wconv_ref[1:2, :] + u_next * wconv_ref[2:3, :]
    a_in = (gb_ref[...].astype(F32) * conv).astype(BF16)

    def branch_outputs(rows):
        z = jnp.concatenate([z_ref[hh, rows, :] for hh in range(RET_HEADS)], axis=1)
        y_r = jnp.dot(z * gsw_ref[rows, :], wr_ref[...], preferred_element_type=F32)
        y_a = jnp.dot(a_in[rows, :], wa_ref[...], preferred_element_type=F32)
        return y_a, y_r

    blocks = [slice(r, r + SUBBLOCK_ROWS) for r in range(0, tm, SUBBLOCK_ROWS)]
    y_next = branch_outputs(blocks[0])
    for s, rows in enumerate(blocks):
        y_a, y_r = y_next
        if s + 1 < len(blocks):
            y_next = branch_outputs(blocks[s + 1])
        merged = ga_ref[rows, :].astype(F32) * y_a + gr_ref[rows, :].astype(F32) * y_r
        o_ref[rows, :] = x_ref[rows, :] + jnp.dot(merged.astype(BF16), wo_ref[...],
                                                  preferred_element_type=F32)


def _merge(x2, proj, z, w_conv, w_a, w_r, w_o, seq, tm):
    t, d = x2.shape
    hb = tm // BF16_SUBLANES
    n_halo_blocks = t // BF16_SUBLANES
    halo = (BF16_SUBLANES, SEC)
    return pl.pallas_call(
        functools.partial(_merge_kernel, tiles_per_seq=seq // tm),
        grid=(t // tm,),
        in_specs=[
            pl.BlockSpec((tm, d), lambda i: (i, 0)),
            pl.BlockSpec((tm, SEC), lambda i: (i, SEC_U)),
            pl.BlockSpec((tm, SEC), lambda i: (i, SEC_GB)),
            pl.BlockSpec(halo, lambda i: (jnp.maximum(i * hb - 1, 0), SEC_U)),
            pl.BlockSpec(halo, lambda i: (jnp.minimum((i + 1) * hb, n_halo_blocks - 1), SEC_U)),
            pl.BlockSpec((tm, SEC), lambda i: (i, SEC_GA)),
            pl.BlockSpec((tm, SEC), lambda i: (i, SEC_GR)),
            pl.BlockSpec((RET_HEADS, tm, V_DIM), lambda i: (0, i, 0)),
            pl.BlockSpec((tm, RET_V), lambda i: (i, (SEC_GSW * SEC) // RET_V)),
            _const_spec(w_conv.shape),
            _const_spec(w_a.shape),
            _const_spec(w_r.shape),
            _const_spec(w_o.shape),
        ],
        out_specs=pl.BlockSpec((tm, d), lambda i: (i, 0)),
        out_shape=jax.ShapeDtypeStruct((t, d), F32),
        compiler_params=pltpu.CompilerParams(
            dimension_semantics=("arbitrary",),
            vmem_limit_bytes=VMEM_LIMIT),
        name="merge",
    )(x2, proj, proj, proj, proj, proj, proj, z, proj, w_conv, w_a, w_r, w_o)


def _ffn_kernel(x_ref, gffn_ref, wg_ref, wu_ref, wd_ref, gfin_ref, o_ref, *, final_norm):
    def gate_up(rows):
        h2 = (_rms_scale(x_ref[rows, :]) * gffn_ref[...]).astype(BF16)
        return (jnp.dot(h2, wg_ref[...], preferred_element_type=F32),
                jnp.dot(h2, wu_ref[...], preferred_element_type=F32))

    blocks = [slice(r, r + SUBBLOCK_ROWS) for r in range(0, x_ref.shape[0], SUBBLOCK_ROWS)]
    gu_next = gate_up(blocks[0])
    for s, rows in enumerate(blocks):
        gate, up = gu_next
        if s + 1 < len(blocks):
            gu_next = gate_up(blocks[s + 1])
        act = (gate * _sigmoid(gate) * up).astype(BF16)
        y = x_ref[rows, :] + jnp.dot(act, wd_ref[...], preferred_element_type=F32)
        o_ref[rows, :] = _rms_scale(y) * gfin_ref[...] if final_norm else y


def _ffn(x1, g_ffn, w_gate, w_up, w_down, g_final, final_norm, tm):
    t, d = x1.shape
    return pl.pallas_call(
        functools.partial(_ffn_kernel, final_norm=final_norm),
        grid=(t // tm,),
        in_specs=[
            pl.BlockSpec((tm, d), lambda i: (i, 0)),
            _const_spec(g_ffn.shape),
            _const_spec(w_gate.shape),
            _const_spec(w_up.shape),
            _const_spec(w_down.shape),
            _const_spec(g_final.shape),
        ],
        out_specs=pl.BlockSpec((tm, d), lambda i: (i, 0)),
        out_shape=jax.ShapeDtypeStruct((t, d), F32),
        compiler_params=pltpu.CompilerParams(
            dimension_semantics=("arbitrary",),
            vmem_limit_bytes=VMEM_LIMIT),
        name="ffn",
    )(x1, g_ffn, w_gate, w_up, w_down, g_final)


def _rotary_tables(seq, tm):
    freqs = ROPE_BASE ** (-jnp.arange(0, QK_DIM, 2, dtype=F32) / QK_DIM)
    base = (jnp.arange(seq // tm, dtype=F32) * tm)[:, None] * freqs[None, :]
    off = jnp.arange(tm, dtype=F32)[:, None] * freqs[None, :]
    return jnp.cos(base), jnp.sin(base), jnp.cos(off), jnp.sin(off)


def kernel(x, g_mix, w_in, w_conv, dec_f, dec_b, g_ret, w_a_out, w_r_out, w_o,
           g_ffn, w_ff_gate, w_ff_up, w_ff_down, g_final):
    batch, seq, d = x.shape
    depth = w_in.shape[0]
    assert d == D_MODEL and seq % RET_BLOCK_TOKENS == 0 and seq % MERGE_ROWS == 0
    x2 = x.reshape(batch * seq, d)
    rot = _rotary_tables(seq, INPROJ_ROWS)
    for l in range(depth):
        lg = jnp.stack([jax.nn.log_sigmoid(dec_f[l].astype(F32)),
                        jax.nn.log_sigmoid(dec_b[l].astype(F32))])
        side = [w_a_out[l], w_r_out[l], w_o[l], w_ff_gate[l], w_ff_up[l], w_ff_down[l]]
        proj, kt, vq, (w_a, w_r, w_ob, w_gate, w_up, w_down) = _inproj(
            x2, g_mix[l][None, :], w_in[l], rot,
            g_ret[l][None, :].astype(F32), side, tm=INPROJ_ROWS)
        z = _retention(lg, vq, kt, batch, seq, block_tokens=RET_BLOCK_TOKENS)
        x1 = _merge(x2, proj, z, w_conv[l], w_a, w_r, w_ob, seq, tm=MERGE_ROWS)
        x2 = _ffn(x1, g_ffn[l][None, :], w_gate, w_up, w_down,
                  g_final[None, :], final_norm=(l == depth - 1), tm=FFN_ROWS)
    return x2.reshape(batch, seq, d)
```

```python
import functools

import jax
import jax.numpy as jnp
from jax import lax
from jax.experimental import pallas as pl
from jax.experimental.pallas import tpu as pltpu

F32 = jnp.float32
BF16 = jnp.bfloat16

D_MODEL = 1024
RET_HEADS = 4
QK_DIM = D_MODEL // RET_HEADS
V_DIM = 2 * QK_DIM
RET_V = RET_HEADS * V_DIM
ROPE_BASE = 10000.0
EPS = 1e-6

SEC = 1024
W_XC, W_GB, W_GC, W_Q, W_K, W_V, W_GSW, W_GA, W_GR, N_W_SEC = 0, 1, 2, 3, 4, 5, 7, 9, 10, 11
SEC_U, SEC_GB, SEC_GSW, SEC_GA, SEC_GR, N_OUT_SEC = 0, 1, 2, 4, 5, 6
VQ_DIM = V_DIM + QK_DIM

RET_CHUNK = 256
SIDE_CAST_STEPS = 16
SUBBLOCK_ROWS = 256
INPROJ_ROWS = 256
RET_BLOCK_TOKENS = 4096
MERGE_ROWS = 1024
FFN_ROWS = 512
LANES = 128
BF16_SUBLANES = 16

VMEM_LIMIT = 56 * 1024 * 1024


def _sigmoid(x):
    return 1.0 / (1.0 + jnp.exp(-x))


def _rms_scale(x):
    return x * lax.rsqrt(jnp.mean(x * x, axis=-1, keepdims=True) + EPS)


def _const_spec(shape):
    return pl.BlockSpec(shape, lambda *_: (0,) * len(shape), pipeline_mode=pl.Buffered(1))


def _inproj_kernel(x_ref, g_ref, w_ref, cos_base_ref, sin_base_ref, cos_off_ref, sin_off_ref,
                   gret_ref, *rest, n_pos_blocks, n_side):
    side_in = rest[:n_side]
    o_ref, kt_ref, vq_ref = rest[n_side:n_side + 3]
    side_out = rest[n_side + 3:2 * n_side + 3]
    w_bf16_ref = rest[2 * n_side + 3]
    step = pl.program_id(0) - N_W_SEC

    @pl.when(step < 0)
    def _():
        w_bf16_ref[pl.program_id(0)] = w_ref[...].astype(BF16)

    @pl.when(step >= 0)
    def _():
        h = (_rms_scale(x_ref[...]) * g_ref[...]).astype(BF16)
        half = QK_DIM // 2
        pos_block = pl.ds(step % n_pos_blocks, 1)
        cos_b = cos_base_ref[pos_block, :]
        sin_b = sin_base_ref[pos_block, :]
        cos = cos_b * cos_off_ref[...] - sin_b * sin_off_ref[...]
        sin = sin_b * cos_off_ref[...] + cos_b * sin_off_ref[...]

        def project(first_sec, n_sec):
            return jnp.concatenate(
                [jnp.dot(h, w_bf16_ref[sec], preferred_element_type=F32)
                 for sec in range(first_sec, first_sec + n_sec)], axis=1)

        def out_cols(sec, n_sec=1):
            return slice(sec * SEC, (sec + n_sec) * SEC)

        qk = project(W_Q, 2)
        for sec in range(2):
            for hh in range(RET_HEADS):
                lo = sec * SEC + hh * QK_DIM
                t1 = qk[:, lo:lo + half]
                t2 = qk[:, lo + half:lo + QK_DIM]
                r1 = t1 * cos - t2 * sin
                r2 = t1 * sin + t2 * cos
                if sec == 0:
                    scale = QK_DIM ** -0.5
                    vq_ref[hh, :, V_DIM:V_DIM + half] = (r1 * scale).astype(BF16)
                    vq_ref[hh, :, V_DIM + half:VQ_DIM] = (r2 * scale).astype(BF16)
                else:
                    out = hh * QK_DIM
                    kt_ref[out:out + half, :] = r1.T.astype(BF16)
                    kt_ref[out + half:out + QK_DIM, :] = r2.T.astype(BF16)

        gsw = project(W_GSW, 2)
        o_ref[:, out_cols(SEC_GSW, 2)] = (gsw * _sigmoid(gsw) * gret_ref[...]).astype(BF16)

        gates = project(W_GA, 2)
        o_ref[:, out_cols(SEC_GA, 2)] = _sigmoid(gates).astype(BF16)

        conv_in = project(W_XC, 3)
        o_ref[:, out_cols(SEC_U)] = (conv_in[:, out_cols(W_GC)] * conv_in[:, out_cols(W_XC)]
                                     ).astype(BF16)
        o_ref[:, out_cols(SEC_GB)] = conv_in[:, out_cols(W_GB)].astype(BF16)

        values = project(W_V, 2)
        for hh in range(RET_HEADS):
            vq_ref[hh, :, :V_DIM] = values[:, hh * V_DIM:(hh + 1) * V_DIM].astype(BF16)

    @pl.when(jnp.logical_and(step >= 0, step < SIDE_CAST_STEPS))
    def _():
        for src, dst in zip(side_in, side_out):
            dst[...] = src[...].astype(BF16)


def _inproj(x2, g_mix, w_in, rot, g_ret, side_weights, tm):
    t, d = x2.shape
    cos_base, sin_base, cos_off, sin_off = rot
    assert cos_off.shape[0] == tm and w_in.shape[1] == N_W_SEC * SEC
    n_out = N_OUT_SEC * SEC
    n_steps = t // tm
    assert n_steps >= SIDE_CAST_STEPS

    def tile(i):
        return jnp.maximum(i - N_W_SEC, 0)

    def slab_spec(w):
        rows = w.shape[0] // SIDE_CAST_STEPS
        assert rows * SIDE_CAST_STEPS == w.shape[0] and rows % BF16_SUBLANES == 0
        return pl.BlockSpec((rows, w.shape[1]),
                            lambda i: (jnp.minimum(tile(i), SIDE_CAST_STEPS - 1), 0))

    side_specs = [slab_spec(w) for w in side_weights]
    outs = pl.pallas_call(
        functools.partial(_inproj_kernel, n_pos_blocks=cos_base.shape[0],
                          n_side=len(side_weights)),
        grid=(N_W_SEC + n_steps,),
        in_specs=[
            pl.BlockSpec((tm, d), lambda i: (tile(i), 0)),
            _const_spec(g_mix.shape),
            pl.BlockSpec((d, SEC), lambda i: (0, jnp.minimum(i, N_W_SEC - 1))),
            _const_spec(cos_base.shape),
            _const_spec(sin_base.shape),
            _const_spec(cos_off.shape),
            _const_spec(sin_off.shape),
            _const_spec(g_ret.shape),
        ] + side_specs,
        out_specs=[
            pl.BlockSpec((tm, n_out), lambda i: (tile(i), 0)),
            pl.BlockSpec((RET_HEADS * QK_DIM, tm), lambda i: (0, tile(i))),
            pl.BlockSpec((RET_HEADS, tm, VQ_DIM), lambda i: (0, tile(i), 0)),
        ] + side_specs,
        out_shape=[
            jax.ShapeDtypeStruct((t, n_out), BF16),
            jax.ShapeDtypeStruct((RET_HEADS * QK_DIM, t), BF16),
            jax.ShapeDtypeStruct((RET_HEADS, t, VQ_DIM), BF16),
        ] + [jax.ShapeDtypeStruct(w.shape, BF16) for w in side_weights],
        scratch_shapes=[pltpu.VMEM((N_W_SEC, d, SEC), BF16)],
        compiler_params=pltpu.CompilerParams(
            dimension_semantics=("arbitrary",),
            vmem_limit_bytes=VMEM_LIMIT),
        name="inproj",
    )(x2, g_mix, w_in, cos_base, sin_base, cos_off, sin_off, g_ret, *side_weights)
    return outs[0], outs[1], outs[2], outs[3:]


def _retention_kernel(lg_ref, vq_ref, ktf_ref, ktb_ref, vb_ref, z_ref,
                      decay_ref, xif_ref, xib_ref, zetaf_ref, zetab_ref,
                      fstate_ref, bstate_ref, rall_ref,
                      *, n_pairs, n_blocks, chunks_per_block):
    c = RET_CHUNK
    cpb = chunks_per_block
    n_chunks = n_blocks * cpb
    p = pl.program_id(0)
    j = pl.program_id(1)
    head_f = jnp.maximum(p - 1, 0) % RET_HEADS
    head_b = jnp.minimum(p, n_pairs - 1) % RET_HEADS
    lg_f = lg_ref[0, head_f]
    lg_fb = lg_ref[1, head_f]
    lg_b = lg_ref[1, head_b]
    has_fwd = p >= 1
    has_bwd = p < n_pairs

    @pl.when(jnp.logical_and(j == 0, has_fwd))
    def _():
        row = lax.broadcasted_iota(jnp.int32, (c, c), 0).astype(F32)
        col = lax.broadcasted_iota(jnp.int32, (c, c), 1).astype(F32)
        diff = row - col
        decay_ref[...] = jnp.where(diff >= 0.0,
                                   jnp.exp(jnp.maximum(diff, 0.0) * lg_f),
                                   jnp.exp(jnp.maximum(-diff, 0.0) * lg_fb))
        zetaf_ref[...] = jnp.exp((c - 1.0 - col) * lg_f).astype(BF16)
        idx = lax.broadcasted_iota(jnp.int32, (c, LANES), 0).astype(F32)
        xif_ref[...] = jnp.exp((idx + 1.0) * lg_f).astype(BF16)
        xib_ref[...] = jnp.exp((c - idx) * lg_fb).astype(BF16)
        fstate_ref[...] = jnp.zeros_like(fstate_ref)

    @pl.when(jnp.logical_and(j == 0, has_bwd))
    def _():
        col = lax.broadcasted_iota(jnp.int32, (c, c), 1).astype(F32)
        zetab_ref[...] = jnp.exp(col * lg_b).astype(BF16)
        bstate_ref[...] = jnp.zeros_like(bstate_ref)

    def row_scaled(a, scale_ref):
        scale = scale_ref[...]
        return jnp.concatenate(
            [a[:, n * LANES:(n + 1) * LANES] * scale
             for n in range(a.shape[1] // LANES)], axis=1)

    def decayed_kv(kt_ref, v_ref, zeta_ref, ci):
        rows = slice(ci * c, (ci + 1) * c)
        return jnp.dot(kt_ref[:, rows] * zeta_ref[...], v_ref[rows, :V_DIM],
                       preferred_element_type=F32)

    def bwd_sweep():
        blk = n_blocks - 1 - j
        slot = (p % 2) * n_chunks
        chunk_decay = jnp.exp(jnp.zeros((1, V_DIM), F32) + c * lg_b)
        state = bstate_ref[...]
        for ci in reversed(range(cpb)):
            rall_ref[slot + blk * cpb + ci] = state.astype(BF16)
            state = chunk_decay * state + decayed_kv(ktb_ref, vb_ref, zetab_ref, ci)
        bstate_ref[...] = state

    def fwd_sweep():
        slot = ((p - 1) % 2) * n_chunks
        chunk_decay = jnp.exp(jnp.zeros((1, V_DIM), F32) + c * lg_f)

        def decayed_scores(ci):
            rows = slice(ci * c, (ci + 1) * c)
            scores = jnp.dot(vq_ref[rows, V_DIM:], ktf_ref[:, rows],
                             preferred_element_type=F32)
            return (scores * decay_ref[...]).astype(BF16)

        state = fstate_ref[...]
        pmat_next = decayed_scores(0)
        kv_next = decayed_kv(ktf_ref, vq_ref, zetaf_ref, 0)
        for ci in range(cpb):
            rows = slice(ci * c, (ci + 1) * c)
            pmat, kv = pmat_next, kv_next
            if ci + 1 < cpb:
                pmat_next = decayed_scores(ci + 1)
                kv_next = decayed_kv(ktf_ref, vq_ref, zetaf_ref, ci + 1)
            q = vq_ref[rows, V_DIM:]
            lhs = jnp.concatenate([pmat, row_scaled(q, xib_ref), row_scaled(q, xif_ref)],
                                  axis=1)
            rhs = jnp.concatenate([vq_ref[rows, :V_DIM], rall_ref[slot + j * cpb + ci],
                                   state.astype(BF16)], axis=0)
            ret = jnp.dot(lhs, rhs, preferred_element_type=F32)
            mu = jnp.mean(ret, axis=-1, keepdims=True)
            var = jnp.mean(jnp.square(ret - mu), axis=-1, keepdims=True)
            rstd = lax.rsqrt(var + EPS)
            z_ref[rows, :] = (ret * rstd - mu * rstd).astype(BF16)
            state = chunk_decay * state + kv
        fstate_ref[...] = state

    @pl.when(jnp.logical_not(has_fwd))
    def _():
        bwd_sweep()

    @pl.when(jnp.logical_and(has_fwd, has_bwd))
    def _():
        fwd_sweep()
        bwd_sweep()

    @pl.when(jnp.logical_not(has_bwd))
    def _():
        fwd_sweep()


def _retention(lg, vq, kt, batch, seq, block_tokens):
    c = RET_CHUNK
    nb = seq // block_tokens
    cpb = block_tokens // c
    nc = seq // c
    t = batch * seq
    n_pairs = batch * RET_HEADS

    def fwd_idx(p, j):
        pair = jnp.maximum(p - 1, 0)
        return pair // RET_HEADS, pair % RET_HEADS, jnp.where(p == 0, 0, j)

    def bwd_idx(p, j):
        pair = jnp.minimum(p, n_pairs - 1)
        return pair // RET_HEADS, pair % RET_HEADS, jnp.where(p == n_pairs, 0, nb - 1 - j)

    def head_rows(idx_fn):
        def index_map(p, j):
            b, h, blk = idx_fn(p, j)
            return h, b * nb + blk, 0
        return index_map

    def head_major(idx_fn):
        def index_map(p, j):
            b, h, blk = idx_fn(p, j)
            return h, b * nb + blk
        return index_map

    return pl.pallas_call(
        functools.partial(_retention_kernel, n_pairs=n_pairs, n_blocks=nb,
                          chunks_per_block=cpb),
        grid=(n_pairs + 1, nb),
        in_specs=[
            pl.BlockSpec(memory_space=pltpu.SMEM),
            pl.BlockSpec((None, block_tokens, VQ_DIM), head_rows(fwd_idx)),
            pl.BlockSpec((QK_DIM, block_tokens), head_major(fwd_idx)),
            pl.BlockSpec((QK_DIM, block_tokens), head_major(bwd_idx)),
            pl.BlockSpec((None, block_tokens, V_DIM), head_rows(bwd_idx)),
        ],
        out_specs=pl.BlockSpec((None, block_tokens, V_DIM), head_rows(fwd_idx)),
        out_shape=jax.ShapeDtypeStruct((RET_HEADS, t, V_DIM), BF16),
        scratch_shapes=[
            pltpu.VMEM((c, c), F32),
            pltpu.VMEM((c, LANES), BF16),
            pltpu.VMEM((c, LANES), BF16),
            pltpu.VMEM((QK_DIM, c), BF16),
            pltpu.VMEM((QK_DIM, c), BF16),
            pltpu.VMEM((QK_DIM, V_DIM), F32),
            pltpu.VMEM((QK_DIM, V_DIM), F32),
            pltpu.VMEM((2 * nc, QK_DIM, V_DIM), BF16),
        ],
        compiler_params=pltpu.CompilerParams(
            dimension_semantics=("arbitrary", "arbitrary"),
            vmem_limit_bytes=VMEM_LIMIT),
        name="retention",
    )(lg, vq, kt, kt, vq)


def _merge_kernel(u_ref, gb_ref, up_ref, un_ref, ga_ref, gr_ref, z_ref, gsw_ref,
                  wconv_ref, wa_ref, wr_ref, o_ref, *, tiles_per_seq):
    i = pl.program_id(0)
    tm = u_ref.shape[0]
    u = u_ref[...].astype(F32)
    pos = i % tiles_per_seq
    last = BF16_SUBLANES - 1
    u_before = up_ref[last:last + 1, :].astype(F32) * (pos != 0).astype(F32)
    u_after = un_ref[0:1, :].astype(F32) * (pos != tiles_per_seq - 1).astype(F32)
    row = lax.broadcasted_iota(jnp.int32, u.shape, 0)
    u_prev = jnp.where(row == 0, u_before, pltpu.roll(u, 1, axis=0))
    u_next = jnp.where(row == tm - 1, u_after, pltpu.roll(u, tm - 1, axis=0))
    conv = u_prev * wconv_ref[0:1, :] + u * wconv_ref[1:2, :] + u_next * wconv_ref[2:3, :]
    a_in = (gb_ref[...].astype(F32) * conv).astype(BF16)

    def branch_outputs(rows):
        z = jnp.concatenate([z_ref[hh, rows, :] for hh in range(RET_HEADS)], axis=1)
        y_r = jnp.dot(z * gsw_ref[rows, :], wr_ref[...], preferred_element_type=F32)
        y_a = jnp.dot(a_in[rows, :], wa_ref[...], preferred_element_type=F32)
        return y_a, y_r

    blocks = [slice(r, r + SUBBLOCK_ROWS) for r in range(0, tm, SUBBLOCK_ROWS)]
    y_next = branch_outputs(blocks[0])
    for s, rows in enumerate(blocks):
        y_a, y_r = y_next
        if s + 1 < len(blocks):
            y_next = branch_outputs(blocks[s + 1])
        merged = ga_ref[rows, :].astype(F32) * y_a + gr_ref[rows, :].astype(F32) * y_r
        o_ref[rows, :] = merged.astype(BF16)


def _merge(proj, z, w_conv, w_a, w_r, seq, tm):
    t, d = proj.shape[0], D_MODEL
    hb = tm // BF16_SUBLANES
    n_halo_blocks = t // BF16_SUBLANES
    halo = (BF16_SUBLANES, SEC)
    return pl.pallas_call(
        functools.partial(_merge_kernel, tiles_per_seq=seq // tm),
        grid=(t // tm,),
        in_specs=[
            pl.BlockSpec((tm, SEC), lambda i: (i, SEC_U)),
            pl.BlockSpec((tm, SEC), lambda i: (i, SEC_GB)),
            pl.BlockSpec(halo, lambda i: (jnp.maximum(i * hb - 1, 0), SEC_U)),
            pl.BlockSpec(halo, lambda i: (jnp.minimum((i + 1) * hb, n_halo_blocks - 1), SEC_U)),
            pl.BlockSpec((tm, SEC), lambda i: (i, SEC_GA)),
            pl.BlockSpec((tm, SEC), lambda i: (i, SEC_GR)),
            pl.BlockSpec((RET_HEADS, tm, V_DIM), lambda i: (0, i, 0)),
            pl.BlockSpec((tm, RET_V), lambda i: (i, (SEC_GSW * SEC) // RET_V)),
            _const_spec(w_conv.shape),
            _const_spec(w_a.shape),
            _const_spec(w_r.shape),
        ],
        out_specs=pl.BlockSpec((tm, d), lambda i: (i, 0)),
        out_shape=jax.ShapeDtypeStruct((t, d), BF16),
        compiler_params=pltpu.CompilerParams(
            dimension_semantics=("arbitrary",),
            vmem_limit_bytes=VMEM_LIMIT),
        name="merge",
    )(proj, proj, proj, proj, proj, proj, z, proj, w_conv, w_a, w_r)


def _ffn_kernel(x_ref, m_ref, wo_ref, gffn_ref, wg_ref, wu_ref, wd_ref, gfin_ref, o_ref,
                *, final_norm):
    def residual(rows):
        return x_ref[rows, :] + jnp.dot(m_ref[rows, :], wo_ref[...], preferred_element_type=F32)

    def gate_up(x1):
        h2 = (_rms_scale(x1) * gffn_ref[...]).astype(BF16)
        return (jnp.dot(h2, wg_ref[...], preferred_element_type=F32),
                jnp.dot(h2, wu_ref[...], preferred_element_type=F32), x1)

    blocks = [slice(r, r + SUBBLOCK_ROWS) for r in range(0, x_ref.shape[0], SUBBLOCK_ROWS)]
    x1s = [residual(rows) for rows in blocks[:2]]
    gu_next = gate_up(x1s[0])
    for s, rows in enumerate(blocks):
        gate, up, x1 = gu_next
        if s + 2 < len(blocks):
            x1s.append(residual(blocks[s + 2]))
        if s + 1 < len(blocks):
            gu_next = gate_up(x1s[s + 1])
        act = (gate * _sigmoid(gate) * up).astype(BF16)
        y = x1 + jnp.dot(act, wd_ref[...], preferred_element_type=F32)
        o_ref[rows, :] = _rms_scale(y) * gfin_ref[...] if final_norm else y


def _ffn(x2, merged, w_o, g_ffn, w_gate, w_up, w_down, g_final, final_norm, tm):
    t, d = x2.shape
    return pl.pallas_call(
        functools.partial(_ffn_kernel, final_norm=final_norm),
        grid=(t // tm,),
        in_specs=[
            pl.BlockSpec((tm, d), lambda i: (i, 0)),
            pl.BlockSpec((tm, d), lambda i: (i, 0)),
            _const_spec(w_o.shape),
            _const_spec(g_ffn.shape),
            _const_spec(w_gate.shape),
            _const_spec(w_up.shape),
            _const_spec(w_down.shape),
            _const_spec(g_final.shape),
        ],
        out_specs=pl.BlockSpec((tm, d), lambda i: (i, 0)),
        out_shape=jax.ShapeDtypeStruct((t, d), F32),
        compiler_params=pltpu.CompilerParams(
            dimension_semantics=("arbitrary",),
            vmem_limit_bytes=VMEM_LIMIT),
        name="ffn",
    )(x2, merged, w_o, g_ffn, w_gate, w_up, w_down, g_final)


def _rotary_tables(seq, tm):
    freqs = ROPE_BASE ** (-jnp.arange(0, QK_DIM, 2, dtype=F32) / QK_DIM)
    base = (jnp.arange(seq // tm, dtype=F32) * tm)[:, None] * freqs[None, :]
    off = jnp.arange(tm, dtype=F32)[:, None] * freqs[None, :]
    return jnp.cos(base), jnp.sin(base), jnp.cos(off), jnp.sin(off)


def kernel(x, g_mix, w_in, w_conv, dec_f, dec_b, g_ret, w_a_out, w_r_out, w_o,
           g_ffn, w_ff_gate, w_ff_up, w_ff_down, g_final):
    batch, seq, d = x.shape
    depth = w_in.shape[0]
    assert d == D_MODEL and seq % RET_BLOCK_TOKENS == 0 and seq % MERGE_ROWS == 0
    x2 = x.reshape(batch * seq, d)
    rot = _rotary_tables(seq, INPROJ_ROWS)
    for l in range(depth):
        lg = jnp.stack([jax.nn.log_sigmoid(dec_f[l].astype(F32)),
                        jax.nn.log_sigmoid(dec_b[l].astype(F32))])
        side = [w_a_out[l], w_r_out[l], w_o[l], w_ff_gate[l], w_ff_up[l], w_ff_down[l]]
        proj, kt, vq, (w_a, w_r, w_ob, w_gate, w_up, w_down) = _inproj(
            x2, g_mix[l][None, :], w_in[l], rot,
            g_ret[l][None, :].astype(F32), side, tm=INPROJ_ROWS)
        z = _retention(lg, vq, kt, batch, seq, block_tokens=RET_BLOCK_TOKENS)
        merged = _merge(proj, z, w_conv[l], w_a, w_r, seq, tm=MERGE_ROWS)
        x2 = _ffn(x2, merged, w_ob, g_ffn[l][None, :], w_gate, w_up, w_down,
                  g_final[None, :], final_norm=(l == depth - 1), tm=FFN_ROWS)
    return x2.reshape(batch, seq, d)
```

```python
import functools

import jax
import jax.numpy as jnp
from jax import lax
from jax.experimental import pallas as pl
from jax.experimental.pallas import tpu as pltpu

F32 = jnp.float32
BF16 = jnp.bfloat16

D_MODEL = 1024
RET_HEADS = 4
QK_DIM = D_MODEL // RET_HEADS
V_DIM = 2 * QK_DIM
RET_V = RET_HEADS * V_DIM
ROPE_BASE = 10000.0
EPS = 1e-6

SEC = 1024
W_XC, W_GB, W_GC, W_Q, W_K, W_V, W_GSW, W_GA, W_GR, N_W_SEC = 0, 1, 2, 3, 4, 5, 7, 9, 10, 11
SEC_U, SEC_GB, SEC_GSW, SEC_GA, SEC_GR, N_OUT_SEC = 0, 1, 2, 4, 5, 6
VQ_DIM = V_DIM + QK_DIM

RET_CHUNK = 256
SIDE_CAST_STEPS = 16
SUBBLOCK_ROWS = 256
INPROJ_ROWS = 256
RET_BLOCK_TOKENS = 4096
MERGE_ROWS = 512
FFN_ROWS = 512
LANES = 128
BF16_SUBLANES = 16

VMEM_LIMIT = 56 * 1024 * 1024


def _sigmoid(x):
    return 1.0 / (1.0 + jnp.exp(-x))


def _rms_scale(x):
    return x * lax.rsqrt(jnp.mean(x * x, axis=-1, keepdims=True) + EPS)


def _const_spec(shape):
    return pl.BlockSpec(shape, lambda *_: (0,) * len(shape), pipeline_mode=pl.Buffered(1))


def _inproj_kernel(x_ref, g_ref, w_ref, cos_base_ref, sin_base_ref, cos_off_ref, sin_off_ref,
                   gret_ref, *rest, n_pos_blocks, n_side):
    side_in = rest[:n_side]
    o_ref, kt_ref, vq_ref = rest[n_side:n_side + 3]
    side_out = rest[n_side + 3:2 * n_side + 3]
    w_bf16_ref = rest[2 * n_side + 3]
    step = pl.program_id(0) - N_W_SEC

    @pl.when(step < 0)
    def _():
        w_bf16_ref[pl.program_id(0)] = w_ref[...].astype(BF16)

    @pl.when(step >= 0)
    def _():
        h = (_rms_scale(x_ref[...]) * g_ref[...]).astype(BF16)
        half = QK_DIM // 2
        pos_block = pl.ds(step % n_pos_blocks, 1)
        cos_b = cos_base_ref[pos_block, :]
        sin_b = sin_base_ref[pos_block, :]
        cos = cos_b * cos_off_ref[...] - sin_b * sin_off_ref[...]
        sin = sin_b * cos_off_ref[...] + cos_b * sin_off_ref[...]

        def project(first_sec, n_sec):
            return jnp.concatenate(
                [jnp.dot(h, w_bf16_ref[sec], preferred_element_type=F32)
                 for sec in range(first_sec, first_sec + n_sec)], axis=1)

        def out_cols(sec, n_sec=1):
            return slice(sec * SEC, (sec + n_sec) * SEC)

        qk = project(W_Q, 2)
        for sec in range(2):
            for hh in range(RET_HEADS):
                lo = sec * SEC + hh * QK_DIM
                t1 = qk[:, lo:lo + half]
                t2 = qk[:, lo + half:lo + QK_DIM]
                r1 = t1 * cos - t2 * sin
                r2 = t1 * sin + t2 * cos
                if sec == 0:
                    scale = QK_DIM ** -0.5
                    vq_ref[hh, :, V_DIM:V_DIM + half] = (r1 * scale).astype(BF16)
                    vq_ref[hh, :, V_DIM + half:VQ_DIM] = (r2 * scale).astype(BF16)
                else:
                    out = hh * QK_DIM
                    kt_ref[out:out + half, :] = r1.T.astype(BF16)
                    kt_ref[out + half:out + QK_DIM, :] = r2.T.astype(BF16)

        gsw = project(W_GSW, 2)
        o_ref[:, out_cols(SEC_GSW, 2)] = (gsw * _sigmoid(gsw) * gret_ref[...]).astype(BF16)

        gates = project(W_GA, 2)
        o_ref[:, out_cols(SEC_GA, 2)] = _sigmoid(gates).astype(BF16)

        conv_in = project(W_XC, 3)
        o_ref[:, out_cols(SEC_U)] = (conv_in[:, out_cols(W_GC)] * conv_in[:, out_cols(W_XC)]
                                     ).astype(BF16)
        o_ref[:, out_cols(SEC_GB)] = conv_in[:, out_cols(W_GB)].astype(BF16)

        values = project(W_V, 2)
        for hh in range(RET_HEADS):
            vq_ref[hh, :, :V_DIM] = values[:, hh * V_DIM:(hh + 1) * V_DIM].astype(BF16)

    @pl.when(jnp.logical_and(step >= 0, step < SIDE_CAST_STEPS))
    def _():
        for src, dst in zip(side_in, side_out):
            dst[...] = src[...].astype(BF16)


def _inproj(x2, g_mix, w_in, rot, g_ret, side_weights, tm):
    t, d = x2.shape
    cos_base, sin_base, cos_off, sin_off = rot
    assert cos_off.shape[0] == tm and w_in.shape[1] == N_W_SEC * SEC
    n_out = N_OUT_SEC * SEC
    n_steps = t // tm
    assert n_steps >= SIDE_CAST_STEPS

    def tile(i):
        return jnp.maximum(i - N_W_SEC, 0)

    def slab_spec(w):
        rows = w.shape[0] // SIDE_CAST_STEPS
        assert rows * SIDE_CAST_STEPS == w.shape[0] and rows % BF16_SUBLANES == 0
        return pl.BlockSpec((rows, w.shape[1]),
                            lambda i: (jnp.minimum(tile(i), SIDE_CAST_STEPS - 1), 0))

    side_specs = [slab_spec(w) for w in side_weights]
    outs = pl.pallas_call(
        functools.partial(_inproj_kernel, n_pos_blocks=cos_base.shape[0],
                          n_side=len(side_weights)),
        grid=(N_W_SEC + n_steps,),
        in_specs=[
            pl.BlockSpec((tm, d), lambda i: (tile(i), 0)),
            _const_spec(g_mix.shape),
            pl.BlockSpec((d, SEC), lambda i: (0, jnp.minimum(i, N_W_SEC - 1))),
            _const_spec(cos_base.shape),
            _const_spec(sin_base.shape),
            _const_spec(cos_off.shape),
            _const_spec(sin_off.shape),
            _const_spec(g_ret.shape),
        ] + side_specs,
        out_specs=[
            pl.BlockSpec((tm, n_out), lambda i: (tile(i), 0)),
            pl.BlockSpec((RET_HEADS * QK_DIM, tm), lambda i: (0, tile(i))),
            pl.BlockSpec((RET_HEADS, tm, VQ_DIM), lambda i: (0, tile(i), 0)),
        ] + side_specs,
        out_shape=[
            jax.ShapeDtypeStruct((t, n_out), BF16),
            jax.ShapeDtypeStruct((RET_HEADS * QK_DIM, t), BF16),
            jax.ShapeDtypeStruct((RET_HEADS, t, VQ_DIM), BF16),
        ] + [jax.ShapeDtypeStruct(w.shape, BF16) for w in side_weights],
        scratch_shapes=[pltpu.VMEM((N_W_SEC, d, SEC), BF16)],
        compiler_params=pltpu.CompilerParams(
            dimension_semantics=("arbitrary",),
            vmem_limit_bytes=VMEM_LIMIT),
        name="inproj",
    )(x2, g_mix, w_in, cos_base, sin_base, cos_off, sin_off, g_ret, *side_weights)
    return outs[0], outs[1], outs[2], outs[3:]


def _retention_kernel(lg_ref, vq_ref, ktf_ref, ktb_ref, vb_ref, z_ref,
                      decay_ref, xif_ref, xib_ref, zetaf_ref, zetab_ref,
                      fstate_ref, bstate_ref, rall_ref,
                      *, n_pairs, n_blocks, chunks_per_block):
    c = RET_CHUNK
    cpb = chunks_per_block
    n_chunks = n_blocks * cpb
    p = pl.program_id(0)
    j = pl.program_id(1)
    head_f = jnp.maximum(p - 1, 0) % RET_HEADS
    head_b = jnp.minimum(p, n_pairs - 1) % RET_HEADS
    lg_f = lg_ref[0, head_f]
    lg_fb = lg_ref[1, head_f]
    lg_b = lg_ref[1, head_b]
    has_fwd = p >= 1
    has_bwd = p < n_pairs

    @pl.when(jnp.logical_and(j == 0, has_fwd))
    def _():
        row = lax.broadcasted_iota(jnp.int32, (c, c), 0).astype(F32)
        col = lax.broadcasted_iota(jnp.int32, (c, c), 1).astype(F32)
        diff = row - col
        decay_ref[...] = jnp.where(diff >= 0.0,
                                   jnp.exp(jnp.maximum(diff, 0.0) * lg_f),
                                   jnp.exp(jnp.maximum(-diff, 0.0) * lg_fb))
        zetaf_ref[...] = jnp.exp((c - 1.0 - col) * lg_f).astype(BF16)
        idx = lax.broadcasted_iota(jnp.int32, (c, LANES), 0).astype(F32)
        xif_ref[...] = jnp.exp((idx + 1.0) * lg_f).astype(BF16)
        xib_ref[...] = jnp.exp((c - idx) * lg_fb).astype(BF16)
        fstate_ref[...] = jnp.zeros_like(fstate_ref)

    @pl.when(jnp.logical_and(j == 0, has_bwd))
    def _():
        col = lax.broadcasted_iota(jnp.int32, (c, c), 1).astype(F32)
        zetab_ref[...] = jnp.exp(col * lg_b).astype(BF16)
        bstate_ref[...] = jnp.zeros_like(bstate_ref)

    def row_scaled(a, scale_ref):
        scale = scale_ref[...]
        return jnp.concatenate(
            [a[:, n * LANES:(n + 1) * LANES] * scale
             for n in range(a.shape[1] // LANES)], axis=1)

    def decayed_kv(kt_ref, v_ref, zeta_ref, ci):
        rows = slice(ci * c, (ci + 1) * c)
        return jnp.dot(kt_ref[:, rows] * zeta_ref[...], v_ref[rows, :V_DIM],
                       preferred_element_type=F32)

    def bwd_sweep():
        blk = n_blocks - 1 - j
        slot = (p % 2) * n_chunks
        chunk_decay = jnp.exp(jnp.zeros((1, V_DIM), F32) + c * lg_b)
        state = bstate_ref[...]
        for ci in reversed(range(cpb)):
            rall_ref[slot + blk * cpb + ci] = state.astype(BF16)
            state = chunk_decay * state + decayed_kv(ktb_ref, vb_ref, zetab_ref, ci)
        bstate_ref[...] = state

    def fwd_sweep():
        slot = ((p - 1) % 2) * n_chunks
        chunk_decay = jnp.exp(jnp.zeros((1, V_DIM), F32) + c * lg_f)

        def decayed_scores(ci):
            rows = slice(ci * c, (ci + 1) * c)
            scores = jnp.dot(vq_ref[rows, V_DIM:], ktf_ref[:, rows],
                             preferred_element_type=F32)
            return (scores * decay_ref[...]).astype(BF16)

        state = fstate_ref[...]
        pmat_next = decayed_scores(0)
        kv_next = decayed_kv(ktf_ref, vq_ref, zetaf_ref, 0)
        for ci in range(cpb):
            rows = slice(ci * c, (ci + 1) * c)
            pmat, kv = pmat_next, kv_next
            if ci + 1 < cpb:
                pmat_next = decayed_scores(ci + 1)
                kv_next = decayed_kv(ktf_ref, vq_ref, zetaf_ref, ci + 1)
            q = vq_ref[rows, V_DIM:]
            lhs = jnp.concatenate([pmat, row_scaled(q, xib_ref), row_scaled(q, xif_ref)],
                                  axis=1)
            rhs = jnp.concatenate([vq_ref[rows, :V_DIM], rall_ref[slot + j * cpb + ci],
                                   state.astype(BF16)], axis=0)
            ret = jnp.dot(lhs, rhs, preferred_element_type=F32)
            mu = jnp.mean(ret, axis=-1, keepdims=True)
            var = jnp.mean(jnp.square(ret - mu), axis=-1, keepdims=True)
            rstd = lax.rsqrt(var + EPS)
            z_ref[rows, :] = (ret * rstd - mu * rstd).astype(BF16)
            state = chunk_decay * state + kv
        fstate_ref[...] = state

    @pl.when(jnp.logical_not(has_fwd))
    def _():
        bwd_sweep()

    @pl.when(jnp.logical_and(has_fwd, has_bwd))
    def _():
        fwd_sweep()
        bwd_sweep()

    @pl.when(jnp.logical_not(has_bwd))
    def _():
        fwd_sweep()


def _retention(lg, vq, kt, batch, seq, block_tokens):
    c = RET_CHUNK
    nb = seq // block_tokens
    cpb = block_tokens // c
    nc = seq // c
    t = batch * seq
    n_pairs = batch * RET_HEADS

    def fwd_idx(p, j):
        pair = jnp.maximum(p - 1, 0)
        return pair // RET_HEADS, pair % RET_HEADS, jnp.where(p == 0, 0, j)

    def bwd_idx(p, j):
        pair = jnp.minimum(p, n_pairs - 1)
        return pair // RET_HEADS, pair % RET_HEADS, jnp.where(p == n_pairs, 0, nb - 1 - j)

    def head_rows(idx_fn):
        def index_map(p, j):
            b, h, blk = idx_fn(p, j)
            return h, b * nb + blk, 0
        return index_map

    def head_major(idx_fn):
        def index_map(p, j):
            b, h, blk = idx_fn(p, j)
            return h, b * nb + blk
        return index_map

    return pl.pallas_call(
        functools.partial(_retention_kernel, n_pairs=n_pairs, n_blocks=nb,
                          chunks_per_block=cpb),
        grid=(n_pairs + 1, nb),
        in_specs=[
            pl.BlockSpec(memory_space=pltpu.SMEM),
            pl.BlockSpec((None, block_tokens, VQ_DIM), head_rows(fwd_idx)),
            pl.BlockSpec((QK_DIM, block_tokens), head_major(fwd_idx)),
            pl.BlockSpec((QK_DIM, block_tokens), head_major(bwd_idx)),
            pl.BlockSpec((None, block_tokens, V_DIM), head_rows(bwd_idx)),
        ],
        out_specs=pl.BlockSpec((None, block_tokens, V_DIM), head_rows(fwd_idx)),
        out_shape=jax.ShapeDtypeStruct((RET_HEADS, t, V_DIM), BF16),
        scratch_shapes=[
            pltpu.VMEM((c, c), F32),
            pltpu.VMEM((c, LANES), BF16),
            pltpu.VMEM((c, LANES), BF16),
            pltpu.VMEM((QK_DIM, c), BF16),
            pltpu.VMEM((QK_DIM, c), BF16),
            pltpu.VMEM((QK_DIM, V_DIM), F32),
            pltpu.VMEM((QK_DIM, V_DIM), F32),
            pltpu.VMEM((2 * nc, QK_DIM, V_DIM), BF16),
        ],
        compiler_params=pltpu.CompilerParams(
            dimension_semantics=("arbitrary", "arbitrary"),
            vmem_limit_bytes=VMEM_LIMIT),
        name="retention",
    )(lg, vq, kt, kt, vq)


def _merge_kernel(u_ref, gb_ref, up_ref, un_ref, ga_ref, gr_ref, z_ref, gsw_ref,
                  wconv_ref, wa_ref, wr_ref, o_ref, *, tiles_per_seq):
    i = pl.program_id(0)
    tm = u_ref.shape[0]
    u = u_ref[...].astype(F32)
    pos = i % tiles_per_seq
    last = BF16_SUBLANES - 1
    u_before = up_ref[last:last + 1, :].astype(F32) * (pos != 0).astype(F32)
    u_after = un_ref[0:1, :].astype(F32) * (pos != tiles_per_seq - 1).astype(F32)
    row = lax.broadcasted_iota(jnp.int32, u.shape, 0)
    u_prev = jnp.where(row == 0, u_before, pltpu.roll(u, 1, axis=0))
    u_next = jnp.where(row == tm - 1, u_after, pltpu.roll(u, tm - 1, axis=0))
    conv = u_prev * wconv_ref[0:1, :] + u * wconv_ref[1:2, :] + u_next * wconv_ref[2:3, :]
    a_in = (gb_ref[...].astype(F32) * conv).astype(BF16)

    def branch_outputs(rows):
        z = jnp.concatenate([z_ref[hh, rows, :] for hh in range(RET_HEADS)], axis=1)
        y_r = jnp.dot(z * gsw_ref[rows, :], wr_ref[...], preferred_element_type=F32)
        y_a = jnp.dot(a_in[rows, :], wa_ref[...], preferred_element_type=F32)
        return y_a, y_r

    blocks = [slice(r, r + SUBBLOCK_ROWS) for r in range(0, tm, SUBBLOCK_ROWS)]
    y_next = branch_outputs(blocks[0])
    for s, rows in enumerate(blocks):
        y_a, y_r = y_next
        if s + 1 < len(blocks):
            y_next = branch_outputs(blocks[s + 1])
        merged = ga_ref[rows, :].astype(F32) * y_a + gr_ref[rows, :].astype(F32) * y_r
        o_ref[rows, :] = merged.astype(BF16)


def _merge(proj, z, w_conv, w_a, w_r, seq, tm):
    t, d = proj.shape[0], D_MODEL
    hb = tm // BF16_SUBLANES
    n_halo_blocks = t // BF16_SUBLANES
    halo = (BF16_SUBLANES, SEC)
    return pl.pallas_call(
        functools.partial(_merge_kernel, tiles_per_seq=seq // tm),
        grid=(t // tm,),
        in_specs=[
            pl.BlockSpec((tm, SEC), lambda i: (i, SEC_U)),
            pl.BlockSpec((tm, SEC), lambda i: (i, SEC_GB)),
            pl.BlockSpec(halo, lambda i: (jnp.maximum(i * hb - 1, 0), SEC_U)),
            pl.BlockSpec(halo, lambda i: (jnp.minimum((i + 1) * hb, n_halo_blocks - 1), SEC_U)),
            pl.BlockSpec((tm, SEC), lambda i: (i, SEC_GA)),
            pl.BlockSpec((tm, SEC), lambda i: (i, SEC_GR)),
            pl.BlockSpec((RET_HEADS, tm, V_DIM), lambda i: (0, i, 0)),
            pl.BlockSpec((tm, RET_V), lambda i: (i, (SEC_GSW * SEC) // RET_V)),
            _const_spec(w_conv.shape),
            _const_spec(w_a.shape),
            _const_spec(w_r.shape),
        ],
        out_specs=pl.BlockSpec((tm, d), lambda i: (i, 0)),
        out_shape=jax.ShapeDtypeStruct((t, d), BF16),
        compiler_params=pltpu.CompilerParams(
            dimension_semantics=("arbitrary",),
            vmem_limit_bytes=VMEM_LIMIT),
        name="merge",
    )(proj, proj, proj, proj, proj, proj, z, proj, w_conv, w_a, w_r)


def _ffn_kernel(x_ref, m_ref, wo_ref, gffn_ref, wg_ref, wu_ref, wd_ref, gfin_ref, o_ref,
                *, final_norm):
    def residual(rows):
        return x_ref[rows, :] + jnp.dot(m_ref[rows, :], wo_ref[...], preferred_element_type=F32)

    def gate_up(x1):
        h2 = (_rms_scale(x1) * gffn_ref[...]).astype(BF16)
        return (jnp.dot(h2, wg_ref[...], preferred_element_type=F32),
                jnp.dot(h2, wu_ref[...], preferred_element_type=F32), x1)

    blocks = [slice(r, r + SUBBLOCK_ROWS) for r in range(0, x_ref.shape[0], SUBBLOCK_ROWS)]
    x1s = [residual(rows) for rows in blocks[:2]]
    gu_next = gate_up(x1s[0])
    for s, rows in enumerate(blocks):
        gate, up, x1 = gu_next
        if s + 2 < len(blocks):
            x1s.append(residual(blocks[s + 2]))
        if s + 1 < len(blocks):
            gu_next = gate_up(x1s[s + 1])
        act = (gate * _sigmoid(gate) * up).astype(BF16)
        y = x1 + jnp.dot(act, wd_ref[...], preferred_element_type=F32)
        o_ref[rows, :] = _rms_scale(y) * gfin_ref[...] if final_norm else y


def _ffn(x2, merged, w_o, g_ffn, w_gate, w_up, w_down, g_final, final_norm, tm):
    t, d = x2.shape
    return pl.pallas_call(
        functools.partial(_ffn_kernel, final_norm=final_norm),
        grid=(t // tm,),
        in_specs=[
            pl.BlockSpec((tm, d), lambda i: (i, 0)),
            pl.BlockSpec((tm, d), lambda i: (i, 0)),
            _const_spec(w_o.shape),
            _const_spec(g_ffn.shape),
            _const_spec(w_gate.shape),
            _const_spec(w_up.shape),
            _const_spec(w_down.shape),
            _const_spec(g_final.shape),
        ],
        out_specs=pl.BlockSpec((tm, d), lambda i: (i, 0)),
        out_shape=jax.ShapeDtypeStruct((t, d), F32),
        compiler_params=pltpu.CompilerParams(
            dimension_semantics=("arbitrary",),
            vmem_limit_bytes=VMEM_LIMIT),
        name="ffn",
    )(x2, merged, w_o, g_ffn, w_gate, w_up, w_down, g_final)


def _rotary_tables(seq, tm):
    freqs = ROPE_BASE ** (-jnp.arange(0, QK_DIM, 2, dtype=F32) / QK_DIM)
    base = (jnp.arange(seq // tm, dtype=F32) * tm)[:, None] * freqs[None, :]
    off = jnp.arange(tm, dtype=F32)[:, None] * freqs[None, :]
    return jnp.cos(base), jnp.sin(base), jnp.cos(off), jnp.sin(off)


def kernel(x, g_mix, w_in, w_conv, dec_f, dec_b, g_ret, w_a_out, w_r_out, w_o,
           g_ffn, w_ff_gate, w_ff_up, w_ff_down, g_final):
    batch, seq, d = x.shape
    depth = w_in.shape[0]
    assert d == D_MODEL and seq % RET_BLOCK_TOKENS == 0 and seq % MERGE_ROWS == 0
    x2 = x.reshape(batch * seq, d)
    rot = _rotary_tables(seq, INPROJ_ROWS)
    for l in range(depth):
        lg = jnp.stack([jax.nn.log_sigmoid(dec_f[l].astype(F32)),
                        jax.nn.log_sigmoid(dec_b[l].astype(F32))])
        side = [w_a_out[l], w_r_out[l], w_o[l], w_ff_gate[l], w_ff_up[l], w_ff_down[l]]
        proj, kt, vq, (w_a, w_r, w_ob, w_gate, w_up, w_down) = _inproj(
            x2, g_mix[l][None, :], w_in[l], rot,
            g_ret[l][None, :].astype(F32), side, tm=INPROJ_ROWS)
        z = _retention(lg, vq, kt, batch, seq, block_tokens=RET_BLOCK_TOKENS)
        merged = _merge(proj, z, w_conv[l], w_a, w_r, seq, tm=MERGE_ROWS)
        x2 = _ffn(x2, merged, w_ob, g_ffn[l][None, :], w_gate, w_up, w_down,
                  g_final[None, :], final_norm=(l == depth - 1), tm=FFN_ROWS)
    return x2.reshape(batch, seq, d)
```

```python
import functools

import jax
import jax.numpy as jnp
from jax import lax
from jax.experimental import pallas as pl
from jax.experimental.pallas import tpu as pltpu

F32 = jnp.float32
BF16 = jnp.bfloat16

D_MODEL = 1024
RET_HEADS = 4
QK_DIM = D_MODEL // RET_HEADS
V_DIM = 2 * QK_DIM
RET_V = RET_HEADS * V_DIM
ROPE_BASE = 10000.0
EPS = 1e-6

SEC = 1024
W_XC, W_GB, W_GC, W_Q, W_K, W_V, W_GSW, W_GA, W_GR, N_W_SEC = 0, 1, 2, 3, 4, 5, 7, 9, 10, 11
SEC_U, SEC_GB, SEC_GSW, SEC_GA, SEC_GR, N_OUT_SEC = 0, 1, 2, 4, 5, 6
VQ_DIM = V_DIM + QK_DIM

RET_CHUNK = 256
SIDE_CAST_STEPS = 16
SUBBLOCK_ROWS = 256
INPROJ_ROWS = 256
RET_BLOCK_TOKENS = 4096
MERGE_ROWS = 512
FFN_ROWS = 512
LANES = 128
BF16_SUBLANES = 16

VMEM_LIMIT = 56 * 1024 * 1024


def _sigmoid(x):
    return 1.0 / (1.0 + jnp.exp(-x))


def _rms_scale(x):
    return x * lax.rsqrt(jnp.mean(x * x, axis=-1, keepdims=True) + EPS)


def _const_spec(shape):
    return pl.BlockSpec(shape, lambda *_: (0,) * len(shape), pipeline_mode=pl.Buffered(1))


def _inproj_kernel(x_ref, g_ref, w_ref, cos_base_ref, sin_base_ref, cos_off_ref, sin_off_ref,
                   gret_ref, *rest, n_pos_blocks, n_side):
    side_in = rest[:n_side]
    o_ref, kt_ref, vq_ref = rest[n_side:n_side + 3]
    side_out = rest[n_side + 3:2 * n_side + 3]
    w_bf16_ref, h0_ref, xc0_ref = rest[2 * n_side + 3:]
    pid = pl.program_id(0)
    step = pid - N_W_SEC
    half = QK_DIM // 2

    def normalised_tile():
        return (_rms_scale(x_ref[...]) * g_ref[...]).astype(BF16)

    def rotary_tables(tile_index):
        pos_block = pl.ds(tile_index % n_pos_blocks, 1)
        cos_b = cos_base_ref[pos_block, :]
        sin_b = sin_base_ref[pos_block, :]
        return (cos_b * cos_off_ref[...] - sin_b * sin_off_ref[...],
                sin_b * cos_off_ref[...] + cos_b * sin_off_ref[...])

    def out_cols(sec):
        return slice(sec * SEC, (sec + 1) * SEC)

    def store_section(sec, acc, tile_index, x_c):
        if sec in (W_Q, W_K):
            cos, sin = rotary_tables(tile_index)
            for hh in range(RET_HEADS):
                lo = hh * QK_DIM
                t1 = acc[:, lo:lo + half]
                t2 = acc[:, lo + half:lo + QK_DIM]
                r1 = t1 * cos - t2 * sin
                r2 = t1 * sin + t2 * cos
                if sec == W_Q:
                    scale = QK_DIM ** -0.5
                    vq_ref[hh, :, V_DIM:V_DIM + half] = (r1 * scale).astype(BF16)
                    vq_ref[hh, :, V_DIM + half:VQ_DIM] = (r2 * scale).astype(BF16)
                else:
                    kt_ref[lo:lo + half, :] = r1.T.astype(BF16)
                    kt_ref[lo + half:lo + QK_DIM, :] = r2.T.astype(BF16)
        elif W_V <= sec < W_GSW:
            heads_per_sec = SEC // V_DIM
            for n in range(heads_per_sec):
                hh = (sec - W_V) * heads_per_sec + n
                vq_ref[hh, :, :V_DIM] = acc[:, n * V_DIM:(n + 1) * V_DIM].astype(BF16)
        elif W_GSW <= sec < W_GA:
            gret = gret_ref[:, out_cols(sec - W_GSW)]
            o_ref[:, out_cols(SEC_GSW + sec - W_GSW)] = (acc * _sigmoid(acc) * gret).astype(BF16)
        elif sec >= W_GA:
            o_ref[:, out_cols(SEC_GA + sec - W_GA)] = _sigmoid(acc).astype(BF16)
        elif sec == W_GB:
            o_ref[:, out_cols(SEC_GB)] = acc.astype(BF16)
        else:
            assert sec == W_GC
            o_ref[:, out_cols(SEC_U)] = (acc * x_c).astype(BF16)

    for sec in range(N_W_SEC):
        @pl.when(pid == sec)
        def _(sec=sec):
            w_bf16_ref[sec] = w_ref[...].astype(BF16)
            if sec == 0:
                h0_ref[...] = normalised_tile()
            acc = jnp.dot(h0_ref[...], w_bf16_ref[sec], preferred_element_type=F32)
            if sec == W_XC:
                xc0_ref[...] = acc
            else:
                store_section(sec, acc, 0, xc0_ref[...] if sec == W_GC else None)

    @pl.when(step >= 0)
    def _():
        h = normalised_tile()
        order = sorted(range(N_W_SEC), key=lambda c: (not (W_Q <= c <= W_K or c >= W_GSW),
                                                      c != W_XC, c))
        x_c = None
        for sec in order:
            acc = jnp.dot(h, w_bf16_ref[sec], preferred_element_type=F32)
            if sec == W_XC:
                x_c = acc
            else:
                store_section(sec, acc, step + 1, x_c)

    @pl.when(jnp.logical_and(step >= 0, step < SIDE_CAST_STEPS))
    def _():
        for src, dst in zip(side_in, side_out):
            dst[...] = src[...].astype(BF16)


def _inproj(x2, g_mix, w_in, rot, g_ret, side_weights, tm):
    t, d = x2.shape
    cos_base, sin_base, cos_off, sin_off = rot
    assert cos_off.shape[0] == tm and w_in.shape[1] == N_W_SEC * SEC
    n_out = N_OUT_SEC * SEC
    n_steps = t // tm
    assert n_steps >= SIDE_CAST_STEPS

    def tile(i):
        return jnp.maximum(i - (N_W_SEC - 1), 0)

    def slab_spec(w):
        rows = w.shape[0] // SIDE_CAST_STEPS
        assert rows * SIDE_CAST_STEPS == w.shape[0] and rows % BF16_SUBLANES == 0
        return pl.BlockSpec((rows, w.shape[1]),
                            lambda i: (jnp.clip(i - N_W_SEC, 0, SIDE_CAST_STEPS - 1), 0))

    side_specs = [slab_spec(w) for w in side_weights]
    outs = pl.pallas_call(
        functools.partial(_inproj_kernel, n_pos_blocks=cos_base.shape[0],
                          n_side=len(side_weights)),
        grid=(N_W_SEC + n_steps - 1,),
        in_specs=[
            pl.BlockSpec((tm, d), lambda i: (tile(i), 0)),
            _const_spec(g_mix.shape),
            pl.BlockSpec((d, SEC), lambda i: (0, jnp.minimum(i, N_W_SEC - 1))),
            _const_spec(cos_base.shape),
            _const_spec(sin_base.shape),
            _const_spec(cos_off.shape),
            _const_spec(sin_off.shape),
            _const_spec(g_ret.shape),
        ] + side_specs,
        out_specs=[
            pl.BlockSpec((tm, n_out), lambda i: (tile(i), 0)),
            pl.BlockSpec((RET_HEADS * QK_DIM, tm), lambda i: (0, tile(i))),
            pl.BlockSpec((RET_HEADS, tm, VQ_DIM), lambda i: (0, tile(i), 0)),
        ] + side_specs,
        out_shape=[
            jax.ShapeDtypeStruct((t, n_out), BF16),
            jax.ShapeDtypeStruct((RET_HEADS * QK_DIM, t), BF16),
            jax.ShapeDtypeStruct((RET_HEADS, t, VQ_DIM), BF16),
        ] + [jax.ShapeDtypeStruct(w.shape, BF16) for w in side_weights],
        scratch_shapes=[pltpu.VMEM((N_W_SEC, d, SEC), BF16),
                        pltpu.VMEM((tm, d), BF16),
                        pltpu.VMEM((tm, SEC), F32)],
        compiler_params=pltpu.CompilerParams(
            dimension_semantics=("arbitrary",),
            vmem_limit_bytes=VMEM_LIMIT),
        name="inproj",
    )(x2, g_mix, w_in, cos_base, sin_base, cos_off, sin_off, g_ret, *side_weights)
    return outs[0], outs[1], outs[2], outs[3:]


def _retention_kernel(lg_ref, vq_ref, ktf_ref, ktb_ref, vb_ref, z_ref,
                      decay_ref, xif_ref, xib_ref, zetaf_ref, zetab_ref,
                      fstate_ref, bstate_ref, rall_ref,
                      *, n_pairs, n_blocks, chunks_per_block):
    c = RET_CHUNK
    cpb = chunks_per_block
    n_chunks = n_blocks * cpb
    p = pl.program_id(0)
    j = pl.program_id(1)
    head_f = jnp.maximum(p - 1, 0) % RET_HEADS
    head_b = jnp.minimum(p, n_pairs - 1) % RET_HEADS
    lg_f = lg_ref[0, head_f]
    lg_fb = lg_ref[1, head_f]
    lg_b = lg_ref[1, head_b]
    has_fwd = p >= 1
    has_bwd = p < n_pairs

    @pl.when(jnp.logical_and(j == 0, has_fwd))
    def _():
        row = lax.broadcasted_iota(jnp.int32, (c, c), 0).astype(F32)
        col = lax.broadcasted_iota(jnp.int32, (c, c), 1).astype(F32)
        diff = row - col
        decay_ref[...] = jnp.where(diff >= 0.0,
                                   jnp.exp(jnp.maximum(diff, 0.0) * lg_f),
                                   jnp.exp(jnp.maximum(-diff, 0.0) * lg_fb))
        zetaf_ref[...] = jnp.exp((c - 1.0 - col) * lg_f).astype(BF16)
        idx = lax.broadcasted_iota(jnp.int32, (c, LANES), 0).astype(F32)
        xif_ref[...] = jnp.exp((idx + 1.0) * lg_f).astype(BF16)
        xib_ref[...] = jnp.exp((c - idx) * lg_fb).astype(BF16)
        fstate_ref[...] = jnp.zeros_like(fstate_ref)

    @pl.when(jnp.logical_and(j == 0, has_bwd))
    def _():
        col = lax.broadcasted_iota(jnp.int32, (c, c), 1).astype(F32)
        zetab_ref[...] = jnp.exp(col * lg_b).astype(BF16)
        bstate_ref[...] = jnp.zeros_like(bstate_ref)

    def row_scaled(a, scale_ref):
        scale = scale_ref[...]
        return jnp.concatenate(
            [a[:, n * LANES:(n + 1) * LANES] * scale
             for n in range(a.shape[1] // LANES)], axis=1)

    def decayed_kv(kt_ref, v_ref, zeta_ref, ci):
        rows = slice(ci * c, (ci + 1) * c)
        return jnp.dot(kt_ref[:, rows] * zeta_ref[...], v_ref[rows, :V_DIM],
                       preferred_element_type=F32)

    def bwd_sweep():
        blk = n_blocks - 1 - j
        slot = (p % 2) * n_chunks
        chunk_decay = jnp.exp(jnp.zeros((1, V_DIM), F32) + c * lg_b)
        state = bstate_ref[...]
        for ci in reversed(range(cpb)):
            rall_ref[slot + blk * cpb + ci] = state.astype(BF16)
            state = chunk_decay * state + decayed_kv(ktb_ref, vb_ref, zetab_ref, ci)
        bstate_ref[...] = state

    def fwd_sweep():
        slot = ((p - 1) % 2) * n_chunks
        chunk_decay = jnp.exp(jnp.zeros((1, V_DIM), F32) + c * lg_f)

        def decayed_scores(ci):
            rows = slice(ci * c, (ci + 1) * c)
            scores = jnp.dot(vq_ref[rows, V_DIM:], ktf_ref[:, rows],
                             preferred_element_type=F32)
            return (scores * decay_ref[...]).astype(BF16)

        state = fstate_ref[...]
        pmat_next = decayed_scores(0)
        kv_next = decayed_kv(ktf_ref, vq_ref, zetaf_ref, 0)
        for ci in range(cpb):
            rows = slice(ci * c, (ci + 1) * c)
            pmat, kv = pmat_next, kv_next
            if ci + 1 < cpb:
                pmat_next = decayed_scores(ci + 1)
                kv_next = decayed_kv(ktf_ref, vq_ref, zetaf_ref, ci + 1)
            q = vq_ref[rows, V_DIM:]
            lhs = jnp.concatenate([pmat, row_scaled(q, xib_ref), row_scaled(q, xif_ref)],
                                  axis=1)
            rhs = jnp.concatenate([vq_ref[rows, :V_DIM], rall_ref[slot + j * cpb + ci],
                                   state.astype(BF16)], axis=0)
            ret = jnp.dot(lhs, rhs, preferred_element_type=F32)
            mu = jnp.mean(ret, axis=-1, keepdims=True)
            var = jnp.mean(jnp.square(ret - mu), axis=-1, keepdims=True)
            rstd = lax.rsqrt(var + EPS)
            z_ref[rows, :] = (ret * rstd - mu * rstd).astype(BF16)
            state = chunk_decay * state + kv
        fstate_ref[...] = state

    @pl.when(jnp.logical_not(has_fwd))
    def _():
        bwd_sweep()

    @pl.when(jnp.logical_and(has_fwd, has_bwd))
    def _():
        fwd_sweep()
        bwd_sweep()

    @pl.when(jnp.logical_not(has_bwd))
    def _():
        fwd_sweep()


def _retention(lg, vq, kt, batch, seq, block_tokens):
    c = RET_CHUNK
    nb = seq // block_tokens
    cpb = block_tokens // c
    nc = seq // c
    t = batch * seq
    n_pairs = batch * RET_HEADS

    def fwd_idx(p, j):
        pair = jnp.maximum(p - 1, 0)
        return pair // RET_HEADS, pair % RET_HEADS, jnp.where(p == 0, 0, j)

    def bwd_idx(p, j):
        pair = jnp.minimum(p, n_pairs - 1)
        return pair // RET_HEADS, pair % RET_HEADS, jnp.where(p == n_pairs, 0, nb - 1 - j)

    def head_rows(idx_fn):
        def index_map(p, j):
            b, h, blk = idx_fn(p, j)
            return h, b * nb + blk, 0
        return index_map

    def head_major(idx_fn):
        def index_map(p, j):
            b, h, blk = idx_fn(p, j)
            return h, b * nb + blk
        return index_map

    return pl.pallas_call(
        functools.partial(_retention_kernel, n_pairs=n_pairs, n_blocks=nb,
                          chunks_per_block=cpb),
        grid=(n_pairs + 1, nb),
        in_specs=[
            pl.BlockSpec(memory_space=pltpu.SMEM),
            pl.BlockSpec((None, block_tokens, VQ_DIM), head_rows(fwd_idx)),
            pl.BlockSpec((QK_DIM, block_tokens), head_major(fwd_idx)),
            pl.BlockSpec((QK_DIM, block_tokens), head_major(bwd_idx)),
            pl.BlockSpec((None, block_tokens, V_DIM), head_rows(bwd_idx)),
        ],
        out_specs=pl.BlockSpec((None, block_tokens, V_DIM), head_rows(fwd_idx)),
        out_shape=jax.ShapeDtypeStruct((RET_HEADS, t, V_DIM), BF16),
        scratch_shapes=[
            pltpu.VMEM((c, c), F32),
            pltpu.VMEM((c, LANES), BF16),
            pltpu.VMEM((c, LANES), BF16),
            pltpu.VMEM((QK_DIM, c), BF16),
            pltpu.VMEM((QK_DIM, c), BF16),
            pltpu.VMEM((QK_DIM, V_DIM), F32),
            pltpu.VMEM((QK_DIM, V_DIM), F32),
            pltpu.VMEM((2 * nc, QK_DIM, V_DIM), BF16),
        ],
        compiler_params=pltpu.CompilerParams(
            dimension_semantics=("arbitrary", "arbitrary"),
            vmem_limit_bytes=VMEM_LIMIT),
        name="retention",
    )(lg, vq, kt, kt, vq)


def _merge_kernel(u_ref, gb_ref, up_ref, un_ref, ga_ref, gr_ref, z_ref, gsw_ref,
                  wconv_ref, wa_ref, wr_ref, o_ref, *, tiles_per_seq):
    i = pl.program_id(0)
    tm = u_ref.shape[0]
    u = u_ref[...].astype(F32)
    pos = i % tiles_per_seq
    last = BF16_SUBLANES - 1
    u_before = up_ref[last:last + 1, :].astype(F32) * (pos != 0).astype(F32)
    u_after = un_ref[0:1, :].astype(F32) * (pos != tiles_per_seq - 1).astype(F32)
    row = lax.broadcasted_iota(jnp.int32, u.shape, 0)
    u_prev = jnp.where(row == 0, u_before, pltpu.roll(u, 1, axis=0))
    u_next = jnp.where(row == tm - 1, u_after, pltpu.roll(u, tm - 1, axis=0))
    conv = u_prev * wconv_ref[0:1, :] + u * wconv_ref[1:2, :] + u_next * wconv_ref[2:3, :]
    a_in = (gb_ref[...].astype(F32) * conv).astype(BF16)

    def branch_outputs(rows):
        z = jnp.concatenate([z_ref[hh, rows, :] for hh in range(RET_HEADS)], axis=1)
        y_r = jnp.dot(z * gsw_ref[rows, :], wr_ref[...], preferred_element_type=F32)
        y_a = jnp.dot(a_in[rows, :], wa_ref[...], preferred_element_type=F32)
        return y_a, y_r

    blocks = [slice(r, r + SUBBLOCK_ROWS) for r in range(0, tm, SUBBLOCK_ROWS)]
    y_next = branch_outputs(blocks[0])
    for s, rows in enumerate(blocks):
        y_a, y_r = y_next
        if s + 1 < len(blocks):
            y_next = branch_outputs(blocks[s + 1])
        merged = ga_ref[rows, :].astype(F32) * y_a + gr_ref[rows, :].astype(F32) * y_r
        o_ref[rows, :] = merged.astype(BF16)


def _merge(proj, z, w_conv, w_a, w_r, seq, tm):
    t, d = proj.shape[0], D_MODEL
    hb = tm // BF16_SUBLANES
    n_halo_blocks = t // BF16_SUBLANES
    halo = (BF16_SUBLANES, SEC)
    return pl.pallas_call(
        functools.partial(_merge_kernel, tiles_per_seq=seq // tm),
        grid=(t // tm,),
        in_specs=[
            pl.BlockSpec((tm, SEC), lambda i: (i, SEC_U)),
            pl.BlockSpec((tm, SEC), lambda i: (i, SEC_GB)),
            pl.BlockSpec(halo, lambda i: (jnp.maximum(i * hb - 1, 0), SEC_U)),
            pl.BlockSpec(halo, lambda i: (jnp.minimum((i + 1) * hb, n_halo_blocks - 1), SEC_U)),
            pl.BlockSpec((tm, SEC), lambda i: (i, SEC_GA)),
            pl.BlockSpec((tm, SEC), lambda i: (i, SEC_GR)),
            pl.BlockSpec((RET_HEADS, tm, V_DIM), lambda i: (0, i, 0)),
            pl.BlockSpec((tm, RET_V), lambda i: (i, (SEC_GSW * SEC) // RET_V)),
            _const_spec(w_conv.shape),
            _const_spec(w_a.shape),
            _const_spec(w_r.shape),
        ],
        out_specs=pl.BlockSpec((tm, d), lambda i: (i, 0)),
        out_shape=jax.ShapeDtypeStruct((t, d), BF16),
        compiler_params=pltpu.CompilerParams(
            dimension_semantics=("arbitrary",),
            vmem_limit_bytes=VMEM_LIMIT),
        name="merge",
    )(proj, proj, proj, proj, proj, proj, z, proj, w_conv, w_a, w_r)


def _ffn_kernel(x_ref, m_ref, wo_ref, gffn_ref, wg_ref, wu_ref, wd_ref, gfin_ref, o_ref,
                *, final_norm):
    def residual(rows):
        return x_ref[rows, :] + jnp.dot(m_ref[rows, :], wo_ref[...], preferred_element_type=F32)

    def gate_up(x1):
        h2 = (_rms_scale(x1) * gffn_ref[...]).astype(BF16)
        return (jnp.dot(h2, wg_ref[...], preferred_element_type=F32),
                jnp.dot(h2, wu_ref[...], preferred_element_type=F32), x1)

    blocks = [slice(r, r + SUBBLOCK_ROWS) for r in range(0, x_ref.shape[0], SUBBLOCK_ROWS)]
    x1s = [residual(rows) for rows in blocks[:2]]
    gu_next = gate_up(x1s[0])
    for s, rows in enumerate(blocks):
        gate, up, x1 = gu_next
        if s + 2 < len(blocks):
            x1s.append(residual(blocks[s + 2]))
        if s + 1 < len(blocks):
            gu_next = gate_up(x1s[s + 1])
        act = (gate * _sigmoid(gate) * up).astype(BF16)
        y = x1 + jnp.dot(act, wd_ref[...], preferred_element_type=F32)
        o_ref[rows, :] = _rms_scale(y) * gfin_ref[...] if final_norm else y


def _ffn(x2, merged, w_o, g_ffn, w_gate, w_up, w_down, g_final, final_norm, tm):
    t, d = x2.shape
    return pl.pallas_call(
        functools.partial(_ffn_kernel, final_norm=final_norm),
        grid=(t // tm,),
        in_specs=[
            pl.BlockSpec((tm, d), lambda i: (i, 0)),
            pl.BlockSpec((tm, d), lambda i: (i, 0)),
            _const_spec(w_o.shape),
            _const_spec(g_ffn.shape),
            _const_spec(w_gate.shape),
            _const_spec(w_up.shape),
            _const_spec(w_down.shape),
            _const_spec(g_final.shape),
        ],
        out_specs=pl.BlockSpec((tm, d), lambda i: (i, 0)),
        out_shape=jax.ShapeDtypeStruct((t, d), F32),
        compiler_params=pltpu.CompilerParams(
            dimension_semantics=("arbitrary",),
            vmem_limit_bytes=VMEM_LIMIT),
        name="ffn",
    )(x2, merged, w_o, g_ffn, w_gate, w_up, w_down, g_final)


def _rotary_tables(seq, tm):
    freqs = ROPE_BASE ** (-jnp.arange(0, QK_DIM, 2, dtype=F32) / QK_DIM)
    base = (jnp.arange(seq // tm, dtype=F32) * tm)[:, None] * freqs[None, :]
    off = jnp.arange(tm, dtype=F32)[:, None] * freqs[None, :]
    return jnp.cos(base), jnp.sin(base), jnp.cos(off), jnp.sin(off)


def kernel(x, g_mix, w_in, w_conv, dec_f, dec_b, g_ret, w_a_out, w_r_out, w_o,
           g_ffn, w_ff_gate, w_ff_up, w_ff_down, g_final):
    batch, seq, d = x.shape
    depth = w_in.shape[0]
    assert d == D_MODEL and seq % RET_BLOCK_TOKENS == 0 and seq % MERGE_ROWS == 0
    x2 = x.reshape(batch * seq, d)
    rot = _rotary_tables(seq, INPROJ_ROWS)
    for l in range(depth):
        lg = jnp.stack([jax.nn.log_sigmoid(dec_f[l].astype(F32)),
                        jax.nn.log_sigmoid(dec_b[l].astype(F32))])
        side = [w_a_out[l], w_r_out[l], w_o[l], w_ff_gate[l], w_ff_up[l], w_ff_down[l]]
        proj, kt, vq, (w_a, w_r, w_ob, w_gate, w_up, w_down) = _inproj(
            x2, g_mix[l][None, :], w_in[l], rot,
            g_ret[l][None, :].astype(F32), side, tm=INPROJ_ROWS)
        z = _retention(lg, vq, kt, batch, seq, block_tokens=RET_BLOCK_TOKENS)
        merged = _merge(proj, z, w_conv[l], w_a, w_r, seq, tm=MERGE_ROWS)
        x2 = _ffn(x2, merged, w_ob, g_ffn[l][None, :], w_gate, w_up, w_down,
                  g_final[None, :], final_norm=(l == depth - 1), tm=FFN_ROWS)
    return x2.reshape(batch, seq, d)
```

```python
import functools

import jax
import jax.numpy as jnp
from jax import lax
from jax.experimental import pallas as pl
from jax.experimental.pallas import tpu as pltpu

F32 = jnp.float32
BF16 = jnp.bfloat16

D_MODEL = 1024
RET_HEADS = 4
QK_DIM = D_MODEL // RET_HEADS
V_DIM = 2 * QK_DIM
RET_V = RET_HEADS * V_DIM
ROPE_BASE = 10000.0
EPS = 1e-6

SEC = 1024
W_XC, W_GB, W_GC, W_Q, W_K, W_V, W_GSW, W_GA, W_GR, N_W_SEC = 0, 1, 2, 3, 4, 5, 7, 9, 10, 11
SEC_U, SEC_GB, SEC_GSW, SEC_GA, SEC_GR, N_OUT_SEC = 0, 1, 2, 4, 5, 6
VQ_DIM = V_DIM + QK_DIM

RET_CHUNK = 256
SIDE_CAST_STEPS = 16
SUBBLOCK_ROWS = 256
INPROJ_ROWS = 256
RET_BLOCK_TOKENS = 4096
MERGE_ROWS = 512
FFN_ROWS = 1024
LANES = 128
BF16_SUBLANES = 16

VMEM_LIMIT = 56 * 1024 * 1024


def _sigmoid(x):
    return 1.0 / (1.0 + jnp.exp(-x))


def _rms_scale(x):
    return x * lax.rsqrt(jnp.mean(x * x, axis=-1, keepdims=True) + EPS)


def _const_spec(shape):
    return pl.BlockSpec(shape, lambda *_: (0,) * len(shape), pipeline_mode=pl.Buffered(1))


def _inproj_kernel(x_ref, g_ref, w_ref, cos_base_ref, sin_base_ref, cos_off_ref, sin_off_ref,
                   gret_ref, *rest, n_pos_blocks, n_side):
    side_in = rest[:n_side]
    o_ref, kt_ref, vq_ref = rest[n_side:n_side + 3]
    side_out = rest[n_side + 3:2 * n_side + 3]
    w_bf16_ref = rest[2 * n_side + 3]
    step = pl.program_id(0) - N_W_SEC

    @pl.when(step < 0)
    def _():
        w_bf16_ref[pl.program_id(0)] = w_ref[...].astype(BF16)

    @pl.when(step >= 0)
    def _():
        h = (_rms_scale(x_ref[...]) * g_ref[...]).astype(BF16)
        half = QK_DIM // 2
        pos_block = pl.ds(step % n_pos_blocks, 1)
        cos_b = cos_base_ref[pos_block, :]
        sin_b = sin_base_ref[pos_block, :]
        cos = cos_b * cos_off_ref[...] - sin_b * sin_off_ref[...]
        sin = sin_b * cos_off_ref[...] + cos_b * sin_off_ref[...]

        def project(first_sec, n_sec):
            return jnp.concatenate(
                [jnp.dot(h, w_bf16_ref[sec], preferred_element_type=F32)
                 for sec in range(first_sec, first_sec + n_sec)], axis=1)

        def out_cols(sec, n_sec=1):
            return slice(sec * SEC, (sec + n_sec) * SEC)

        qk = project(W_Q, 2)
        for sec in range(2):
            for hh in range(RET_HEADS):
                lo = sec * SEC + hh * QK_DIM
                t1 = qk[:, lo:lo + half]
                t2 = qk[:, lo + half:lo + QK_DIM]
                r1 = t1 * cos - t2 * sin
                r2 = t1 * sin + t2 * cos
                if sec == 0:
                    scale = QK_DIM ** -0.5
                    vq_ref[hh, :, V_DIM:V_DIM + half] = (r1 * scale).astype(BF16)
                    vq_ref[hh, :, V_DIM + half:VQ_DIM] = (r2 * scale).astype(BF16)
                else:
                    out = hh * QK_DIM
                    kt_ref[out:out + half, :] = r1.T.astype(BF16)
                    kt_ref[out + half:out + QK_DIM, :] = r2.T.astype(BF16)

        gsw = project(W_GSW, 2)
        o_ref[:, out_cols(SEC_GSW, 2)] = (gsw * _sigmoid(gsw) * gret_ref[...]).astype(BF16)

        gates = project(W_GA, 2)
        o_ref[:, out_cols(SEC_GA, 2)] = _sigmoid(gates).astype(BF16)

        conv_in = project(W_XC, 3)
        o_ref[:, out_cols(SEC_U)] = (conv_in[:, out_cols(W_GC)] * conv_in[:, out_cols(W_XC)]
                                     ).astype(BF16)
        o_ref[:, out_cols(SEC_GB)] = conv_in[:, out_cols(W_GB)].astype(BF16)

        values = project(W_V, 2)
        for hh in range(RET_HEADS):
            vq_ref[hh, :, :V_DIM] = values[:, hh * V_DIM:(hh + 1) * V_DIM].astype(BF16)

    @pl.when(jnp.logical_and(step >= 0, step < SIDE_CAST_STEPS))
    def _():
        for src, dst in zip(side_in, side_out):
            dst[...] = src[...].astype(BF16)


def _inproj(x2, g_mix, w_in, rot, g_ret, side_weights, tm):
    t, d = x2.shape
    cos_base, sin_base, cos_off, sin_off = rot
    assert cos_off.shape[0] == tm and w_in.shape[1] == N_W_SEC * SEC
    n_out = N_OUT_SEC * SEC
    n_steps = t // tm
    assert n_steps >= SIDE_CAST_STEPS

    def tile(i):
        return jnp.maximum(i - N_W_SEC, 0)

    def slab_spec(w):
        rows = w.shape[0] // SIDE_CAST_STEPS
        assert rows * SIDE_CAST_STEPS == w.shape[0] and rows % BF16_SUBLANES == 0
        return pl.BlockSpec((rows, w.shape[1]),
                            lambda i: (jnp.minimum(tile(i), SIDE_CAST_STEPS - 1), 0))

    side_specs = [slab_spec(w) for w in side_weights]
    outs = pl.pallas_call(
        functools.partial(_inproj_kernel, n_pos_blocks=cos_base.shape[0],
                          n_side=len(side_weights)),
        grid=(N_W_SEC + n_steps,),
        in_specs=[
            pl.BlockSpec((tm, d), lambda i: (tile(i), 0)),
            _const_spec(g_mix.shape),
            pl.BlockSpec((d, SEC), lambda i: (0, jnp.minimum(i, N_W_SEC - 1))),
            _const_spec(cos_base.shape),
            _const_spec(sin_base.shape),
            _const_spec(cos_off.shape),
            _const_spec(sin_off.shape),
            _const_spec(g_ret.shape),
        ] + side_specs,
        out_specs=[
            pl.BlockSpec((tm, n_out), lambda i: (tile(i), 0)),
            pl.BlockSpec((RET_HEADS * QK_DIM, tm), lambda i: (0, tile(i))),
            pl.BlockSpec((RET_HEADS, tm, VQ_DIM), lambda i: (0, tile(i), 0)),
        ] + side_specs,
        out_shape=[
            jax.ShapeDtypeStruct((t, n_out), BF16),
            jax.ShapeDtypeStruct((RET_HEADS * QK_DIM, t), BF16),
            jax.ShapeDtypeStruct((RET_HEADS, t, VQ_DIM), BF16),
        ] + [jax.ShapeDtypeStruct(w.shape, BF16) for w in side_weights],
        scratch_shapes=[pltpu.VMEM((N_W_SEC, d, SEC), BF16)],
        compiler_params=pltpu.CompilerParams(
            dimension_semantics=("arbitrary",),
            vmem_limit_bytes=VMEM_LIMIT),
        name="inproj",
    )(x2, g_mix, w_in, cos_base, sin_base, cos_off, sin_off, g_ret, *side_weights)
    return outs[0], outs[1], outs[2], outs[3:]


def _retention_kernel(lg_ref, vq_ref, ktf_ref, ktb_ref, vb_ref, z_ref,
                      decay_ref, xif_ref, xib_ref, zetaf_ref, zetab_ref,
                      fstate_ref, bstate_ref, rall_ref,
                      *, n_pairs, n_blocks, chunks_per_block):
    c = RET_CHUNK
    cpb = chunks_per_block
    n_chunks = n_blocks * cpb
    p = pl.program_id(0)
    j = pl.program_id(1)
    head_f = jnp.maximum(p - 1, 0) % RET_HEADS
    head_b = jnp.minimum(p, n_pairs - 1) % RET_HEADS
    lg_f = lg_ref[0, head_f]
    lg_fb = lg_ref[1, head_f]
    lg_b = lg_ref[1, head_b]
    has_fwd = p >= 1
    has_bwd = p < n_pairs

    @pl.when(jnp.logical_and(j == 0, has_fwd))
    def _():
        row = lax.broadcasted_iota(jnp.int32, (c, c), 0).astype(F32)
        col = lax.broadcasted_iota(jnp.int32, (c, c), 1).astype(F32)
        diff = row - col
        decay_ref[...] = jnp.where(diff >= 0.0,
                                   jnp.exp(jnp.maximum(diff, 0.0) * lg_f),
                                   jnp.exp(jnp.maximum(-diff, 0.0) * lg_fb))
        zetaf_ref[...] = jnp.exp((c - 1.0 - col) * lg_f).astype(BF16)
        idx = lax.broadcasted_iota(jnp.int32, (c, LANES), 0).astype(F32)
        xif_ref[...] = jnp.exp((idx + 1.0) * lg_f).astype(BF16)
        xib_ref[...] = jnp.exp((c - idx) * lg_fb).astype(BF16)
        fstate_ref[...] = jnp.zeros_like(fstate_ref)

    @pl.when(jnp.logical_and(j == 0, has_bwd))
    def _():
        col = lax.broadcasted_iota(jnp.int32, (c, c), 1).astype(F32)
        zetab_ref[...] = jnp.exp(col * lg_b).astype(BF16)
        bstate_ref[...] = jnp.zeros_like(bstate_ref)

    def row_scaled(a, scale_ref):
        scale = scale_ref[...]
        return jnp.concatenate(
            [a[:, n * LANES:(n + 1) * LANES] * scale
             for n in range(a.shape[1] // LANES)], axis=1)

    def decayed_kv(kt_ref, v_ref, zeta_ref, ci):
        rows = slice(ci * c, (ci + 1) * c)
        return jnp.dot(kt_ref[:, rows] * zeta_ref[...], v_ref[rows, :V_DIM],
                       preferred_element_type=F32)

    def bwd_sweep():
        blk = n_blocks - 1 - j
        slot = (p % 2) * n_chunks
        chunk_decay = jnp.exp(jnp.zeros((1, V_DIM), F32) + c * lg_b)
        state = bstate_ref[...]
        for ci in reversed(range(cpb)):
            rall_ref[slot + blk * cpb + ci] = state.astype(BF16)
            state = chunk_decay * state + decayed_kv(ktb_ref, vb_ref, zetab_ref, ci)
        bstate_ref[...] = state

    def fwd_sweep():
        slot = ((p - 1) % 2) * n_chunks
        chunk_decay = jnp.exp(jnp.zeros((1, V_DIM), F32) + c * lg_f)

        def decayed_scores(ci):
            rows = slice(ci * c, (ci + 1) * c)
            scores = jnp.dot(vq_ref[rows, V_DIM:], ktf_ref[:, rows],
                             preferred_element_type=F32)
            return (scores * decay_ref[...]).astype(BF16)

        state = fstate_ref[...]
        pmat_next = decayed_scores(0)
        kv_next = decayed_kv(ktf_ref, vq_ref, zetaf_ref, 0)
        for ci in range(cpb):
            rows = slice(ci * c, (ci + 1) * c)
            pmat, kv = pmat_next, kv_next
            if ci + 1 < cpb:
                pmat_next = decayed_scores(ci + 1)
                kv_next = decayed_kv(ktf_ref, vq_ref, zetaf_ref, ci + 1)
            q = vq_ref[rows, V_DIM:]
            lhs = jnp.concatenate([pmat, row_scaled(q, xib_ref), row_scaled(q, xif_ref)],
                                  axis=1)
            rhs = jnp.concatenate([vq_ref[rows, :V_DIM], rall_ref[slot + j * cpb + ci],
                                   state.astype(BF16)], axis=0)
            ret = jnp.dot(lhs, rhs, preferred_element_type=F32)
            mu = jnp.mean(ret, axis=-1, keepdims=True)
            var = jnp.mean(jnp.square(ret - mu), axis=-1, keepdims=True)
            rstd = lax.rsqrt(var + EPS)
            z_ref[rows, :] = (ret * rstd - mu * rstd).astype(BF16)
            state = chunk_decay * state + kv
        fstate_ref[...] = state

    @pl.when(jnp.logical_not(has_fwd))
    def _():
        bwd_sweep()

    @pl.when(jnp.logical_and(has_fwd, has_bwd))
    def _():
        fwd_sweep()
        bwd_sweep()

    @pl.when(jnp.logical_not(has_bwd))
    def _():
        fwd_sweep()


def _retention(lg, vq, kt, batch, seq, block_tokens):
    c = RET_CHUNK
    nb = seq // block_tokens
    cpb = block_tokens // c
    nc = seq // c
    t = batch * seq
    n_pairs = batch * RET_HEADS

    def fwd_idx(p, j):
        pair = jnp.maximum(p - 1, 0)
        return pair // RET_HEADS, pair % RET_HEADS, jnp.where(p == 0, 0, j)

    def bwd_idx(p, j):
        pair = jnp.minimum(p, n_pairs - 1)
        return pair // RET_HEADS, pair % RET_HEADS, jnp.where(p == n_pairs, 0, nb - 1 - j)

    def head_rows(idx_fn):
        def index_map(p, j):
            b, h, blk = idx_fn(p, j)
            return h, b * nb + blk, 0
        return index_map

    def head_major(idx_fn):
        def index_map(p, j):
            b, h, blk = idx_fn(p, j)
            return h, b * nb + blk
        return index_map

    return pl.pallas_call(
        functools.partial(_retention_kernel, n_pairs=n_pairs, n_blocks=nb,
                          chunks_per_block=cpb),
        grid=(n_pairs + 1, nb),
        in_specs=[
            pl.BlockSpec(memory_space=pltpu.SMEM),
            pl.BlockSpec((None, block_tokens, VQ_DIM), head_rows(fwd_idx)),
            pl.BlockSpec((QK_DIM, block_tokens), head_major(fwd_idx)),
            pl.BlockSpec((QK_DIM, block_tokens), head_major(bwd_idx)),
            pl.BlockSpec((None, block_tokens, V_DIM), head_rows(bwd_idx)),
        ],
        out_specs=pl.BlockSpec((None, block_tokens, V_DIM), head_rows(fwd_idx)),
        out_shape=jax.ShapeDtypeStruct((RET_HEADS, t, V_DIM), BF16),
        scratch_shapes=[
            pltpu.VMEM((c, c), F32),
            pltpu.VMEM((c, LANES), BF16),
            pltpu.VMEM((c, LANES), BF16),
            pltpu.VMEM((QK_DIM, c), BF16),
            pltpu.VMEM((QK_DIM, c), BF16),
            pltpu.VMEM((QK_DIM, V_DIM), F32),
            pltpu.VMEM((QK_DIM, V_DIM), F32),
            pltpu.VMEM((2 * nc, QK_DIM, V_DIM), BF16),
        ],
        compiler_params=pltpu.CompilerParams(
            dimension_semantics=("arbitrary", "arbitrary"),
            vmem_limit_bytes=VMEM_LIMIT),
        name="retention",
    )(lg, vq, kt, kt, vq)


def _merge_kernel(u_ref, gb_ref, up_ref, un_ref, ga_ref, gr_ref, z_ref, gsw_ref,
                  wconv_ref, wa_ref, wr_ref, o_ref, *, tiles_per_seq):
    i = pl.program_id(0)
    tm = u_ref.shape[0]
    u = u_ref[...].astype(F32)
    pos = i % tiles_per_seq
    last = BF16_SUBLANES - 1
    u_before = up_ref[last:last + 1, :].astype(F32) * (pos != 0).astype(F32)
    u_after = un_ref[0:1, :].astype(F32) * (pos != tiles_per_seq - 1).astype(F32)
    row = lax.broadcasted_iota(jnp.int32, u.shape, 0)
    u_prev = jnp.where(row == 0, u_before, pltpu.roll(u, 1, axis=0))
    u_next = jnp.where(row == tm - 1, u_after, pltpu.roll(u, tm - 1, axis=0))
    conv = u_prev * wconv_ref[0:1, :] + u * wconv_ref[1:2, :] + u_next * wconv_ref[2:3, :]
    a_in = (gb_ref[...].astype(F32) * conv).astype(BF16)

    def branch_outputs(rows):
        z = jnp.concatenate([z_ref[hh, rows, :] for hh in range(RET_HEADS)], axis=1)
        y_r = jnp.dot(z * gsw_ref[rows, :], wr_ref[...], preferred_element_type=F32)
        y_a = jnp.dot(a_in[rows, :], wa_ref[...], preferred_element_type=F32)
        return y_a, y_r

    blocks = [slice(r, r + SUBBLOCK_ROWS) for r in range(0, tm, SUBBLOCK_ROWS)]
    y_next = branch_outputs(blocks[0])
    for s, rows in enumerate(blocks):
        y_a, y_r = y_next
        if s + 1 < len(blocks):
            y_next = branch_outputs(blocks[s + 1])
        merged = ga_ref[rows, :].astype(F32) * y_a + gr_ref[rows, :].astype(F32) * y_r
        o_ref[rows, :] = merged.astype(BF16)


def _merge(proj, z, w_conv, w_a, w_r, seq, tm):
    t, d = proj.shape[0], D_MODEL
    hb = tm // BF16_SUBLANES
    n_halo_blocks = t // BF16_SUBLANES
    halo = (BF16_SUBLANES, SEC)
    return pl.pallas_call(
        functools.partial(_merge_kernel, tiles_per_seq=seq // tm),
        grid=(t // tm,),
        in_specs=[
            pl.BlockSpec((tm, SEC), lambda i: (i, SEC_U)),
            pl.BlockSpec((tm, SEC), lambda i: (i, SEC_GB)),
            pl.BlockSpec(halo, lambda i: (jnp.maximum(i * hb - 1, 0), SEC_U)),
            pl.BlockSpec(halo, lambda i: (jnp.minimum((i + 1) * hb, n_halo_blocks - 1), SEC_U)),
            pl.BlockSpec((tm, SEC), lambda i: (i, SEC_GA)),
            pl.BlockSpec((tm, SEC), lambda i: (i, SEC_GR)),
            pl.BlockSpec((RET_HEADS, tm, V_DIM), lambda i: (0, i, 0)),
            pl.BlockSpec((tm, RET_V), lambda i: (i, (SEC_GSW * SEC) // RET_V)),
            _const_spec(w_conv.shape),
            _const_spec(w_a.shape),
            _const_spec(w_r.shape),
        ],
        out_specs=pl.BlockSpec((tm, d), lambda i: (i, 0)),
        out_shape=jax.ShapeDtypeStruct((t, d), BF16),
        compiler_params=pltpu.CompilerParams(
            dimension_semantics=("arbitrary",),
            vmem_limit_bytes=VMEM_LIMIT),
        name="merge",
    )(proj, proj, proj, proj, proj, proj, z, proj, w_conv, w_a, w_r)


def _ffn_kernel(x_ref, m_ref, wo_ref, gffn_ref, wg_ref, wu_ref, wd_ref, gfin_ref, o_ref,
                *, final_norm):
    def residual(rows):
        return x_ref[rows, :] + jnp.dot(m_ref[rows, :], wo_ref[...], preferred_element_type=F32)

    def gate_up(x1):
        h2 = (_rms_scale(x1) * gffn_ref[...]).astype(BF16)
        return (jnp.dot(h2, wg_ref[...], preferred_element_type=F32),
                jnp.dot(h2, wu_ref[...], preferred_element_type=F32), x1)

    blocks = [slice(r, r + SUBBLOCK_ROWS) for r in range(0, x_ref.shape[0], SUBBLOCK_ROWS)]
    x1s = [residual(rows) for rows in blocks[:2]]
    gu_next = gate_up(x1s[0])
    for s, rows in enumerate(blocks):
        gate, up, x1 = gu_next
        if s + 2 < len(blocks):
            x1s.append(residual(blocks[s + 2]))
        if s + 1 < len(blocks):
            gu_next = gate_up(x1s[s + 1])
        act = (gate * _sigmoid(gate) * up).astype(BF16)
        y = x1 + jnp.dot(act, wd_ref[...], preferred_element_type=F32)
        o_ref[rows, :] = _rms_scale(y) * gfin_ref[...] if final_norm else y


def _ffn(x2, merged, w_o, g_ffn, w_gate, w_up, w_down, g_final, final_norm, tm):
    t, d = x2.shape
    return pl.pallas_call(
        functools.partial(_ffn_kernel, final_norm=final_norm),
        grid=(t // tm,),
        in_specs=[
            pl.BlockSpec((tm, d), lambda i: (i, 0)),
            pl.BlockSpec((tm, d), lambda i: (i, 0)),
            _const_spec(w_o.shape),
            _const_spec(g_ffn.shape),
            _const_spec(w_gate.shape),
            _const_spec(w_up.shape),
            _const_spec(w_down.shape),
            _const_spec(g_final.shape),
        ],
        out_specs=pl.BlockSpec((tm, d), lambda i: (i, 0)),
        out_shape=jax.ShapeDtypeStruct((t, d), F32),
        compiler_params=pltpu.CompilerParams(
            dimension_semantics=("arbitrary",),
            vmem_limit_bytes=VMEM_LIMIT),
        name="ffn",
    )(x2, merged, w_o, g_ffn, w_gate, w_up, w_down, g_final)


def _rotary_tables(seq, tm):
    freqs = ROPE_BASE ** (-jnp.arange(0, QK_DIM, 2, dtype=F32) / QK_DIM)
    base = (jnp.arange(seq // tm, dtype=F32) * tm)[:, None] * freqs[None, :]
    off = jnp.arange(tm, dtype=F32)[:, None] * freqs[None, :]
    return jnp.cos(base), jnp.sin(base), jnp.cos(off), jnp.sin(off)


def kernel(x, g_mix, w_in, w_conv, dec_f, dec_b, g_ret, w_a_out, w_r_out, w_o,
           g_ffn, w_ff_gate, w_ff_up, w_ff_down, g_final):
    batch, seq, d = x.shape
    depth = w_in.shape[0]
    assert d == D_MODEL and seq % RET_BLOCK_TOKENS == 0 and seq % MERGE_ROWS == 0
    x2 = x.reshape(batch * seq, d)
    rot = _rotary_tables(seq, INPROJ_ROWS)
    for l in range(depth):
        lg = jnp.stack([jax.nn.log_sigmoid(dec_f[l].astype(F32)),
                        jax.nn.log_sigmoid(dec_b[l].astype(F32))])
        side = [w_a_out[l], w_r_out[l], w_o[l], w_ff_gate[l], w_ff_up[l], w_ff_down[l]]
        proj, kt, vq, (w_a, w_r, w_ob, w_gate, w_up, w_down) = _inproj(
            x2, g_mix[l][None, :], w_in[l], rot,
            g_ret[l][None, :].astype(F32), side, tm=INPROJ_ROWS)
        z = _retention(lg, vq, kt, batch, seq, block_tokens=RET_BLOCK_TOKENS)
        merged = _merge(proj, z, w_conv[l], w_a, w_r, seq, tm=MERGE_ROWS)
        x2 = _ffn(x2, merged, w_ob, g_ffn[l][None, :], w_gate, w_up, w_down,
                  g_final[None, :], final_norm=(l == depth - 1), tm=FFN_ROWS)
    return x2.reshape(batch, seq, d)
```

```python
import functools

import jax
import jax.numpy as jnp
from jax import lax
from jax.experimental import pallas as pl
from jax.experimental.pallas import tpu as pltpu

F32 = jnp.float32
BF16 = jnp.bfloat16

D_MODEL = 1024
RET_HEADS = 4
QK_DIM = D_MODEL // RET_HEADS
V_DIM = 2 * QK_DIM
RET_V = RET_HEADS * V_DIM
ROPE_BASE = 10000.0
EPS = 1e-6

SEC = 1024
W_XC, W_GB, W_GC, W_Q, W_K, W_V, W_GSW, W_GA, W_GR, N_W_SEC = 0, 1, 2, 3, 4, 5, 7, 9, 10, 11
SEC_U, SEC_GB, SEC_GSW, SEC_GA, SEC_GR, N_OUT_SEC = 0, 1, 2, 4, 5, 6
VQ_DIM = V_DIM + QK_DIM

RET_CHUNK = 256
SIDE_CAST_STEPS = 16
SUBBLOCK_ROWS = 256
INPROJ_ROWS = 256
RET_BLOCK_TOKENS = 4096
MERGE_ROWS = 1024
FFN_ROWS = 512
LANES = 128
BF16_SUBLANES = 16

VMEM_LIMIT = 56 * 1024 * 1024


def _sigmoid(x):
    return 1.0 / (1.0 + jnp.exp(-x))


def _rms_scale(x):
    return x * lax.rsqrt(jnp.mean(x * x, axis=-1, keepdims=True) + EPS)


def _const_spec(shape):
    return pl.BlockSpec(shape, lambda *_: (0,) * len(shape), pipeline_mode=pl.Buffered(1))


def _inproj_kernel(x_ref, g_ref, w_ref, cos_base_ref, sin_base_ref, cos_off_ref, sin_off_ref,
                   gret_ref, *rest, n_pos_blocks, n_side):
    side_in = rest[:n_side]
    o_ref, kt_ref, vq_ref = rest[n_side:n_side + 3]
    side_out = rest[n_side + 3:2 * n_side + 3]
    w_bf16_ref = rest[2 * n_side + 3]
    step = pl.program_id(0) - N_W_SEC

    @pl.when(step < 0)
    def _():
        w_bf16_ref[pl.program_id(0)] = w_ref[...].astype(BF16)

    @pl.when(step >= 0)
    def _():
        h = (_rms_scale(x_ref[...]) * g_ref[...]).astype(BF16)
        half = QK_DIM // 2
        pos_block = pl.ds(step % n_pos_blocks, 1)
        cos_b = cos_base_ref[pos_block, :]
        sin_b = sin_base_ref[pos_block, :]
        cos = cos_b * cos_off_ref[...] - sin_b * sin_off_ref[...]
        sin = sin_b * cos_off_ref[...] + cos_b * sin_off_ref[...]

        def project(first_sec, n_sec):
            return jnp.concatenate(
                [jnp.dot(h, w_bf16_ref[sec], preferred_element_type=F32)
                 for sec in range(first_sec, first_sec + n_sec)], axis=1)

        def out_cols(sec, n_sec=1):
            return slice(sec * SEC, (sec + n_sec) * SEC)

        qk = project(W_Q, 2)
        for sec in range(2):
            for hh in range(RET_HEADS):
                lo = sec * SEC + hh * QK_DIM
                t1 = qk[:, lo:lo + half]
                t2 = qk[:, lo + half:lo + QK_DIM]
                r1 = t1 * cos - t2 * sin
                r2 = t1 * sin + t2 * cos
                if sec == 0:
                    scale = QK_DIM ** -0.5
                    vq_ref[hh, :, V_DIM:V_DIM + half] = (r1 * scale).astype(BF16)
                    vq_ref[hh, :, V_DIM + half:VQ_DIM] = (r2 * scale).astype(BF16)
                else:
                    out = hh * QK_DIM
                    kt_ref[out:out + half, :] = r1.T.astype(BF16)
                    kt_ref[out + half:out + QK_DIM, :] = r2.T.astype(BF16)

        gsw = project(W_GSW, 2)
        o_ref[:, out_cols(SEC_GSW, 2)] = (gsw * _sigmoid(gsw) * gret_ref[...]).astype(BF16)

        gates = project(W_GA, 2)
        o_ref[:, out_cols(SEC_GA, 2)] = _sigmoid(gates).astype(BF16)

        conv_in = project(W_XC, 3)
        o_ref[:, out_cols(SEC_U)] = (conv_in[:, out_cols(W_GC)] * conv_in[:, out_cols(W_XC)]
                                     ).astype(BF16)
        o_ref[:, out_cols(SEC_GB)] = conv_in[:, out_cols(W_GB)].astype(BF16)

        values = project(W_V, 2)
        for hh in range(RET_HEADS):
            vq_ref[hh, :, :V_DIM] = values[:, hh * V_DIM:(hh + 1) * V_DIM].astype(BF16)

    @pl.when(jnp.logical_and(step >= 0, step < SIDE_CAST_STEPS))
    def _():
        for src, dst in zip(side_in, side_out):
            dst[...] = src[...].astype(BF16)


def _inproj(x2, g_mix, w_in, rot, g_ret, side_weights, tm):
    t, d = x2.shape
    cos_base, sin_base, cos_off, sin_off = rot
    assert cos_off.shape[0] == tm and w_in.shape[1] == N_W_SEC * SEC
    n_out = N_OUT_SEC * SEC
    n_steps = t // tm
    assert n_steps >= SIDE_CAST_STEPS

    def tile(i):
        return jnp.maximum(i - N_W_SEC, 0)

    def slab_spec(w):
        rows = w.shape[0] // SIDE_CAST_STEPS
        assert rows * SIDE_CAST_STEPS == w.shape[0] and rows % BF16_SUBLANES == 0
        return pl.BlockSpec((rows, w.shape[1]),
                            lambda i: (jnp.minimum(tile(i), SIDE_CAST_STEPS - 1), 0))

    side_specs = [slab_spec(w) for w in side_weights]
    outs = pl.pallas_call(
        functools.partial(_inproj_kernel, n_pos_blocks=cos_base.shape[0],
                          n_side=len(side_weights)),
        grid=(N_W_SEC + n_steps,),
        in_specs=[
            pl.BlockSpec((tm, d), lambda i: (tile(i), 0)),
            _const_spec(g_mix.shape),
            pl.BlockSpec((d, SEC), lambda i: (0, jnp.minimum(i, N_W_SEC - 1))),
            _const_spec(cos_base.shape),
            _const_spec(sin_base.shape),
            _const_spec(cos_off.shape),
            _const_spec(sin_off.shape),
            _const_spec(g_ret.shape),
        ] + side_specs,
        out_specs=[
            pl.BlockSpec((tm, n_out), lambda i: (tile(i), 0)),
            pl.BlockSpec((RET_HEADS * QK_DIM, tm), lambda i: (0, tile(i))),
            pl.BlockSpec((RET_HEADS, tm, VQ_DIM), lambda i: (0, tile(i), 0)),
        ] + side_specs,
        out_shape=[
            jax.ShapeDtypeStruct((t, n_out), BF16),
            jax.ShapeDtypeStruct((RET_HEADS * QK_DIM, t), BF16),
            jax.ShapeDtypeStruct((RET_HEADS, t, VQ_DIM), BF16),
        ] + [jax.ShapeDtypeStruct(w.shape, BF16) for w in side_weights],
        scratch_shapes=[pltpu.VMEM((N_W_SEC, d, SEC), BF16)],
        compiler_params=pltpu.CompilerParams(
            dimension_semantics=("arbitrary",),
            vmem_limit_bytes=VMEM_LIMIT),
        name="inproj",
    )(x2, g_mix, w_in, cos_base, sin_base, cos_off, sin_off, g_ret, *side_weights)
    return outs[0], outs[1], outs[2], outs[3:]


def _retention_kernel(lg_ref, vq_ref, ktf_ref, ktb_ref, vb_ref, z_ref,
                      decay_ref, xif_ref, xib_ref, zetaf_ref, zetab_ref,
                      fstate_ref, bstate_ref, rall_ref,
                      *, n_pairs, n_blocks, chunks_per_block):
    c = RET_CHUNK
    cpb = chunks_per_block
    n_chunks = n_blocks * cpb
    p = pl.program_id(0)
    j = pl.program_id(1)
    head_f = jnp.maximum(p - 1, 0) % RET_HEADS
    head_b = jnp.minimum(p, n_pairs - 1) % RET_HEADS
    lg_f = lg_ref[0, head_f]
    lg_fb = lg_ref[1, head_f]
    lg_b = lg_ref[1, head_b]
    has_fwd = p >= 1
    has_bwd = p < n_pairs

    @pl.when(jnp.logical_and(j == 0, has_fwd))
    def _():
        row = lax.broadcasted_iota(jnp.int32, (c, c), 0).astype(F32)
        col = lax.broadcasted_iota(jnp.int32, (c, c), 1).astype(F32)
        diff = row - col
        decay_ref[...] = jnp.where(diff >= 0.0,
                                   jnp.exp(jnp.maximum(diff, 0.0) * lg_f),
                                   jnp.exp(jnp.maximum(-diff, 0.0) * lg_fb))
        zetaf_ref[...] = jnp.exp((c - 1.0 - col) * lg_f).astype(BF16)
        idx = lax.broadcasted_iota(jnp.int32, (c, LANES), 0).astype(F32)
        xif_ref[...] = jnp.exp((idx + 1.0) * lg_f).astype(BF16)
        xib_ref[...] = jnp.exp((c - idx) * lg_fb).astype(BF16)
        fstate_ref[...] = jnp.zeros_like(fstate_ref)

    @pl.when(jnp.logical_and(j == 0, has_bwd))
    def _():
        col = lax.broadcasted_iota(jnp.int32, (c, c), 1).astype(F32)
        zetab_ref[...] = jnp.exp(col * lg_b).astype(BF16)
        bstate_ref[...] = jnp.zeros_like(bstate_ref)

    def row_scaled(a, scale_ref):
        scale = scale_ref[...]
        return jnp.concatenate(
            [a[:, n * LANES:(n + 1) * LANES] * scale
             for n in range(a.shape[1] // LANES)], axis=1)

    def decayed_kv(kt_ref, v_ref, zeta_ref, ci):
        rows = slice(ci * c, (ci + 1) * c)
        return jnp.dot(kt_ref[:, rows] * zeta_ref[...], v_ref[rows, :V_DIM],
                       preferred_element_type=F32)

    def bwd_sweep():
        blk = n_blocks - 1 - j
        slot = (p % 2) * n_chunks
        chunk_decay = jnp.exp(jnp.zeros((1, V_DIM), F32) + c * lg_b)
        state = bstate_ref[...]
        for ci in reversed(range(cpb)):
            rall_ref[slot + blk * cpb + ci] = state.astype(BF16)
            state = chunk_decay * state + decayed_kv(ktb_ref, vb_ref, zetab_ref, ci)
        bstate_ref[...] = state

    def fwd_sweep():
        slot = ((p - 1) % 2) * n_chunks
        chunk_decay = jnp.exp(jnp.zeros((1, V_DIM), F32) + c * lg_f)

        def decayed_scores(ci):
            rows = slice(ci * c, (ci + 1) * c)
            scores = jnp.dot(vq_ref[rows, V_DIM:], ktf_ref[:, rows],
                             preferred_element_type=F32)
            return (scores * decay_ref[...]).astype(BF16)

        state = fstate_ref[...]
        pmat_next = decayed_scores(0)
        kv_next = decayed_kv(ktf_ref, vq_ref, zetaf_ref, 0)
        for ci in range(cpb):
            rows = slice(ci * c, (ci + 1) * c)
            pmat, kv = pmat_next, kv_next
            if ci + 1 < cpb:
                pmat_next = decayed_scores(ci + 1)
                kv_next = decayed_kv(ktf_ref, vq_ref, zetaf_ref, ci + 1)
            q = vq_ref[rows, V_DIM:]
            lhs = jnp.concatenate([pmat, row_scaled(q, xib_ref), row_scaled(q, xif_ref)],
                                  axis=1)
            rhs = jnp.concatenate([vq_ref[rows, :V_DIM], rall_ref[slot + j * cpb + ci],
                                   state.astype(BF16)], axis=0)
            ret = jnp.dot(lhs, rhs, preferred_element_type=F32)
            mu = jnp.mean(ret, axis=-1, keepdims=True)
            var = jnp.mean(jnp.square(ret - mu), axis=-1, keepdims=True)
            rstd = lax.rsqrt(var + EPS)
            z_ref[rows, :] = (ret * rstd - mu * rstd).astype(BF16)
            state = chunk_decay * state + kv
        fstate_ref[...] = state

    @pl.when(jnp.logical_not(has_fwd))
    def _():
        bwd_sweep()

    @pl.when(jnp.logical_and(has_fwd, has_bwd))
    def _():
        fwd_sweep()
        bwd_sweep()

    @pl.when(jnp.logical_not(has_bwd))
    def _():
        fwd_sweep()


def _retention(lg, vq, kt, batch, seq, block_tokens):
    c = RET_CHUNK
    nb = seq // block_tokens
    cpb = block_tokens // c
    nc = seq // c
    t = batch * seq
    n_pairs = batch * RET_HEADS

    def fwd_idx(p, j):
        pair = jnp.maximum(p - 1, 0)
        return pair // RET_HEADS, pair % RET_HEADS, jnp.where(p == 0, 0, j)

    def bwd_idx(p, j):
        pair = jnp.minimum(p, n_pairs - 1)
        return pair // RET_HEADS, pair % RET_HEADS, jnp.where(p == n_pairs, 0, nb - 1 - j)

    def head_rows(idx_fn):
        def index_map(p, j):
            b, h, blk = idx_fn(p, j)
            return h, b * nb + blk, 0
        return index_map

    def head_major(idx_fn):
        def index_map(p, j):
            b, h, blk = idx_fn(p, j)
            return h, b * nb + blk
        return index_map

    return pl.pallas_call(
        functools.partial(_retention_kernel, n_pairs=n_pairs, n_blocks=nb,
                          chunks_per_block=cpb),
        grid=(n_pairs + 1, nb),
        in_specs=[
            pl.BlockSpec(memory_space=pltpu.SMEM),
            pl.BlockSpec((None, block_tokens, VQ_DIM), head_rows(fwd_idx)),
            pl.BlockSpec((QK_DIM, block_tokens), head_major(fwd_idx)),
            pl.BlockSpec((QK_DIM, block_tokens), head_major(bwd_idx)),
            pl.BlockSpec((None, block_tokens, V_DIM), head_rows(bwd_idx)),
        ],
        out_specs=pl.BlockSpec((None, block_tokens, V_DIM), head_rows(fwd_idx)),
        out_shape=jax.ShapeDtypeStruct((RET_HEADS, t, V_DIM), BF16),
        scratch_shapes=[
            pltpu.VMEM((c, c), F32),
            pltpu.VMEM((c, LANES), BF16),
            pltpu.VMEM((c, LANES), BF16),
            pltpu.VMEM((QK_DIM, c), BF16),
            pltpu.VMEM((QK_DIM, c), BF16),
            pltpu.VMEM((QK_DIM, V_DIM), F32),
            pltpu.VMEM((QK_DIM, V_DIM), F32),
            pltpu.VMEM((2 * nc, QK_DIM, V_DIM), BF16),
        ],
        compiler_params=pltpu.CompilerParams(
            dimension_semantics=("arbitrary", "arbitrary"),
            vmem_limit_bytes=VMEM_LIMIT),
        name="retention",
    )(lg, vq, kt, kt, vq)


def _merge_kernel(u_ref, gb_ref, up_ref, un_ref, ga_ref, gr_ref, z_ref, gsw_ref,
                  wconv_ref, wa_ref, wr_ref, o_ref, *, tiles_per_seq):
    i = pl.program_id(0)
    tm = u_ref.shape[0]
    u = u_ref[...].astype(F32)
    pos = i % tiles_per_seq
    last = BF16_SUBLANES - 1
    u_before = up_ref[last:last + 1, :].astype(F32) * (pos != 0).astype(F32)
    u_after = un_ref[0:1, :].astype(F32) * (pos != tiles_per_seq - 1).astype(F32)
    row = lax.broadcasted_iota(jnp.int32, u.shape, 0)
    u_prev = jnp.where(row == 0, u_before, pltpu.roll(u, 1, axis=0))
    u_next = jnp.where(row == tm - 1, u_after, pltpu.roll(u, tm - 1, axis=0))
    conv = u_prev * wconv_ref[0:1, :] + u * wconv_ref[1:2, :] + u_next * wconv_ref[2:3, :]
    a_in = (gb_ref[...].astype(F32) * conv).astype(BF16)

    def branch_outputs(rows):
        z = jnp.concatenate([z_ref[hh, rows, :] for hh in range(RET_HEADS)], axis=1)
        y_r = jnp.dot(z * gsw_ref[rows, :], wr_ref[...], preferred_element_type=F32)
        y_a = jnp.dot(a_in[rows, :], wa_ref[...], preferred_element_type=F32)
        return y_a, y_r

    blocks = [slice(r, r + SUBBLOCK_ROWS) for r in range(0, tm, SUBBLOCK_ROWS)]
    y_next = branch_outputs(blocks[0])
    for s, rows in enumerate(blocks):
        y_a, y_r = y_next
        if s + 1 < len(blocks):
            y_next = branch_outputs(blocks[s + 1])
        merged = ga_ref[rows, :].astype(F32) * y_a + gr_ref[rows, :].astype(F32) * y_r
        o_ref[rows, :] = merged.astype(BF16)


def _merge(proj, z, w_conv, w_a, w_r, seq, tm):
    t, d = proj.shape[0], D_MODEL
    hb = tm // BF16_SUBLANES
    n_halo_blocks = t // BF16_SUBLANES
    halo = (BF16_SUBLANES, SEC)
    return pl.pallas_call(
        functools.partial(_merge_kernel, tiles_per_seq=seq // tm),
        grid=(t // tm,),
        in_specs=[
            pl.BlockSpec((tm, SEC), lambda i: (i, SEC_U)),
            pl.BlockSpec((tm, SEC), lambda i: (i, SEC_GB)),
            pl.BlockSpec(halo, lambda i: (jnp.maximum(i * hb - 1, 0), SEC_U)),
            pl.BlockSpec(halo, lambda i: (jnp.minimum((i + 1) * hb, n_halo_blocks - 1), SEC_U)),
            pl.BlockSpec((tm, SEC), lambda i: (i, SEC_GA)),
            pl.BlockSpec((tm, SEC), lambda i: (i, SEC_GR)),
            pl.BlockSpec((RET_HEADS, tm, V_DIM), lambda i: (0, i, 0)),
            pl.BlockSpec((tm, RET_V), lambda i: (i, (SEC_GSW * SEC) // RET_V)),
            _const_spec(w_conv.shape),
            _const_spec(w_a.shape),
            _const_spec(w_r.shape),
        ],
        out_specs=pl.BlockSpec((tm, d), lambda i: (i, 0)),
        out_shape=jax.ShapeDtypeStruct((t, d), BF16),
        compiler_params=pltpu.CompilerParams(
            dimension_semantics=("arbitrary",),
            vmem_limit_bytes=VMEM_LIMIT),
        name="merge",
    )(proj, proj, proj, proj, proj, proj, z, proj, w_conv, w_a, w_r)


def _ffn_kernel(x_ref, m_ref, wo_ref, gffn_ref, wg_ref, wu_ref, wd_ref, gfin_ref, o_ref,
                *, final_norm):
    def residual(rows):
        return x_ref[rows, :] + jnp.dot(m_ref[rows, :], wo_ref[...], preferred_element_type=F32)

    def gate_up(x1):
        h2 = (_rms_scale(x1) * gffn_ref[...]).astype(BF16)
        return (jnp.dot(h2, wg_ref[...], preferred_element_type=F32),
                jnp.dot(h2, wu_ref[...], preferred_element_type=F32), x1)

    blocks = [slice(r, r + SUBBLOCK_ROWS) for r in range(0, x_ref.shape[0], SUBBLOCK_ROWS)]
    x1s = [residual(rows) for rows in blocks[:2]]
    gu_next = gate_up(x1s[0])
    for s, rows in enumerate(blocks):
        gate, up, x1 = gu_next
        if s + 2 < len(blocks):
            x1s.append(residual(blocks[s + 2]))
        if s + 1 < len(blocks):
            gu_next = gate_up(x1s[s + 1])
        act = (gate * _sigmoid(gate) * up).astype(BF16)
        y = x1 + jnp.dot(act, wd_ref[...], preferred_element_type=F32)
        o_ref[rows, :] = _rms_scale(y) * gfin_ref[...] if final_norm else y


def _ffn(x2, merged, w_o, g_ffn, w_gate, w_up, w_down, g_final, final_norm, tm):
    t, d = x2.shape
    return pl.pallas_call(
        functools.partial(_ffn_kernel, final_norm=final_norm),
        grid=(t // tm,),
        in_specs=[
            pl.BlockSpec((tm, d), lambda i: (i, 0)),
            pl.BlockSpec((tm, d), lambda i: (i, 0)),
            _const_spec(w_o.shape),
            _const_spec(g_ffn.shape),
            _const_spec(w_gate.shape),
            _const_spec(w_up.shape),
            _const_spec(w_down.shape),
            _const_spec(g_final.shape),
        ],
        out_specs=pl.BlockSpec((tm, d), lambda i: (i, 0)),
        out_shape=jax.ShapeDtypeStruct((t, d), F32),
        compiler_params=pltpu.CompilerParams(
            dimension_semantics=("arbitrary",),
            vmem_limit_bytes=VMEM_LIMIT),
        name="ffn",
    )(x2, merged, w_o, g_ffn, w_gate, w_up, w_down, g_final)


def _rotary_tables(seq, tm):
    freqs = ROPE_BASE ** (-jnp.arange(0, QK_DIM, 2, dtype=F32) / QK_DIM)
    base = (jnp.arange(seq // tm, dtype=F32) * tm)[:, None] * freqs[None, :]
    off = jnp.arange(tm, dtype=F32)[:, None] * freqs[None, :]
    return jnp.cos(base), jnp.sin(base), jnp.cos(off), jnp.sin(off)


def kernel(x, g_mix, w_in, w_conv, dec_f, dec_b, g_ret, w_a_out, w_r_out, w_o,
           g_ffn, w_ff_gate, w_ff_up, w_ff_down, g_final):
    batch, seq, d = x.shape
    depth = w_in.shape[0]
    assert d == D_MODEL and seq % RET_BLOCK_TOKENS == 0 and seq % MERGE_ROWS == 0
    x2 = x.reshape(batch * seq, d)
    rot = _rotary_tables(seq, INPROJ_ROWS)
    for l in range(depth):
        lg = jnp.stack([jax.nn.log_sigmoid(dec_f[l].astype(F32)),
                        jax.nn.log_sigmoid(dec_b[l].astype(F32))])
        side = [w_a_out[l], w_r_out[l], w_o[l], w_ff_gate[l], w_ff_up[l], w_ff_down[l]]
        proj, kt, vq, (w_a, w_r, w_ob, w_gate, w_up, w_down) = _inproj(
            x2, g_mix[l][None, :], w_in[l], rot,
            g_ret[l][None, :].astype(F32), side, tm=INPROJ_ROWS)
        z = _retention(lg, vq, kt, batch, seq, block_tokens=RET_BLOCK_TOKENS)
        merged = _merge(proj, z, w_conv[l], w_a, w_r, seq, tm=MERGE_ROWS)
        x2 = _ffn(x2, merged, w_ob, g_ffn[l][None, :], w_gate, w_up, w_down,
                  g_final[None, :], final_norm=(l == depth - 1), tm=FFN_ROWS)
    return x2.reshape(batch, seq, d)
```

```python
import functools

import jax
import jax.numpy as jnp
from jax import lax
from jax.experimental import pallas as pl
from jax.experimental.pallas import tpu as pltpu

F32 = jnp.float32
BF16 = jnp.bfloat16

D_MODEL = 1024
RET_HEADS = 4
QK_DIM = D_MODEL // RET_HEADS
V_DIM = 2 * QK_DIM
RET_V = RET_HEADS * V_DIM
ROPE_BASE = 10000.0
EPS = 1e-6

SEC = 1024
W_XC, W_GB, W_GC, W_Q, W_K, W_V, W_GSW, W_GA, W_GR, N_W_SEC = 0, 1, 2, 3, 4, 5, 7, 9, 10, 11
SEC_U, SEC_GB, SEC_GSW, SEC_GA, SEC_GR, N_OUT_SEC = 0, 1, 2, 4, 5, 6
VQ_DIM = V_DIM + QK_DIM

RET_CHUNK = 256
SIDE_CAST_STEPS = 16
SUBBLOCK_ROWS = 256
INPROJ_ROWS = 256
RET_BLOCK_TOKENS = 4096
MERGE_ROWS = 512
FFN_ROWS = 512
LANES = 128
BF16_SUBLANES = 16

VMEM_LIMIT = 56 * 1024 * 1024
MERGE_VMEM_LIMIT = 40 * 1024 * 1024
FFN_VMEM_LIMIT = 44 * 1024 * 1024


def _sigmoid(x):
    return 1.0 / (1.0 + jnp.exp(-x))


def _rms_scale(x):
    return x * lax.rsqrt(jnp.mean(x * x, axis=-1, keepdims=True) + EPS)


def _const_spec(shape):
    return pl.BlockSpec(shape, lambda *_: (0,) * len(shape), pipeline_mode=pl.Buffered(1))


def _inproj_kernel(x_ref, g_ref, w_ref, cos_base_ref, sin_base_ref, cos_off_ref, sin_off_ref,
                   gret_ref, *rest, n_pos_blocks, n_side):
    side_in = rest[:n_side]
    o_ref, kt_ref, vq_ref = rest[n_side:n_side + 3]
    side_out = rest[n_side + 3:2 * n_side + 3]
    w_bf16_ref = rest[2 * n_side + 3]
    step = pl.program_id(0) - N_W_SEC

    @pl.when(step < 0)
    def _():
        w_bf16_ref[pl.program_id(0)] = w_ref[...].astype(BF16)

    @pl.when(step >= 0)
    def _():
        h = (_rms_scale(x_ref[...]) * g_ref[...]).astype(BF16)
        half = QK_DIM // 2
        pos_block = pl.ds(step % n_pos_blocks, 1)
        cos_b = cos_base_ref[pos_block, :]
        sin_b = sin_base_ref[pos_block, :]
        cos = cos_b * cos_off_ref[...] - sin_b * sin_off_ref[...]
        sin = sin_b * cos_off_ref[...] + cos_b * sin_off_ref[...]

        def project(first_sec, n_sec):
            return jnp.concatenate(
                [jnp.dot(h, w_bf16_ref[sec], preferred_element_type=F32)
                 for sec in range(first_sec, first_sec + n_sec)], axis=1)

        def out_cols(sec, n_sec=1):
            return slice(sec * SEC, (sec + n_sec) * SEC)

        qk = project(W_Q, 2)
        for sec in range(2):
            for hh in range(RET_HEADS):
                lo = sec * SEC + hh * QK_DIM
                t1 = qk[:, lo:lo + half]
                t2 = qk[:, lo + half:lo + QK_DIM]
                r1 = t1 * cos - t2 * sin
                r2 = t1 * sin + t2 * cos
                if sec == 0:
                    scale = QK_DIM ** -0.5
                    vq_ref[hh, :, V_DIM:V_DIM + half] = (r1 * scale).astype(BF16)
                    vq_ref[hh, :, V_DIM + half:VQ_DIM] = (r2 * scale).astype(BF16)
                else:
                    out = hh * QK_DIM
                    kt_ref[out:out + half, :] = r1.T.astype(BF16)
                    kt_ref[out + half:out + QK_DIM, :] = r2.T.astype(BF16)

        gsw = project(W_GSW, 2)
        o_ref[:, out_cols(SEC_GSW, 2)] = (gsw * _sigmoid(gsw) * gret_ref[...]).astype(BF16)

        gates = project(W_GA, 2)
        o_ref[:, out_cols(SEC_GA, 2)] = _sigmoid(gates).astype(BF16)

        conv_in = project(W_XC, 3)
        o_ref[:, out_cols(SEC_U)] = (conv_in[:, out_cols(W_GC)] * conv_in[:, out_cols(W_XC)]
                                     ).astype(BF16)
        o_ref[:, out_cols(SEC_GB)] = conv_in[:, out_cols(W_GB)].astype(BF16)

        values = project(W_V, 2)
        for hh in range(RET_HEADS):
            vq_ref[hh, :, :V_DIM] = values[:, hh * V_DIM:(hh + 1) * V_DIM].astype(BF16)

    @pl.when(jnp.logical_and(step >= 0, step < SIDE_CAST_STEPS))
    def _():
        for src, dst in zip(side_in, side_out):
            dst[...] = src[...].astype(BF16)


def _inproj(x2, g_mix, w_in, rot, g_ret, side_weights, tm):
    t, d = x2.shape
    cos_base, sin_base, cos_off, sin_off = rot
    assert cos_off.shape[0] == tm and w_in.shape[1] == N_W_SEC * SEC
    n_out = N_OUT_SEC * SEC
    n_steps = t // tm
    assert n_steps >= SIDE_CAST_STEPS

    def tile(i):
        return jnp.maximum(i - N_W_SEC, 0)

    def slab_spec(w):
        rows = w.shape[0] // SIDE_CAST_STEPS
        assert rows * SIDE_CAST_STEPS == w.shape[0] and rows % BF16_SUBLANES == 0
        return pl.BlockSpec((rows, w.shape[1]),
                            lambda i: (jnp.minimum(tile(i), SIDE_CAST_STEPS - 1), 0))

    side_specs = [slab_spec(w) for w in side_weights]
    outs = pl.pallas_call(
        functools.partial(_inproj_kernel, n_pos_blocks=cos_base.shape[0],
                          n_side=len(side_weights)),
        grid=(N_W_SEC + n_steps,),
        in_specs=[
            pl.BlockSpec((tm, d), lambda i: (tile(i), 0)),
            _const_spec(g_mix.shape),
            pl.BlockSpec((d, SEC), lambda i: (0, jnp.minimum(i, N_W_SEC - 1))),
            _const_spec(cos_base.shape),
            _const_spec(sin_base.shape),
            _const_spec(cos_off.shape),
            _const_spec(sin_off.shape),
            _const_spec(g_ret.shape),
        ] + side_specs,
        out_specs=[
            pl.BlockSpec((tm, n_out), lambda i: (tile(i), 0)),
            pl.BlockSpec((RET_HEADS * QK_DIM, tm), lambda i: (0, tile(i))),
            pl.BlockSpec((RET_HEADS, tm, VQ_DIM), lambda i: (0, tile(i), 0)),
        ] + side_specs,
        out_shape=[
            jax.ShapeDtypeStruct((t, n_out), BF16),
            jax.ShapeDtypeStruct((RET_HEADS * QK_DIM, t), BF16),
            jax.ShapeDtypeStruct((RET_HEADS, t, VQ_DIM), BF16),
        ] + [jax.ShapeDtypeStruct(w.shape, BF16) for w in side_weights],
        scratch_shapes=[pltpu.VMEM((N_W_SEC, d, SEC), BF16)],
        compiler_params=pltpu.CompilerParams(
            dimension_semantics=("arbitrary",),
            vmem_limit_bytes=VMEM_LIMIT),
        name="inproj",
    )(x2, g_mix, w_in, cos_base, sin_base, cos_off, sin_off, g_ret, *side_weights)
    return outs[0], outs[1], outs[2], outs[3:]


def _retention_kernel(lg_ref, vq_ref, ktf_ref, ktb_ref, vb_ref, z_ref,
                      decay_ref, xif_ref, xib_ref, zetaf_ref, zetab_ref,
                      fstate_ref, bstate_ref, rall_ref,
                      *, n_pairs, n_blocks, chunks_per_block):
    c = RET_CHUNK
    cpb = chunks_per_block
    n_chunks = n_blocks * cpb
    p = pl.program_id(0)
    j = pl.program_id(1)
    head_f = jnp.maximum(p - 1, 0) % RET_HEADS
    head_b = jnp.minimum(p, n_pairs - 1) % RET_HEADS
    lg_f = lg_ref[0, head_f]
    lg_fb = lg_ref[1, head_f]
    lg_b = lg_ref[1, head_b]
    has_fwd = p >= 1
    has_bwd = p < n_pairs

    @pl.when(jnp.logical_and(j == 0, has_fwd))
    def _():
        row = lax.broadcasted_iota(jnp.int32, (c, c), 0).astype(F32)
        col = lax.broadcasted_iota(jnp.int32, (c, c), 1).astype(F32)
        diff = row - col
        decay_ref[...] = jnp.where(diff >= 0.0,
                                   jnp.exp(jnp.maximum(diff, 0.0) * lg_f),
                                   jnp.exp(jnp.maximum(-diff, 0.0) * lg_fb))
        zetaf_ref[...] = jnp.exp((c - 1.0 - col) * lg_f).astype(BF16)
        idx = lax.broadcasted_iota(jnp.int32, (c, LANES), 0).astype(F32)
        xif_ref[...] = jnp.exp((idx + 1.0) * lg_f).astype(BF16)
        xib_ref[...] = jnp.exp((c - idx) * lg_fb).astype(BF16)
        fstate_ref[...] = jnp.zeros_like(fstate_ref)

    @pl.when(jnp.logical_and(j == 0, has_bwd))
    def _():
        col = lax.broadcasted_iota(jnp.int32, (c, c), 1).astype(F32)
        zetab_ref[...] = jnp.exp(col * lg_b).astype(BF16)
        bstate_ref[...] = jnp.zeros_like(bstate_ref)

    def row_scaled(a, scale_ref):
        scale = scale_ref[...]
        return jnp.concatenate(
            [a[:, n * LANES:(n + 1) * LANES] * scale
             for n in range(a.shape[1] // LANES)], axis=1)

    def decayed_kv(kt_ref, v_ref, zeta_ref, ci):
        rows = slice(ci * c, (ci + 1) * c)
        return jnp.dot(kt_ref[:, rows] * zeta_ref[...], v_ref[rows, :V_DIM],
                       preferred_element_type=F32)

    def bwd_sweep():
        blk = n_blocks - 1 - j
        slot = (p % 2) * n_chunks
        chunk_decay = jnp.exp(jnp.zeros((1, V_DIM), F32) + c * lg_b)
        state = bstate_ref[...]
        for ci in reversed(range(cpb)):
            rall_ref[slot + blk * cpb + ci] = state.astype(BF16)
            state = chunk_decay * state + decayed_kv(ktb_ref, vb_ref, zetab_ref, ci)
        bstate_ref[...] = state

    def fwd_sweep():
        slot = ((p - 1) % 2) * n_chunks
        chunk_decay = jnp.exp(jnp.zeros((1, V_DIM), F32) + c * lg_f)

        def decayed_scores(ci):
            rows = slice(ci * c, (ci + 1) * c)
            scores = jnp.dot(vq_ref[rows, V_DIM:], ktf_ref[:, rows],
                             preferred_element_type=F32)
            return (scores * decay_ref[...]).astype(BF16)

        state = fstate_ref[...]
        pmat_next = decayed_scores(0)
        kv_next = decayed_kv(ktf_ref, vq_ref, zetaf_ref, 0)
        for ci in range(cpb):
            rows = slice(ci * c, (ci + 1) * c)
            pmat, kv = pmat_next, kv_next
            if ci + 1 < cpb:
                pmat_next = decayed_scores(ci + 1)
                kv_next = decayed_kv(ktf_ref, vq_ref, zetaf_ref, ci + 1)
            q = vq_ref[rows, V_DIM:]
            lhs = jnp.concatenate([pmat, row_scaled(q, xib_ref), row_scaled(q, xif_ref)],
                                  axis=1)
            rhs = jnp.concatenate([vq_ref[rows, :V_DIM], rall_ref[slot + j * cpb + ci],
                                   state.astype(BF16)], axis=0)
            ret = jnp.dot(lhs, rhs, preferred_element_type=F32)
            mu = jnp.mean(ret, axis=-1, keepdims=True)
            var = jnp.mean(jnp.square(ret - mu), axis=-1, keepdims=True)
            rstd = lax.rsqrt(var + EPS)
            z_ref[rows, :] = (ret * rstd - mu * rstd).astype(BF16)
            state = chunk_decay * state + kv
        fstate_ref[...] = state

    @pl.when(jnp.logical_not(has_fwd))
    def _():
        bwd_sweep()

    @pl.when(jnp.logical_and(has_fwd, has_bwd))
    def _():
        fwd_sweep()
        bwd_sweep()

    @pl.when(jnp.logical_not(has_bwd))
    def _():
        fwd_sweep()


def _retention(lg, vq, kt, batch, seq, block_tokens):
    c = RET_CHUNK
    nb = seq // block_tokens
    cpb = block_tokens // c
    nc = seq // c
    t = batch * seq
    n_pairs = batch * RET_HEADS

    def fwd_idx(p, j):
        pair = jnp.maximum(p - 1, 0)
        return pair // RET_HEADS, pair % RET_HEADS, jnp.where(p == 0, 0, j)

    def bwd_idx(p, j):
        pair = jnp.minimum(p, n_pairs - 1)
        return pair // RET_HEADS, pair % RET_HEADS, jnp.where(p == n_pairs, 0, nb - 1 - j)

    def head_rows(idx_fn):
        def index_map(p, j):
            b, h, blk = idx_fn(p, j)
            return h, b * nb + blk, 0
        return index_map

    def head_major(idx_fn):
        def index_map(p, j):
            b, h, blk = idx_fn(p, j)
            return h, b * nb + blk
        return index_map

    return pl.pallas_call(
        functools.partial(_retention_kernel, n_pairs=n_pairs, n_blocks=nb,
                          chunks_per_block=cpb),
        grid=(n_pairs + 1, nb),
        in_specs=[
            pl.BlockSpec(memory_space=pltpu.SMEM),
            pl.BlockSpec((None, block_tokens, VQ_DIM), head_rows(fwd_idx)),
            pl.BlockSpec((QK_DIM, block_tokens), head_major(fwd_idx)),
            pl.BlockSpec((QK_DIM, block_tokens), head_major(bwd_idx)),
            pl.BlockSpec((None, block_tokens, V_DIM), head_rows(bwd_idx)),
        ],
        out_specs=pl.BlockSpec((None, block_tokens, V_DIM), head_rows(fwd_idx)),
        out_shape=jax.ShapeDtypeStruct((RET_HEADS, t, V_DIM), BF16),
        scratch_shapes=[
            pltpu.VMEM((c, c), F32),
            pltpu.VMEM((c, LANES), BF16),
            pltpu.VMEM((c, LANES), BF16),
            pltpu.VMEM((QK_DIM, c), BF16),
            pltpu.VMEM((QK_DIM, c), BF16),
            pltpu.VMEM((QK_DIM, V_DIM), F32),
            pltpu.VMEM((QK_DIM, V_DIM), F32),
            pltpu.VMEM((2 * nc, QK_DIM, V_DIM), BF16),
        ],
        compiler_params=pltpu.CompilerParams(
            dimension_semantics=("arbitrary", "arbitrary"),
            vmem_limit_bytes=VMEM_LIMIT),
        name="retention",
    )(lg, vq, kt, kt, vq)


def _merge_kernel(u_ref, gb_ref, up_ref, un_ref, ga_ref, gr_ref, z_ref, gsw_ref,
                  wconv_ref, wa_ref, wr_ref, o_ref, *, tiles_per_seq):
    i = pl.program_id(0)
    tm = u_ref.shape[0]
    u = u_ref[...].astype(F32)
    pos = i % tiles_per_seq
    last = BF16_SUBLANES - 1
    u_before = up_ref[last:last + 1, :].astype(F32) * (pos != 0).astype(F32)
    u_after = un_ref[0:1, :].astype(F32) * (pos != tiles_per_seq - 1).astype(F32)
    row = lax.broadcasted_iota(jnp.int32, u.shape, 0)
    u_prev = jnp.where(row == 0, u_before, pltpu.roll(u, 1, axis=0))
    u_next = jnp.where(row == tm - 1, u_after, pltpu.roll(u, tm - 1, axis=0))
    conv = u_prev * wconv_ref[0:1, :] + u * wconv_ref[1:2, :] + u_next * wconv_ref[2:3, :]
    a_in = (gb_ref[...].astype(F32) * conv).astype(BF16)

    def branch_outputs(rows):
        z = jnp.concatenate([z_ref[hh, rows, :] for hh in range(RET_HEADS)], axis=1)
        y_r = jnp.dot(z * gsw_ref[rows, :], wr_ref[...], preferred_element_type=F32)
        y_a = jnp.dot(a_in[rows, :], wa_ref[...], preferred_element_type=F32)
        return y_a, y_r

    blocks = [slice(r, r + SUBBLOCK_ROWS) for r in range(0, tm, SUBBLOCK_ROWS)]
    y_next = branch_outputs(blocks[0])
    for s, rows in enumerate(blocks):
        y_a, y_r = y_next
        if s + 1 < len(blocks):
            y_next = branch_outputs(blocks[s + 1])
        merged = ga_ref[rows, :].astype(F32) * y_a + gr_ref[rows, :].astype(F32) * y_r
        o_ref[rows, :] = merged.astype(BF16)


def _merge(proj, z, w_conv, w_a, w_r, seq, tm):
    t, d = proj.shape[0], D_MODEL
    hb = tm // BF16_SUBLANES
    n_halo_blocks = t // BF16_SUBLANES
    halo = (BF16_SUBLANES, SEC)
    return pl.pallas_call(
        functools.partial(_merge_kernel, tiles_per_seq=seq // tm),
        grid=(t // tm,),
        in_specs=[
            pl.BlockSpec((tm, SEC), lambda i: (i, SEC_U)),
            pl.BlockSpec((tm, SEC), lambda i: (i, SEC_GB)),
            pl.BlockSpec(halo, lambda i: (jnp.maximum(i * hb - 1, 0), SEC_U)),
            pl.BlockSpec(halo, lambda i: (jnp.minimum((i + 1) * hb, n_halo_blocks - 1), SEC_U)),
            pl.BlockSpec((tm, SEC), lambda i: (i, SEC_GA)),
            pl.BlockSpec((tm, SEC), lambda i: (i, SEC_GR)),
            pl.BlockSpec((RET_HEADS, tm, V_DIM), lambda i: (0, i, 0)),
            pl.BlockSpec((tm, RET_V), lambda i: (i, (SEC_GSW * SEC) // RET_V)),
            _const_spec(w_conv.shape),
            _const_spec(w_a.shape),
            _const_spec(w_r.shape),
        ],
        out_specs=pl.BlockSpec((tm, d), lambda i: (i, 0)),
        out_shape=jax.ShapeDtypeStruct((t, d), BF16),
        compiler_params=pltpu.CompilerParams(
            dimension_semantics=("arbitrary",),
            vmem_limit_bytes=MERGE_VMEM_LIMIT),
        name="merge",
    )(proj, proj, proj, proj, proj, proj, z, proj, w_conv, w_a, w_r)


def _ffn_kernel(x_ref, m_ref, wo_ref, gffn_ref, wg_ref, wu_ref, wd_ref, gfin_ref, o_ref,
                *, final_norm):
    def residual(rows):
        return x_ref[rows, :] + jnp.dot(m_ref[rows, :], wo_ref[...], preferred_element_type=F32)

    def gate_up(x1):
        h2 = (_rms_scale(x1) * gffn_ref[...]).astype(BF16)
        return (jnp.dot(h2, wg_ref[...], preferred_element_type=F32),
                jnp.dot(h2, wu_ref[...], preferred_element_type=F32), x1)

    blocks = [slice(r, r + SUBBLOCK_ROWS) for r in range(0, x_ref.shape[0], SUBBLOCK_ROWS)]
    x1s = [residual(rows) for rows in blocks[:2]]
    gu_next = gate_up(x1s[0])
    for s, rows in enumerate(blocks):
        gate, up, x1 = gu_next
        if s + 2 < len(blocks):
            x1s.append(residual(blocks[s + 2]))
        if s + 1 < len(blocks):
            gu_next = gate_up(x1s[s + 1])
        act = (gate * _sigmoid(gate) * up).astype(BF16)
        y = x1 + jnp.dot(act, wd_ref[...], preferred_element_type=F32)
        o_ref[rows, :] = _rms_scale(y) * gfin_ref[...] if final_norm else y


def _ffn(x2, merged, w_o, g_ffn, w_gate, w_up, w_down, g_final, final_norm, tm):
    t, d = x2.shape
    return pl.pallas_call(
        functools.partial(_ffn_kernel, final_norm=final_norm),
        grid=(t // tm,),
        in_specs=[
            pl.BlockSpec((tm, d), lambda i: (i, 0)),
            pl.BlockSpec((tm, d), lambda i: (i, 0)),
            _const_spec(w_o.shape),
            _const_spec(g_ffn.shape),
            _const_spec(w_gate.shape),
            _const_spec(w_up.shape),
            _const_spec(w_down.shape),
            _const_spec(g_final.shape),
        ],
        out_specs=pl.BlockSpec((tm, d), lambda i: (i, 0)),
        out_shape=jax.ShapeDtypeStruct((t, d), F32),
        compiler_params=pltpu.CompilerParams(
            dimension_semantics=("arbitrary",),
            vmem_limit_bytes=FFN_VMEM_LIMIT),
        name="ffn",
    )(x2, merged, w_o, g_ffn, w_gate, w_up, w_down, g_final)


def _rotary_tables(seq, tm):
    freqs = ROPE_BASE ** (-jnp.arange(0, QK_DIM, 2, dtype=F32) / QK_DIM)
    base = (jnp.arange(seq // tm, dtype=F32) * tm)[:, None] * freqs[None, :]
    off = jnp.arange(tm, dtype=F32)[:, None] * freqs[None, :]
    return jnp.cos(base), jnp.sin(base), jnp.cos(off), jnp.sin(off)


def kernel(x, g_mix, w_in, w_conv, dec_f, dec_b, g_ret, w_a_out, w_r_out, w_o,
           g_ffn, w_ff_gate, w_ff_up, w_ff_down, g_final):
    batch, seq, d = x.shape
    depth = w_in.shape[0]
    assert d == D_MODEL and seq % RET_BLOCK_TOKENS == 0 and seq % MERGE_ROWS == 0
    x2 = x.reshape(batch * seq, d)
    rot = _rotary_tables(seq, INPROJ_ROWS)
    for l in range(depth):
        lg = jnp.stack([jax.nn.log_sigmoid(dec_f[l].astype(F32)),
                        jax.nn.log_sigmoid(dec_b[l].astype(F32))])
        side = [w_a_out[l], w_r_out[l], w_o[l], w_ff_gate[l], w_ff_up[l], w_ff_down[l]]
        proj, kt, vq, (w_a, w_r, w_ob, w_gate, w_up, w_down) = _inproj(
            x2, g_mix[l][None, :], w_in[l], rot,
            g_ret[l][None, :].astype(F32), side, tm=INPROJ_ROWS)
        z = _retention(lg, vq, kt, batch, seq, block_tokens=RET_BLOCK_TOKENS)
        merged = _merge(proj, z, w_conv[l], w_a, w_r, seq, tm=MERGE_ROWS)
        x2 = _ffn(x2, merged, w_ob, g_ffn[l][None, :], w_gate, w_up, w_down,
                  g_final[None, :], final_norm=(l == depth - 1), tm=FFN_ROWS)
    return x2.reshape(batch, seq, d)
```

```python
import functools

import jax
import jax.numpy as jnp
from jax import lax
from jax.experimental import pallas as pl
from jax.experimental.pallas import tpu as pltpu

F32 = jnp.float32
BF16 = jnp.bfloat16

D_MODEL = 1024
RET_HEADS = 4
QK_DIM = D_MODEL // RET_HEADS
V_DIM = 2 * QK_DIM
RET_V = RET_HEADS * V_DIM
ROPE_BASE = 10000.0
EPS = 1e-6

SEC = 1024
W_XC, W_GB, W_GC, W_Q, W_K, W_V, W_GSW, W_GA, W_GR, N_W_SEC = 0, 1, 2, 3, 4, 5, 7, 9, 10, 11
SEC_U, SEC_GB, SEC_GSW, SEC_GA, SEC_GR, N_OUT_SEC = 0, 1, 2, 4, 5, 6
VQ_DIM = V_DIM + QK_DIM

RET_CHUNK = 256
SIDE_CAST_STEPS = 16
SUBBLOCK_ROWS = 256
INPROJ_ROWS = 256
RET_BLOCK_TOKENS = 4096
MERGE_ROWS = 512
FFN_ROWS = 512
LANES = 128
BF16_SUBLANES = 16

VMEM_LIMIT = 58 * 1024 * 1024


def _sigmoid(x):
    return 1.0 / (1.0 + jnp.exp(-x))


def _rms_scale(x):
    return x * lax.rsqrt(jnp.mean(x * x, axis=-1, keepdims=True) + EPS)


def _const_spec(shape):
    return pl.BlockSpec(shape, lambda *_: (0,) * len(shape), pipeline_mode=pl.Buffered(1))


def _inproj_kernel(x_ref, g_ref, w_ref, cos_base_ref, sin_base_ref, cos_off_ref, sin_off_ref,
                   gret_ref, *rest, n_pos_blocks, n_side):
    side_in = rest[:n_side]
    o_ref, kt_ref, vq_ref = rest[n_side:n_side + 3]
    side_out = rest[n_side + 3:2 * n_side + 3]
    w_bf16_ref = rest[2 * n_side + 3]
    step = pl.program_id(0) - N_W_SEC

    @pl.when(step < 0)
    def _():
        w_bf16_ref[pl.program_id(0)] = w_ref[...].astype(BF16)

    @pl.when(step >= 0)
    def _():
        h = (_rms_scale(x_ref[...]) * g_ref[...]).astype(BF16)
        half = QK_DIM // 2
        pos_block = pl.ds(step % n_pos_blocks, 1)
        cos_b = cos_base_ref[pos_block, :]
        sin_b = sin_base_ref[pos_block, :]
        cos = cos_b * cos_off_ref[...] - sin_b * sin_off_ref[...]
        sin = sin_b * cos_off_ref[...] + cos_b * sin_off_ref[...]

        def project(first_sec, n_sec):
            return jnp.concatenate(
                [jnp.dot(h, w_bf16_ref[sec], preferred_element_type=F32)
                 for sec in range(first_sec, first_sec + n_sec)], axis=1)

        def out_cols(sec, n_sec=1):
            return slice(sec * SEC, (sec + n_sec) * SEC)

        qk = project(W_Q, 2)
        for sec in range(2):
            for hh in range(RET_HEADS):
                lo = sec * SEC + hh * QK_DIM
                t1 = qk[:, lo:lo + half]
                t2 = qk[:, lo + half:lo + QK_DIM]
                r1 = t1 * cos - t2 * sin
                r2 = t1 * sin + t2 * cos
                if sec == 0:
                    scale = QK_DIM ** -0.5
                    vq_ref[hh, :, V_DIM:V_DIM + half] = (r1 * scale).astype(BF16)
                    vq_ref[hh, :, V_DIM + half:VQ_DIM] = (r2 * scale).astype(BF16)
                else:
                    out = hh * QK_DIM
                    kt_ref[out:out + half, :] = r1.T.astype(BF16)
                    kt_ref[out + half:out + QK_DIM, :] = r2.T.astype(BF16)

        gsw = project(W_GSW, 2)
        o_ref[:, out_cols(SEC_GSW, 2)] = (gsw * _sigmoid(gsw) * gret_ref[...]).astype(BF16)

        gates = project(W_GA, 2)
        o_ref[:, out_cols(SEC_GA, 2)] = _sigmoid(gates).astype(BF16)

        conv_in = project(W_XC, 3)
        o_ref[:, out_cols(SEC_U)] = (conv_in[:, out_cols(W_GC)] * conv_in[:, out_cols(W_XC)]
                                     ).astype(BF16)
        o_ref[:, out_cols(SEC_GB)] = conv_in[:, out_cols(W_GB)].astype(BF16)

        values = project(W_V, 2)
        for hh in range(RET_HEADS):
            vq_ref[hh, :, :V_DIM] = values[:, hh * V_DIM:(hh + 1) * V_DIM].astype(BF16)

    @pl.when(jnp.logical_and(step >= 0, step < SIDE_CAST_STEPS))
    def _():
        for src, dst in zip(side_in, side_out):
            dst[...] = src[...].astype(BF16)


def _inproj(x2, g_mix, w_in, rot, g_ret, side_weights, tm):
    t, d = x2.shape
    cos_base, sin_base, cos_off, sin_off = rot
    assert cos_off.shape[0] == tm and w_in.shape[1] == N_W_SEC * SEC
    n_out = N_OUT_SEC * SEC
    n_steps = t // tm
    assert n_steps >= SIDE_CAST_STEPS

    def tile(i):
        return jnp.maximum(i - N_W_SEC, 0)

    def slab_spec(w):
        rows = w.shape[0] // SIDE_CAST_STEPS
        assert rows * SIDE_CAST_STEPS == w.shape[0] and rows % BF16_SUBLANES == 0
        return pl.BlockSpec((rows, w.shape[1]),
                            lambda i: (jnp.minimum(tile(i), SIDE_CAST_STEPS - 1), 0))

    side_specs = [slab_spec(w) for w in side_weights]
    outs = pl.pallas_call(
        functools.partial(_inproj_kernel, n_pos_blocks=cos_base.shape[0],
                          n_side=len(side_weights)),
        grid=(N_W_SEC + n_steps,),
        in_specs=[
            pl.BlockSpec((tm, d), lambda i: (tile(i), 0)),
            _const_spec(g_mix.shape),
            pl.BlockSpec((d, SEC), lambda i: (0, jnp.minimum(i, N_W_SEC - 1))),
            _const_spec(cos_base.shape),
            _const_spec(sin_base.shape),
            _const_spec(cos_off.shape),
            _const_spec(sin_off.shape),
            _const_spec(g_ret.shape),
        ] + side_specs,
        out_specs=[
            pl.BlockSpec((tm, n_out), lambda i: (tile(i), 0)),
            pl.BlockSpec((RET_HEADS * QK_DIM, tm), lambda i: (0, tile(i))),
            pl.BlockSpec((RET_HEADS, tm, VQ_DIM), lambda i: (0, tile(i), 0)),
        ] + side_specs,
        out_shape=[
            jax.ShapeDtypeStruct((t, n_out), BF16),
            jax.ShapeDtypeStruct((RET_HEADS * QK_DIM, t), BF16),
            jax.ShapeDtypeStruct((RET_HEADS, t, VQ_DIM), BF16),
        ] + [jax.ShapeDtypeStruct(w.shape, BF16) for w in side_weights],
        scratch_shapes=[pltpu.VMEM((N_W_SEC, d, SEC), BF16)],
        compiler_params=pltpu.CompilerParams(
            dimension_semantics=("arbitrary",),
            vmem_limit_bytes=VMEM_LIMIT),
        name="inproj",
    )(x2, g_mix, w_in, cos_base, sin_base, cos_off, sin_off, g_ret, *side_weights)
    return outs[0], outs[1], outs[2], outs[3:]


def _retention_kernel(lg_ref, vq_ref, ktf_ref, ktb_ref, vb_ref, z_ref,
                      decay_ref, xif_ref, xib_ref, zetaf_ref, zetab_ref,
                      fstate_ref, bstate_ref, rall_ref,
                      *, n_pairs, n_blocks, chunks_per_block):
    c = RET_CHUNK
    cpb = chunks_per_block
    n_chunks = n_blocks * cpb
    p = pl.program_id(0)
    j = pl.program_id(1)
    head_f = jnp.maximum(p - 1, 0) % RET_HEADS
    head_b = jnp.minimum(p, n_pairs - 1) % RET_HEADS
    lg_f = lg_ref[0, head_f]
    lg_fb = lg_ref[1, head_f]
    lg_b = lg_ref[1, head_b]
    has_fwd = p >= 1
    has_bwd = p < n_pairs

    @pl.when(jnp.logical_and(j == 0, has_fwd))
    def _():
        row = lax.broadcasted_iota(jnp.int32, (c, c), 0).astype(F32)
        col = lax.broadcasted_iota(jnp.int32, (c, c), 1).astype(F32)
        diff = row - col
        decay_ref[...] = jnp.where(diff >= 0.0,
                                   jnp.exp(jnp.maximum(diff, 0.0) * lg_f),
                                   jnp.exp(jnp.maximum(-diff, 0.0) * lg_fb))
        zetaf_ref[...] = jnp.exp((c - 1.0 - col) * lg_f).astype(BF16)
        idx = lax.broadcasted_iota(jnp.int32, (c, LANES), 0).astype(F32)
        xif_ref[...] = jnp.exp((idx + 1.0) * lg_f).astype(BF16)
        xib_ref[...] = jnp.exp((c - idx) * lg_fb).astype(BF16)
        fstate_ref[...] = jnp.zeros_like(fstate_ref)

    @pl.when(jnp.logical_and(j == 0, has_bwd))
    def _():
        col = lax.broadcasted_iota(jnp.int32, (c, c), 1).astype(F32)
        zetab_ref[...] = jnp.exp(col * lg_b).astype(BF16)
        bstate_ref[...] = jnp.zeros_like(bstate_ref)

    def row_scaled(a, scale_ref):
        scale = scale_ref[...]
        return jnp.concatenate(
            [a[:, n * LANES:(n + 1) * LANES] * scale
             for n in range(a.shape[1] // LANES)], axis=1)

    def decayed_kv(kt_ref, v_ref, zeta_ref, ci):
        rows = slice(ci * c, (ci + 1) * c)
        return jnp.dot(kt_ref[:, rows] * zeta_ref[...], v_ref[rows, :V_DIM],
                       preferred_element_type=F32)

    def bwd_sweep():
        blk = n_blocks - 1 - j
        slot = (p % 2) * n_chunks
        chunk_decay = jnp.exp(jnp.zeros((1, V_DIM), F32) + c * lg_b)
        state = bstate_ref[...]
        for ci in reversed(range(cpb)):
            rall_ref[slot + blk * cpb + ci] = state.astype(BF16)
            state = chunk_decay * state + decayed_kv(ktb_ref, vb_ref, zetab_ref, ci)
        bstate_ref[...] = state

    def fwd_sweep():
        slot = ((p - 1) % 2) * n_chunks
        chunk_decay = jnp.exp(jnp.zeros((1, V_DIM), F32) + c * lg_f)

        def decayed_scores(ci):
            rows = slice(ci * c, (ci + 1) * c)
            scores = jnp.dot(vq_ref[rows, V_DIM:], ktf_ref[:, rows],
                             preferred_element_type=F32)
            return (scores * decay_ref[...]).astype(BF16)

        state = fstate_ref[...]
        pmat_next = decayed_scores(0)
        kv_next = decayed_kv(ktf_ref, vq_ref, zetaf_ref, 0)
        for ci in range(cpb):
            rows = slice(ci * c, (ci + 1) * c)
            pmat, kv = pmat_next, kv_next
            if ci + 1 < cpb:
                pmat_next = decayed_scores(ci + 1)
                kv_next = decayed_kv(ktf_ref, vq_ref, zetaf_ref, ci + 1)
            q = vq_ref[rows, V_DIM:]
            lhs = jnp.concatenate([pmat, row_scaled(q, xib_ref), row_scaled(q, xif_ref)],
                                  axis=1)
            rhs = jnp.concatenate([vq_ref[rows, :V_DIM], rall_ref[slot + j * cpb + ci],
                                   state.astype(BF16)], axis=0)
            ret = jnp.dot(lhs, rhs, preferred_element_type=F32)
            mu = jnp.mean(ret, axis=-1, keepdims=True)
            var = jnp.mean(jnp.square(ret - mu), axis=-1, keepdims=True)
            rstd = lax.rsqrt(var + EPS)
            z_ref[rows, :] = (ret * rstd - mu * rstd).astype(BF16)
            state = chunk_decay * state + kv
        fstate_ref[...] = state

    @pl.when(jnp.logical_not(has_fwd))
    def _():
        bwd_sweep()

    @pl.when(jnp.logical_and(has_fwd, has_bwd))
    def _():
        fwd_sweep()
        bwd_sweep()

    @pl.when(jnp.logical_not(has_bwd))
    def _():
        fwd_sweep()


def _retention(lg, vq, kt, batch, seq, block_tokens):
    c = RET_CHUNK
    nb = seq // block_tokens
    cpb = block_tokens // c
    nc = seq // c
    t = batch * seq
    n_pairs = batch * RET_HEADS

    def fwd_idx(p, j):
        pair = jnp.maximum(p - 1, 0)
        return pair // RET_HEADS, pair % RET_HEADS, jnp.where(p == 0, 0, j)

    def bwd_idx(p, j):
        pair = jnp.minimum(p, n_pairs - 1)
        return pair // RET_HEADS, pair % RET_HEADS, jnp.where(p == n_pairs, 0, nb - 1 - j)

    def head_rows(idx_fn):
        def index_map(p, j):
            b, h, blk = idx_fn(p, j)
            return h, b * nb + blk, 0
        return index_map

    def head_major(idx_fn):
        def index_map(p, j):
            b, h, blk = idx_fn(p, j)
            return h, b * nb + blk
        return index_map

    return pl.pallas_call(
        functools.partial(_retention_kernel, n_pairs=n_pairs, n_blocks=nb,
                          chunks_per_block=cpb),
        grid=(n_pairs + 1, nb),
        in_specs=[
            pl.BlockSpec(memory_space=pltpu.SMEM),
            pl.BlockSpec((None, block_tokens, VQ_DIM), head_rows(fwd_idx)),
            pl.BlockSpec((QK_DIM, block_tokens), head_major(fwd_idx)),
            pl.BlockSpec((QK_DIM, block_tokens), head_major(bwd_idx)),
            pl.BlockSpec((None, block_tokens, V_DIM), head_rows(bwd_idx)),
        ],
        out_specs=pl.BlockSpec((None, block_tokens, V_DIM), head_rows(fwd_idx)),
        out_shape=jax.ShapeDtypeStruct((RET_HEADS, t, V_DIM), BF16),
        scratch_shapes=[
            pltpu.VMEM((c, c), F32),
            pltpu.VMEM((c, LANES), BF16),
            pltpu.VMEM((c, LANES), BF16),
            pltpu.VMEM((QK_DIM, c), BF16),
            pltpu.VMEM((QK_DIM, c), BF16),
            pltpu.VMEM((QK_DIM, V_DIM), F32),
            pltpu.VMEM((QK_DIM, V_DIM), F32),
            pltpu.VMEM((2 * nc, QK_DIM, V_DIM), BF16),
        ],
        compiler_params=pltpu.CompilerParams(
            dimension_semantics=("arbitrary", "arbitrary"),
            vmem_limit_bytes=VMEM_LIMIT),
        name="retention",
    )(lg, vq, kt, kt, vq)


def _merge_kernel(u_ref, gb_ref, up_ref, un_ref, ga_ref, gr_ref, z_ref, gsw_ref,
                  wconv_ref, wa_ref, wr_ref, o_ref, *, tiles_per_seq):
    i = pl.program_id(0)
    tm = u_ref.shape[0]
    u = u_ref[...].astype(F32)
    pos = i % tiles_per_seq
    last = BF16_SUBLANES - 1
    u_before = up_ref[last:last + 1, :].astype(F32) * (pos != 0).astype(F32)
    u_after = un_ref[0:1, :].astype(F32) * (pos != tiles_per_seq - 1).astype(F32)
    row = lax.broadcasted_iota(jnp.int32, u.shape, 0)
    u_prev = jnp.where(row == 0, u_before, pltpu.roll(u, 1, axis=0))
    u_next = jnp.where(row == tm - 1, u_after, pltpu.roll(u, tm - 1, axis=0))
    conv = u_prev * wconv_ref[0:1, :] + u * wconv_ref[1:2, :] + u_next * wconv_ref[2:3, :]
    a_in = (gb_ref[...].astype(F32) * conv).astype(BF16)

    def branch_outputs(rows):
        z = jnp.concatenate([z_ref[hh, rows, :] for hh in range(RET_HEADS)], axis=1)
        y_r = jnp.dot(z * gsw_ref[rows, :], wr_ref[...], preferred_element_type=F32)
        y_a = jnp.dot(a_in[rows, :], wa_ref[...], preferred_element_type=F32)
        return y_a, y_r

    blocks = [slice(r, r + SUBBLOCK_ROWS) for r in range(0, tm, SUBBLOCK_ROWS)]
    y_next = branch_outputs(blocks[0])
    for s, rows in enumerate(blocks):
        y_a, y_r = y_next
        if s + 1 < len(blocks):
            y_next = branch_outputs(blocks[s + 1])
        merged = ga_ref[rows, :].astype(F32) * y_a + gr_ref[rows, :].astype(F32) * y_r
        o_ref[rows, :] = merged.astype(BF16)


def _merge(proj, z, w_conv, w_a, w_r, seq, tm):
    t, d = proj.shape[0], D_MODEL
    hb = tm // BF16_SUBLANES
    n_halo_blocks = t // BF16_SUBLANES
    halo = (BF16_SUBLANES, SEC)
    return pl.pallas_call(
        functools.partial(_merge_kernel, tiles_per_seq=seq // tm),
        grid=(t // tm,),
        in_specs=[
            pl.BlockSpec((tm, SEC), lambda i: (i, SEC_U)),
            pl.BlockSpec((tm, SEC), lambda i: (i, SEC_GB)),
            pl.BlockSpec(halo, lambda i: (jnp.maximum(i * hb - 1, 0), SEC_U)),
            pl.BlockSpec(halo, lambda i: (jnp.minimum((i + 1) * hb, n_halo_blocks - 1), SEC_U)),
            pl.BlockSpec((tm, SEC), lambda i: (i, SEC_GA)),
            pl.BlockSpec((tm, SEC), lambda i: (i, SEC_GR)),
            pl.BlockSpec((RET_HEADS, tm, V_DIM), lambda i: (0, i, 0)),
            pl.BlockSpec((tm, RET_V), lambda i: (i, (SEC_GSW * SEC) // RET_V)),
            _const_spec(w_conv.shape),
            _const_spec(w_a.shape),
            _const_spec(w_r.shape),
        ],
        out_specs=pl.BlockSpec((tm, d), lambda i: (i, 0)),
        out_shape=jax.ShapeDtypeStruct((t, d), BF16),
        compiler_params=pltpu.CompilerParams(
            dimension_semantics=("arbitrary",),
            vmem_limit_bytes=VMEM_LIMIT),
        name="merge",
    )(proj, proj, proj, proj, proj, proj, z, proj, w_conv, w_a, w_r)


def _ffn_kernel(x_ref, m_ref, wo_ref, gffn_ref, wg_ref, wu_ref, wd_ref, gfin_ref, o_ref,
                *, final_norm):
    def residual(rows):
        return x_ref[rows, :] + jnp.dot(m_ref[rows, :], wo_ref[...], preferred_element_type=F32)

    def gate_up(x1):
        h2 = (_rms_scale(x1) * gffn_ref[...]).astype(BF16)
        return (jnp.dot(h2, wg_ref[...], preferred_element_type=F32),
                jnp.dot(h2, wu_ref[...], preferred_element_type=F32), x1)

    blocks = [slice(r, r + SUBBLOCK_ROWS) for r in range(0, x_ref.shape[0], SUBBLOCK_ROWS)]
    x1s = [residual(rows) for rows in blocks[:2]]
    gu_next = gate_up(x1s[0])
    for s, rows in enumerate(blocks):
        gate, up, x1 = gu_next
        if s + 2 < len(blocks):
            x1s.append(residual(blocks[s + 2]))
        if s + 1 < len(blocks):
            gu_next = gate_up(x1s[s + 1])
        act = (gate * _sigmoid(gate) * up).astype(BF16)
        y = x1 + jnp.dot(act, wd_ref[...], preferred_element_type=F32)
        o_ref[rows, :] = _rms_scale(y) * gfin_ref[...] if final_norm else y


def _ffn(x2, merged, w_o, g_ffn, w_gate, w_up, w_down, g_final, final_norm, tm):
    t, d = x2.shape
    return pl.pallas_call(
        functools.partial(_ffn_kernel, final_norm=final_norm),
        grid=(t // tm,),
        in_specs=[
            pl.BlockSpec((tm, d), lambda i: (i, 0)),
            pl.BlockSpec((tm, d), lambda i: (i, 0)),
            _const_spec(w_o.shape),
            _const_spec(g_ffn.shape),
            _const_spec(w_gate.shape),
            _const_spec(w_up.shape),
            _const_spec(w_down.shape),
            _const_spec(g_final.shape),
        ],
        out_specs=pl.BlockSpec((tm, d), lambda i: (i, 0)),
        out_shape=jax.ShapeDtypeStruct((t, d), F32),
        compiler_params=pltpu.CompilerParams(
            dimension_semantics=("arbitrary",),
            vmem_limit_bytes=VMEM_LIMIT),
        name="ffn",
    )(x2, merged, w_o, g_ffn, w_gate, w_up, w_down, g_final)


def _rotary_tables(seq, tm):
    freqs = ROPE_BASE ** (-jnp.arange(0, QK_DIM, 2, dtype=F32) / QK_DIM)
    base = (jnp.arange(seq // tm, dtype=F32) * tm)[:, None] * freqs[None, :]
    off = jnp.arange(tm, dtype=F32)[:, None] * freqs[None, :]
    return jnp.cos(base), jnp.sin(base), jnp.cos(off), jnp.sin(off)


def kernel(x, g_mix, w_in, w_conv, dec_f, dec_b, g_ret, w_a_out, w_r_out, w_o,
           g_ffn, w_ff_gate, w_ff_up, w_ff_down, g_final):
    batch, seq, d = x.shape
    depth = w_in.shape[0]
    assert d == D_MODEL and seq % RET_BLOCK_TOKENS == 0 and seq % MERGE_ROWS == 0
    x2 = x.reshape(batch * seq, d)
    rot = _rotary_tables(seq, INPROJ_ROWS)
    for l in range(depth):
        lg = jnp.stack([jax.nn.log_sigmoid(dec_f[l].astype(F32)),
                        jax.nn.log_sigmoid(dec_b[l].astype(F32))])
        side = [w_a_out[l], w_r_out[l], w_o[l], w_ff_gate[l], w_ff_up[l], w_ff_down[l]]
        proj, kt, vq, (w_a, w_r, w_ob, w_gate, w_up, w_down) = _inproj(
            x2, g_mix[l][None, :], w_in[l], rot,
            g_ret[l][None, :].astype(F32), side, tm=INPROJ_ROWS)
        z = _retention(lg, vq, kt, batch, seq, block_tokens=RET_BLOCK_TOKENS)
        merged = _merge(proj, z, w_conv[l], w_a, w_r, seq, tm=MERGE_ROWS)
        x2 = _ffn(x2, merged, w_ob, g_ffn[l][None, :], w_gate, w_up, w_down,
                  g_final[None, :], final_norm=(l == depth - 1), tm=FFN_ROWS)
    return x2.reshape(batch, seq, d)
```

```python
import functools

import jax
import jax.numpy as jnp
from jax import lax
from jax.experimental import pallas as pl
from jax.experimental.pallas import tpu as pltpu

F32 = jnp.float32
BF16 = jnp.bfloat16

D_MODEL = 1024
RET_HEADS = 4
QK_DIM = D_MODEL // RET_HEADS
V_DIM = 2 * QK_DIM
RET_V = RET_HEADS * V_DIM
ROPE_BASE = 10000.0
EPS = 1e-6

SEC = 1024
W_XC, W_GB, W_GC, W_Q, W_K, W_V, W_GSW, W_GA, W_GR, N_W_SEC = 0, 1, 2, 3, 4, 5, 7, 9, 10, 11
SEC_U, SEC_GB, SEC_GSW, SEC_GA, SEC_GR, N_OUT_SEC = 0, 1, 2, 4, 5, 6
VQ_DIM = V_DIM + QK_DIM

RET_CHUNK = 256
SIDE_CAST_STEPS = 16
SUBBLOCK_ROWS = 256
INPROJ_ROWS = 256
RET_BLOCK_TOKENS = 4096
MERGE_ROWS = 512
FFN_ROWS = 512
FFN_INPUT_BUFFERS = 3
LANES = 128
BF16_SUBLANES = 16

VMEM_LIMIT = 56 * 1024 * 1024


def _sigmoid(x):
    return 1.0 / (1.0 + jnp.exp(-x))


def _rms_scale(x):
    return x * lax.rsqrt(jnp.mean(x * x, axis=-1, keepdims=True) + EPS)


def _const_spec(shape):
    return pl.BlockSpec(shape, lambda *_: (0,) * len(shape), pipeline_mode=pl.Buffered(1))


def _inproj_kernel(x_ref, g_ref, w_ref, cos_base_ref, sin_base_ref, cos_off_ref, sin_off_ref,
                   gret_ref, *rest, n_pos_blocks, n_side):
    side_in = rest[:n_side]
    o_ref, kt_ref, vq_ref = rest[n_side:n_side + 3]
    side_out = rest[n_side + 3:2 * n_side + 3]
    w_bf16_ref = rest[2 * n_side + 3]
    step = pl.program_id(0) - N_W_SEC

    @pl.when(step < 0)
    def _():
        w_bf16_ref[pl.program_id(0)] = w_ref[...].astype(BF16)

    @pl.when(step >= 0)
    def _():
        h = (_rms_scale(x_ref[...]) * g_ref[...]).astype(BF16)
        half = QK_DIM // 2
        pos_block = pl.ds(step % n_pos_blocks, 1)
        cos_b = cos_base_ref[pos_block, :]
        sin_b = sin_base_ref[pos_block, :]
        cos = cos_b * cos_off_ref[...] - sin_b * sin_off_ref[...]
        sin = sin_b * cos_off_ref[...] + cos_b * sin_off_ref[...]

        def project(first_sec, n_sec):
            return jnp.concatenate(
                [jnp.dot(h, w_bf16_ref[sec], preferred_element_type=F32)
                 for sec in range(first_sec, first_sec + n_sec)], axis=1)

        def out_cols(sec, n_sec=1):
            return slice(sec * SEC, (sec + n_sec) * SEC)

        qk = project(W_Q, 2)
        for sec in range(2):
            for hh in range(RET_HEADS):
                lo = sec * SEC + hh * QK_DIM
                t1 = qk[:, lo:lo + half]
                t2 = qk[:, lo + half:lo + QK_DIM]
                r1 = t1 * cos - t2 * sin
                r2 = t1 * sin + t2 * cos
                if sec == 0:
                    scale = QK_DIM ** -0.5
                    vq_ref[hh, :, V_DIM:V_DIM + half] = (r1 * scale).astype(BF16)
                    vq_ref[hh, :, V_DIM + half:VQ_DIM] = (r2 * scale).astype(BF16)
                else:
                    out = hh * QK_DIM
                    kt_ref[out:out + half, :] = r1.T.astype(BF16)
                    kt_ref[out + half:out + QK_DIM, :] = r2.T.astype(BF16)

        gsw = project(W_GSW, 2)
        o_ref[:, out_cols(SEC_GSW, 2)] = (gsw * _sigmoid(gsw) * gret_ref[...]).astype(BF16)

        gates = project(W_GA, 2)
        o_ref[:, out_cols(SEC_GA, 2)] = _sigmoid(gates).astype(BF16)

        conv_in = project(W_XC, 3)
        o_ref[:, out_cols(SEC_U)] = (conv_in[:, out_cols(W_GC)] * conv_in[:, out_cols(W_XC)]
                                     ).astype(BF16)
        o_ref[:, out_cols(SEC_GB)] = conv_in[:, out_cols(W_GB)].astype(BF16)

        values = project(W_V, 2)
        for hh in range(RET_HEADS):
            vq_ref[hh, :, :V_DIM] = values[:, hh * V_DIM:(hh + 1) * V_DIM].astype(BF16)

    @pl.when(jnp.logical_and(step >= 0, step < SIDE_CAST_STEPS))
    def _():
        for src, dst in zip(side_in, side_out):
            dst[...] = src[...].astype(BF16)


def _inproj(x2, g_mix, w_in, rot, g_ret, side_weights, tm):
    t, d = x2.shape
    cos_base, sin_base, cos_off, sin_off = rot
    assert cos_off.shape[0] == tm and w_in.shape[1] == N_W_SEC * SEC
    n_out = N_OUT_SEC * SEC
    n_steps = t // tm
    assert n_steps >= SIDE_CAST_STEPS

    def tile(i):
        return jnp.maximum(i - N_W_SEC, 0)

    def slab_spec(w):
        rows = w.shape[0] // SIDE_CAST_STEPS
        assert rows * SIDE_CAST_STEPS == w.shape[0] and rows % BF16_SUBLANES == 0
        return pl.BlockSpec((rows, w.shape[1]),
                            lambda i: (jnp.minimum(tile(i), SIDE_CAST_STEPS - 1), 0))

    side_specs = [slab_spec(w) for w in side_weights]
    outs = pl.pallas_call(
        functools.partial(_inproj_kernel, n_pos_blocks=cos_base.shape[0],
                          n_side=len(side_weights)),
        grid=(N_W_SEC + n_steps,),
        in_specs=[
            pl.BlockSpec((tm, d), lambda i: (tile(i), 0)),
            _const_spec(g_mix.shape),
            pl.BlockSpec((d, SEC), lambda i: (0, jnp.minimum(i, N_W_SEC - 1))),
            _const_spec(cos_base.shape),
            _const_spec(sin_base.shape),
            _const_spec(cos_off.shape),
            _const_spec(sin_off.shape),
            _const_spec(g_ret.shape),
        ] + side_specs,
        out_specs=[
            pl.BlockSpec((tm, n_out), lambda i: (tile(i), 0)),
            pl.BlockSpec((RET_HEADS * QK_DIM, tm), lambda i: (0, tile(i))),
            pl.BlockSpec((RET_HEADS, tm, VQ_DIM), lambda i: (0, tile(i), 0)),
        ] + side_specs,
        out_shape=[
            jax.ShapeDtypeStruct((t, n_out), BF16),
            jax.ShapeDtypeStruct((RET_HEADS * QK_DIM, t), BF16),
            jax.ShapeDtypeStruct((RET_HEADS, t, VQ_DIM), BF16),
        ] + [jax.ShapeDtypeStruct(w.shape, BF16) for w in side_weights],
        scratch_shapes=[pltpu.VMEM((N_W_SEC, d, SEC), BF16)],
        compiler_params=pltpu.CompilerParams(
            dimension_semantics=("arbitrary",),
            vmem_limit_bytes=VMEM_LIMIT),
        name="inproj",
    )(x2, g_mix, w_in, cos_base, sin_base, cos_off, sin_off, g_ret, *side_weights)
    return outs[0], outs[1], outs[2], outs[3:]


def _retention_kernel(lg_ref, vq_ref, ktf_ref, ktb_ref, vb_ref, z_ref,
                      decay_ref, xif_ref, xib_ref, zetaf_ref, zetab_ref,
                      fstate_ref, bstate_ref, rall_ref,
                      *, n_pairs, n_blocks, chunks_per_block):
    c = RET_CHUNK
    cpb = chunks_per_block
    n_chunks = n_blocks * cpb
    p = pl.program_id(0)
    j = pl.program_id(1)
    head_f = jnp.maximum(p - 1, 0) % RET_HEADS
    head_b = jnp.minimum(p, n_pairs - 1) % RET_HEADS
    lg_f = lg_ref[0, head_f]
    lg_fb = lg_ref[1, head_f]
    lg_b = lg_ref[1, head_b]
    has_fwd = p >= 1
    has_bwd = p < n_pairs

    @pl.when(jnp.logical_and(j == 0, has_fwd))
    def _():
        row = lax.broadcasted_iota(jnp.int32, (c, c), 0).astype(F32)
        col = lax.broadcasted_iota(jnp.int32, (c, c), 1).astype(F32)
        diff = row - col
        decay_ref[...] = jnp.where(diff >= 0.0,
                                   jnp.exp(jnp.maximum(diff, 0.0) * lg_f),
                                   jnp.exp(jnp.maximum(-diff, 0.0) * lg_fb))
        zetaf_ref[...] = jnp.exp((c - 1.0 - col) * lg_f).astype(BF16)
        idx = lax.broadcasted_iota(jnp.int32, (c, LANES), 0).astype(F32)
        xif_ref[...] = jnp.exp((idx + 1.0) * lg_f).astype(BF16)
        xib_ref[...] = jnp.exp((c - idx) * lg_fb).astype(BF16)
        fstate_ref[...] = jnp.zeros_like(fstate_ref)

    @pl.when(jnp.logical_and(j == 0, has_bwd))
    def _():
        col = lax.broadcasted_iota(jnp.int32, (c, c), 1).astype(F32)
        zetab_ref[...] = jnp.exp(col * lg_b).astype(BF16)
        bstate_ref[...] = jnp.zeros_like(bstate_ref)

    def row_scaled(a, scale_ref):
        scale = scale_ref[...]
        return jnp.concatenate(
            [a[:, n * LANES:(n + 1) * LANES] * scale
             for n in range(a.shape[1] // LANES)], axis=1)

    def decayed_kv(kt_ref, v_ref, zeta_ref, ci):
        rows = slice(ci * c, (ci + 1) * c)
        return jnp.dot(kt_ref[:, rows] * zeta_ref[...], v_ref[rows, :V_DIM],
                       preferred_element_type=F32)

    def bwd_sweep():
        blk = n_blocks - 1 - j
        slot = (p % 2) * n_chunks
        chunk_decay = jnp.exp(jnp.zeros((1, V_DIM), F32) + c * lg_b)
        state = bstate_ref[...]
        for ci in reversed(range(cpb)):
            rall_ref[slot + blk * cpb + ci] = state.astype(BF16)
            state = chunk_decay * state + decayed_kv(ktb_ref, vb_ref, zetab_ref, ci)
        bstate_ref[...] = state

    def fwd_sweep():
        slot = ((p - 1) % 2) * n_chunks
        chunk_decay = jnp.exp(jnp.zeros((1, V_DIM), F32) + c * lg_f)

        def decayed_scores(ci):
            rows = slice(ci * c, (ci + 1) * c)
            scores = jnp.dot(vq_ref[rows, V_DIM:], ktf_ref[:, rows],
                             preferred_element_type=F32)
            return (scores * decay_ref[...]).astype(BF16)

        state = fstate_ref[...]
        pmat_next = decayed_scores(0)
        kv_next = decayed_kv(ktf_ref, vq_ref, zetaf_ref, 0)
        for ci in range(cpb):
            rows = slice(ci * c, (ci + 1) * c)
            pmat, kv = pmat_next, kv_next
            if ci + 1 < cpb:
                pmat_next = decayed_scores(ci + 1)
                kv_next = decayed_kv(ktf_ref, vq_ref, zetaf_ref, ci + 1)
            q = vq_ref[rows, V_DIM:]
            lhs = jnp.concatenate([pmat, row_scaled(q, xib_ref), row_scaled(q, xif_ref)],
                                  axis=1)
            rhs = jnp.concatenate([vq_ref[rows, :V_DIM], rall_ref[slot + j * cpb + ci],
                                   state.astype(BF16)], axis=0)
            ret = jnp.dot(lhs, rhs, preferred_element_type=F32)
            mu = jnp.mean(ret, axis=-1, keepdims=True)
            var = jnp.mean(jnp.square(ret - mu), axis=-1, keepdims=True)
            rstd = lax.rsqrt(var + EPS)
            z_ref[rows, :] = (ret * rstd - mu * rstd).astype(BF16)
            state = chunk_decay * state + kv
        fstate_ref[...] = state

    @pl.when(jnp.logical_not(has_fwd))
    def _():
        bwd_sweep()

    @pl.when(jnp.logical_and(has_fwd, has_bwd))
    def _():
        fwd_sweep()
        bwd_sweep()

    @pl.when(jnp.logical_not(has_bwd))
    def _():
        fwd_sweep()


def _retention(lg, vq, kt, batch, seq, block_tokens):
    c = RET_CHUNK
    nb = seq // block_tokens
    cpb = block_tokens // c
    nc = seq // c
    t = batch * seq
    n_pairs = batch * RET_HEADS

    def fwd_idx(p, j):
        pair = jnp.maximum(p - 1, 0)
        return pair // RET_HEADS, pair % RET_HEADS, jnp.where(p == 0, 0, j)

    def bwd_idx(p, j):
        pair = jnp.minimum(p, n_pairs - 1)
        return pair // RET_HEADS, pair % RET_HEADS, jnp.where(p == n_pairs, 0, nb - 1 - j)

    def head_rows(idx_fn):
        def index_map(p, j):
            b, h, blk = idx_fn(p, j)
            return h, b * nb + blk, 0
        return index_map

    def head_major(idx_fn):
        def index_map(p, j):
            b, h, blk = idx_fn(p, j)
            return h, b * nb + blk
        return index_map

    return pl.pallas_call(
        functools.partial(_retention_kernel, n_pairs=n_pairs, n_blocks=nb,
                          chunks_per_block=cpb),
        grid=(n_pairs + 1, nb),
        in_specs=[
            pl.BlockSpec(memory_space=pltpu.SMEM),
            pl.BlockSpec((None, block_tokens, VQ_DIM), head_rows(fwd_idx)),
            pl.BlockSpec((QK_DIM, block_tokens), head_major(fwd_idx)),
            pl.BlockSpec((QK_DIM, block_tokens), head_major(bwd_idx)),
            pl.BlockSpec((None, block_tokens, V_DIM), head_rows(bwd_idx)),
        ],
        out_specs=pl.BlockSpec((None, block_tokens, V_DIM), head_rows(fwd_idx)),
        out_shape=jax.ShapeDtypeStruct((RET_HEADS, t, V_DIM), BF16),
        scratch_shapes=[
            pltpu.VMEM((c, c), F32),
            pltpu.VMEM((c, LANES), BF16),
            pltpu.VMEM((c, LANES), BF16),
            pltpu.VMEM((QK_DIM, c), BF16),
            pltpu.VMEM((QK_DIM, c), BF16),
            pltpu.VMEM((QK_DIM, V_DIM), F32),
            pltpu.VMEM((QK_DIM, V_DIM), F32),
            pltpu.VMEM((2 * nc, QK_DIM, V_DIM), BF16),
        ],
        compiler_params=pltpu.CompilerParams(
            dimension_semantics=("arbitrary", "arbitrary"),
            vmem_limit_bytes=VMEM_LIMIT),
        name="retention",
    )(lg, vq, kt, kt, vq)


def _merge_kernel(u_ref, gb_ref, up_ref, un_ref, ga_ref, gr_ref, z_ref, gsw_ref,
                  wconv_ref, wa_ref, wr_ref, o_ref, *, tiles_per_seq):
    i = pl.program_id(0)
    tm = u_ref.shape[0]
    u = u_ref[...].astype(F32)
    pos = i % tiles_per_seq
    last = BF16_SUBLANES - 1
    u_before = up_ref[last:last + 1, :].astype(F32) * (pos != 0).astype(F32)
    u_after = un_ref[0:1, :].astype(F32) * (pos != tiles_per_seq - 1).astype(F32)
    row = lax.broadcasted_iota(jnp.int32, u.shape, 0)
    u_prev = jnp.where(row == 0, u_before, pltpu.roll(u, 1, axis=0))
    u_next = jnp.where(row == tm - 1, u_after, pltpu.roll(u, tm - 1, axis=0))
    conv = u_prev * wconv_ref[0:1, :] + u * wconv_ref[1:2, :] + u_next * wconv_ref[2:3, :]
    a_in = (gb_ref[...].astype(F32) * conv).astype(BF16)

    def branch_outputs(rows):
        z = jnp.concatenate([z_ref[hh, rows, :] for hh in range(RET_HEADS)], axis=1)
        y_r = jnp.dot(z * gsw_ref[rows, :], wr_ref[...], preferred_element_type=F32)
        y_a = jnp.dot(a_in[rows, :], wa_ref[...], preferred_element_type=F32)
        return y_a, y_r

    blocks = [slice(r, r + SUBBLOCK_ROWS) for r in range(0, tm, SUBBLOCK_ROWS)]
    y_next = branch_outputs(blocks[0])
    for s, rows in enumerate(blocks):
        y_a, y_r = y_next
        if s + 1 < len(blocks):
            y_next = branch_outputs(blocks[s + 1])
        merged = ga_ref[rows, :].astype(F32) * y_a + gr_ref[rows, :].astype(F32) * y_r
        o_ref[rows, :] = merged.astype(BF16)


def _merge(proj, z, w_conv, w_a, w_r, seq, tm):
    t, d = proj.shape[0], D_MODEL
    hb = tm // BF16_SUBLANES
    n_halo_blocks = t // BF16_SUBLANES
    halo = (BF16_SUBLANES, SEC)
    return pl.pallas_call(
        functools.partial(_merge_kernel, tiles_per_seq=seq // tm),
        grid=(t // tm,),
        in_specs=[
            pl.BlockSpec((tm, SEC), lambda i: (i, SEC_U)),
            pl.BlockSpec((tm, SEC), lambda i: (i, SEC_GB)),
            pl.BlockSpec(halo, lambda i: (jnp.maximum(i * hb - 1, 0), SEC_U)),
            pl.BlockSpec(halo, lambda i: (jnp.minimum((i + 1) * hb, n_halo_blocks - 1), SEC_U)),
            pl.BlockSpec((tm, SEC), lambda i: (i, SEC_GA)),
            pl.BlockSpec((tm, SEC), lambda i: (i, SEC_GR)),
            pl.BlockSpec((RET_HEADS, tm, V_DIM), lambda i: (0, i, 0)),
            pl.BlockSpec((tm, RET_V), lambda i: (i, (SEC_GSW * SEC) // RET_V)),
            _const_spec(w_conv.shape),
            _const_spec(w_a.shape),
            _const_spec(w_r.shape),
        ],
        out_specs=pl.BlockSpec((tm, d), lambda i: (i, 0)),
        out_shape=jax.ShapeDtypeStruct((t, d), BF16),
        compiler_params=pltpu.CompilerParams(
            dimension_semantics=("arbitrary",),
            vmem_limit_bytes=VMEM_LIMIT),
        name="merge",
    )(proj, proj, proj, proj, proj, proj, z, proj, w_conv, w_a, w_r)


def _ffn_kernel(x_ref, m_ref, wo_ref, gffn_ref, wg_ref, wu_ref, wd_ref, gfin_ref, o_ref,
                *, final_norm):
    def residual(rows):
        return x_ref[rows, :] + jnp.dot(m_ref[rows, :], wo_ref[...], preferred_element_type=F32)

    def gate_up(x1):
        h2 = (_rms_scale(x1) * gffn_ref[...]).astype(BF16)
        return (jnp.dot(h2, wg_ref[...], preferred_element_type=F32),
                jnp.dot(h2, wu_ref[...], preferred_element_type=F32), x1)

    blocks = [slice(r, r + SUBBLOCK_ROWS) for r in range(0, x_ref.shape[0], SUBBLOCK_ROWS)]
    x1s = [residual(rows) for rows in blocks[:2]]
    gu_next = gate_up(x1s[0])
    for s, rows in enumerate(blocks):
        gate, up, x1 = gu_next
        if s + 2 < len(blocks):
            x1s.append(residual(blocks[s + 2]))
        if s + 1 < len(blocks):
            gu_next = gate_up(x1s[s + 1])
        act = (gate * _sigmoid(gate) * up).astype(BF16)
        y = x1 + jnp.dot(act, wd_ref[...], preferred_element_type=F32)
        o_ref[rows, :] = _rms_scale(y) * gfin_ref[...] if final_norm else y


def _ffn_pipelined(x_hbm, m_hbm, wo_ref, gffn_ref, wg_ref, wu_ref, wd_ref, gfin_ref, o_hbm,
                   *, final_norm, tm):
    t, d = x_hbm.shape

    def tile_body(x_ref, m_ref, o_ref):
        _ffn_kernel(x_ref, m_ref, wo_ref, gffn_ref, wg_ref, wu_ref, wd_ref, gfin_ref, o_ref,
                    final_norm=final_norm)

    def tile_spec(buffers):
        return pl.BlockSpec((tm, d), lambda i: (i, 0), pipeline_mode=pl.Buffered(buffers))

    pltpu.emit_pipeline(
        tile_body, grid=(t // tm,),
        in_specs=[tile_spec(FFN_INPUT_BUFFERS), tile_spec(FFN_INPUT_BUFFERS)],
        out_specs=[tile_spec(2)],
        trace_scopes=False,
    )(x_hbm, m_hbm, o_hbm)


def _ffn(x2, merged, w_o, g_ffn, w_gate, w_up, w_down, g_final, final_norm, tm):
    t, d = x2.shape
    hbm = pl.BlockSpec(memory_space=pl.ANY)
    vmem = pl.BlockSpec(memory_space=pltpu.VMEM)
    return pl.pallas_call(
        functools.partial(_ffn_pipelined, final_norm=final_norm, tm=tm),
        in_specs=[hbm, hbm, vmem, vmem, vmem, vmem, vmem, vmem],
        out_specs=hbm,
        out_shape=jax.ShapeDtypeStruct((t, d), F32),
        compiler_params=pltpu.CompilerParams(vmem_limit_bytes=VMEM_LIMIT),
        name="ffn",
    )(x2, merged, w_o, g_ffn, w_gate, w_up, w_down, g_final)


def _rotary_tables(seq, tm):
    freqs = ROPE_BASE ** (-jnp.arange(0, QK_DIM, 2, dtype=F32) / QK_DIM)
    base = (jnp.arange(seq // tm, dtype=F32) * tm)[:, None] * freqs[None, :]
    off = jnp.arange(tm, dtype=F32)[:, None] * freqs[None, :]
    return jnp.cos(base), jnp.sin(base), jnp.cos(off), jnp.sin(off)


def kernel(x, g_mix, w_in, w_conv, dec_f, dec_b, g_ret, w_a_out, w_r_out, w_o,
           g_ffn, w_ff_gate, w_ff_up, w_ff_down, g_final):
    batch, seq, d = x.shape
    depth = w_in.shape[0]
    assert d == D_MODEL and seq % RET_BLOCK_TOKENS == 0 and seq % MERGE_ROWS == 0
    x2 = x.reshape(batch * seq, d)
    rot = _rotary_tables(seq, INPROJ_ROWS)
    for l in range(depth):
        lg = jnp.stack([jax.nn.log_sigmoid(dec_f[l].astype(F32)),
                        jax.nn.log_sigmoid(dec_b[l].astype(F32))])
        side = [w_a_out[l], w_r_out[l], w_o[l], w_ff_gate[l], w_ff_up[l], w_ff_down[l]]
        proj, kt, vq, (w_a, w_r, w_ob, w_gate, w_up, w_down) = _inproj(
            x2, g_mix[l][None, :], w_in[l], rot,
            g_ret[l][None, :].astype(F32), side, tm=INPROJ_ROWS)
        z = _retention(lg, vq, kt, batch, seq, block_tokens=RET_BLOCK_TOKENS)
        merged = _merge(proj, z, w_conv[l], w_a, w_r, seq, tm=MERGE_ROWS)
        x2 = _ffn(x2, merged, w_ob, g_ffn[l][None, :], w_gate, w_up, w_down,
                  g_final[None, :], final_norm=(l == depth - 1), tm=FFN_ROWS)
    return x2.reshape(batch, seq, d)
```

```python
import functools

import jax
import jax.numpy as jnp
from jax import lax
from jax.experimental import pallas as pl
from jax.experimental.pallas import tpu as pltpu

F32 = jnp.float32
BF16 = jnp.bfloat16

D_MODEL = 1024
RET_HEADS = 4
QK_DIM = D_MODEL // RET_HEADS
V_DIM = 2 * QK_DIM
RET_V = RET_HEADS * V_DIM
ROPE_BASE = 10000.0
EPS = 1e-6

SEC = 1024
W_XC, W_GB, W_GC, W_Q, W_K, W_V, W_GSW, W_GA, W_GR, N_W_SEC = 0, 1, 2, 3, 4, 5, 7, 9, 10, 11
SEC_U, SEC_GB, SEC_GSW, SEC_GA, SEC_GR, N_OUT_SEC = 0, 1, 2, 4, 5, 6
VQ_DIM = V_DIM + QK_DIM

RET_CHUNK = 256
SIDE_CAST_STEPS = 16
SUBBLOCK_ROWS = 256
INPROJ_ROWS = 256
RET_BLOCK_TOKENS = 4096
MERGE_ROWS = 512
FFN_ROWS = 512
LANES = 128
BF16_SUBLANES = 16

VMEM_LIMIT = 56 * 1024 * 1024


def _sigmoid(x):
    return 1.0 / (1.0 + jnp.exp(-x))


def _rms_scale(x):
    return x * lax.rsqrt(jnp.mean(x * x, axis=-1, keepdims=True) + EPS)


def _const_spec(shape):
    return pl.BlockSpec(shape, lambda *_: (0,) * len(shape), pipeline_mode=pl.Buffered(1))


def _inproj_kernel(x_ref, g_ref, w_ref, cos_base_ref, sin_base_ref, cos_off_ref, sin_off_ref,
                   gret_ref, *rest, n_pos_blocks, n_side):
    side_in = rest[:n_side]
    o_ref, kt_ref, vq_ref = rest[n_side:n_side + 3]
    side_out = rest[n_side + 3:2 * n_side + 3]
    w_bf16_ref = rest[2 * n_side + 3]
    step = pl.program_id(0) - N_W_SEC

    @pl.when(step < 0)
    def _():
        w_bf16_ref[pl.program_id(0)] = w_ref[...].astype(BF16)

    @pl.when(step >= 0)
    def _():
        h = (_rms_scale(x_ref[...]) * g_ref[...]).astype(BF16)
        half = QK_DIM // 2
        pos_block = pl.ds(step % n_pos_blocks, 1)
        cos_b = cos_base_ref[pos_block, :]
        sin_b = sin_base_ref[pos_block, :]
        cos = cos_b * cos_off_ref[...] - sin_b * sin_off_ref[...]
        sin = sin_b * cos_off_ref[...] + cos_b * sin_off_ref[...]

        def project(first_sec, n_sec):
            return jnp.concatenate(
                [jnp.dot(h, w_bf16_ref[sec], preferred_element_type=F32)
                 for sec in range(first_sec, first_sec + n_sec)], axis=1)

        def out_cols(sec, n_sec=1):
            return slice(sec * SEC, (sec + n_sec) * SEC)

        qk = project(W_Q, 2)
        for sec in range(2):
            for hh in range(RET_HEADS):
                lo = sec * SEC + hh * QK_DIM
                t1 = qk[:, lo:lo + half]
                t2 = qk[:, lo + half:lo + QK_DIM]
                r1 = t1 * cos - t2 * sin
                r2 = t1 * sin + t2 * cos
                if sec == 0:
                    scale = QK_DIM ** -0.5
                    vq_ref[hh, :, V_DIM:V_DIM + half] = (r1 * scale).astype(BF16)
                    vq_ref[hh, :, V_DIM + half:VQ_DIM] = (r2 * scale).astype(BF16)
                else:
                    out = hh * QK_DIM
                    kt_ref[out:out + half, :] = r1.T.astype(BF16)
                    kt_ref[out + half:out + QK_DIM, :] = r2.T.astype(BF16)

        gsw = project(W_GSW, 2)
        o_ref[:, out_cols(SEC_GSW, 2)] = (gsw * _sigmoid(gsw) * gret_ref[...]).astype(BF16)

        gates = project(W_GA, 2)
        o_ref[:, out_cols(SEC_GA, 2)] = _sigmoid(gates).astype(BF16)

        conv_in = project(W_XC, 3)
        o_ref[:, out_cols(SEC_U)] = (conv_in[:, out_cols(W_GC)] * conv_in[:, out_cols(W_XC)]
                                     ).astype(BF16)
        o_ref[:, out_cols(SEC_GB)] = conv_in[:, out_cols(W_GB)].astype(BF16)

        values = project(W_V, 2)
        for hh in range(RET_HEADS):
            vq_ref[hh, :, :V_DIM] = values[:, hh * V_DIM:(hh + 1) * V_DIM].astype(BF16)

    @pl.when(jnp.logical_and(step >= 0, step < SIDE_CAST_STEPS))
    def _():
        for src, dst in zip(side_in, side_out):
            dst[...] = src[...].astype(BF16)


def _inproj(x2, g_mix, w_in, rot, g_ret, side_weights, tm):
    t, d = x2.shape
    cos_base, sin_base, cos_off, sin_off = rot
    assert cos_off.shape[0] == tm and w_in.shape[1] == N_W_SEC * SEC
    n_out = N_OUT_SEC * SEC
    n_steps = t // tm
    assert n_steps >= SIDE_CAST_STEPS

    def tile(i):
        return jnp.maximum(i - N_W_SEC, 0)

    def slab_spec(w):
        rows = w.shape[0] // SIDE_CAST_STEPS
        assert rows * SIDE_CAST_STEPS == w.shape[0] and rows % BF16_SUBLANES == 0
        return pl.BlockSpec((rows, w.shape[1]),
                            lambda i: (jnp.minimum(tile(i), SIDE_CAST_STEPS - 1), 0))

    side_specs = [slab_spec(w) for w in side_weights]
    outs = pl.pallas_call(
        functools.partial(_inproj_kernel, n_pos_blocks=cos_base.shape[0],
                          n_side=len(side_weights)),
        grid=(N_W_SEC + n_steps,),
        in_specs=[
            pl.BlockSpec((tm, d), lambda i: (tile(i), 0)),
            _const_spec(g_mix.shape),
            pl.BlockSpec((d, SEC), lambda i: (0, jnp.minimum(i, N_W_SEC - 1))),
            _const_spec(cos_base.shape),
            _const_spec(sin_base.shape),
            _const_spec(cos_off.shape),
            _const_spec(sin_off.shape),
            _const_spec(g_ret.shape),
        ] + side_specs,
        out_specs=[
            pl.BlockSpec((tm, n_out), lambda i: (tile(i), 0)),
            pl.BlockSpec((RET_HEADS * QK_DIM, tm), lambda i: (0, tile(i))),
            pl.BlockSpec((RET_HEADS, tm, VQ_DIM), lambda i: (0, tile(i), 0)),
        ] + side_specs,
        out_shape=[
            jax.ShapeDtypeStruct((t, n_out), BF16),
            jax.ShapeDtypeStruct((RET_HEADS * QK_DIM, t), BF16),
            jax.ShapeDtypeStruct((RET_HEADS, t, VQ_DIM), BF16),
        ] + [jax.ShapeDtypeStruct(w.shape, BF16) for w in side_weights],
        scratch_shapes=[pltpu.VMEM((N_W_SEC, d, SEC), BF16)],
        compiler_params=pltpu.CompilerParams(
            dimension_semantics=("arbitrary",),
            vmem_limit_bytes=VMEM_LIMIT),
        name="inproj",
    )(x2, g_mix, w_in, cos_base, sin_base, cos_off, sin_off, g_ret, *side_weights)
    return outs[0], outs[1], outs[2], outs[3:]


def _retention_kernel(lg_ref, vq_ref, ktf_ref, ktb_ref, vb_ref, z_ref,
                      decay_ref, xif_ref, xib_ref, zetaf_ref, zetab_ref,
                      fstate_ref, bstate_ref, rall_ref,
                      *, n_pairs, n_blocks, chunks_per_block):
    c = RET_CHUNK
    cpb = chunks_per_block
    n_chunks = n_blocks * cpb
    p = pl.program_id(0)
    j = pl.program_id(1)
    head_f = jnp.maximum(p - 1, 0) % RET_HEADS
    head_b = jnp.minimum(p, n_pairs - 1) % RET_HEADS
    lg_f = lg_ref[0, head_f]
    lg_fb = lg_ref[1, head_f]
    lg_b = lg_ref[1, head_b]
    has_fwd = p >= 1
    has_bwd = p < n_pairs

    @pl.when(jnp.logical_and(j == 0, has_fwd))
    def _():
        row = lax.broadcasted_iota(jnp.int32, (c, c), 0).astype(F32)
        col = lax.broadcasted_iota(jnp.int32, (c, c), 1).astype(F32)
        diff = row - col
        decay_ref[...] = jnp.where(diff >= 0.0,
                                   jnp.exp(jnp.maximum(diff, 0.0) * lg_f),
                                   jnp.exp(jnp.maximum(-diff, 0.0) * lg_fb))
        zetaf_ref[...] = jnp.exp((c - 1.0 - col) * lg_f).astype(BF16)
        idx = lax.broadcasted_iota(jnp.int32, (c, LANES), 0).astype(F32)
        xif_ref[...] = jnp.exp((idx + 1.0) * lg_f).astype(BF16)
        xib_ref[...] = jnp.exp((c - idx) * lg_fb).astype(BF16)
        fstate_ref[...] = jnp.zeros_like(fstate_ref)

    @pl.when(jnp.logical_and(j == 0, has_bwd))
    def _():
        col = lax.broadcasted_iota(jnp.int32, (c, c), 1).astype(F32)
        zetab_ref[...] = jnp.exp(col * lg_b).astype(BF16)
        bstate_ref[...] = jnp.zeros_like(bstate_ref)

    def row_scaled(a, scale_ref):
        scale = scale_ref[...]
        return jnp.concatenate(
            [a[:, n * LANES:(n + 1) * LANES] * scale
             for n in range(a.shape[1] // LANES)], axis=1)

    def decayed_kv(kt_ref, v_ref, zeta_ref, ci):
        rows = slice(ci * c, (ci + 1) * c)
        return jnp.dot(kt_ref[:, rows] * zeta_ref[...], v_ref[rows, :V_DIM],
                       preferred_element_type=F32)

    def bwd_sweep():
        blk = n_blocks - 1 - j
        slot = (p % 2) * n_chunks
        chunk_decay = jnp.exp(jnp.zeros((1, V_DIM), F32) + c * lg_b)
        state = bstate_ref[...]
        for ci in reversed(range(cpb)):
            rall_ref[slot + blk * cpb + ci] = state.astype(BF16)
            state = chunk_decay * state + decayed_kv(ktb_ref, vb_ref, zetab_ref, ci)
        bstate_ref[...] = state

    def fwd_sweep():
        slot = ((p - 1) % 2) * n_chunks
        chunk_decay = jnp.exp(jnp.zeros((1, V_DIM), F32) + c * lg_f)

        def decayed_scores(ci):
            rows = slice(ci * c, (ci + 1) * c)
            scores = jnp.dot(vq_ref[rows, V_DIM:], ktf_ref[:, rows],
                             preferred_element_type=F32)
            return (scores * decay_ref[...]).astype(BF16)

        state = fstate_ref[...]
        pmat_next = decayed_scores(0)
        kv_next = decayed_kv(ktf_ref, vq_ref, zetaf_ref, 0)
        for ci in range(cpb):
            rows = slice(ci * c, (ci + 1) * c)
            pmat, kv = pmat_next, kv_next
            if ci + 1 < cpb:
                pmat_next = decayed_scores(ci + 1)
                kv_next = decayed_kv(ktf_ref, vq_ref, zetaf_ref, ci + 1)
            q = vq_ref[rows, V_DIM:]
            lhs = jnp.concatenate([pmat, row_scaled(q, xib_ref), row_scaled(q, xif_ref)],
                                  axis=1)
            rhs = jnp.concatenate([vq_ref[rows, :V_DIM], rall_ref[slot + j * cpb + ci],
                                   state.astype(BF16)], axis=0)
            ret = jnp.dot(lhs, rhs, preferred_element_type=F32)
            mu = jnp.mean(ret, axis=-1, keepdims=True)
            var = jnp.mean(jnp.square(ret - mu), axis=-1, keepdims=True)
            rstd = lax.rsqrt(var + EPS)
            z_ref[rows, :] = (ret * rstd - mu * rstd).astype(BF16)
            state = chunk_decay * state + kv
        fstate_ref[...] = state

    @pl.when(jnp.logical_not(has_fwd))
    def _():
        bwd_sweep()

    @pl.when(jnp.logical_and(has_fwd, has_bwd))
    def _():
        fwd_sweep()
        bwd_sweep()

    @pl.when(jnp.logical_not(has_bwd))
    def _():
        fwd_sweep()


def _retention(lg, vq, kt, batch, seq, block_tokens):
    c = RET_CHUNK
    nb = seq // block_tokens
    cpb = block_tokens // c
    nc = seq // c
    t = batch * seq
    n_pairs = batch * RET_HEADS

    def fwd_idx(p, j):
        pair = jnp.maximum(p - 1, 0)
        return pair // RET_HEADS, pair % RET_HEADS, jnp.where(p == 0, 0, j)

    def bwd_idx(p, j):
        pair = jnp.minimum(p, n_pairs - 1)
        return pair // RET_HEADS, pair % RET_HEADS, jnp.where(p == n_pairs, 0, nb - 1 - j)

    def head_rows(idx_fn):
        def index_map(p, j):
            b, h, blk = idx_fn(p, j)
            return h, b * nb + blk, 0
        return index_map

    def head_major(idx_fn):
        def index_map(p, j):
            b, h, blk = idx_fn(p, j)
            return h, b * nb + blk
        return index_map

    return pl.pallas_call(
        functools.partial(_retention_kernel, n_pairs=n_pairs, n_blocks=nb,
                          chunks_per_block=cpb),
        grid=(n_pairs + 1, nb),
        in_specs=[
            pl.BlockSpec(memory_space=pltpu.SMEM),
            pl.BlockSpec((None, block_tokens, VQ_DIM), head_rows(fwd_idx)),
            pl.BlockSpec((QK_DIM, block_tokens), head_major(fwd_idx)),
            pl.BlockSpec((QK_DIM, block_tokens), head_major(bwd_idx)),
            pl.BlockSpec((None, block_tokens, V_DIM), head_rows(bwd_idx)),
        ],
        out_specs=pl.BlockSpec((None, block_tokens, V_DIM), head_rows(fwd_idx)),
        out_shape=jax.ShapeDtypeStruct((RET_HEADS, t, V_DIM), BF16),
        scratch_shapes=[
            pltpu.VMEM((c, c), F32),
            pltpu.VMEM((c, LANES), BF16),
            pltpu.VMEM((c, LANES), BF16),
            pltpu.VMEM((QK_DIM, c), BF16),
            pltpu.VMEM((QK_DIM, c), BF16),
            pltpu.VMEM((QK_DIM, V_DIM), F32),
            pltpu.VMEM((QK_DIM, V_DIM), F32),
            pltpu.VMEM((2 * nc, QK_DIM, V_DIM), BF16),
        ],
        compiler_params=pltpu.CompilerParams(
            dimension_semantics=("arbitrary", "arbitrary"),
            vmem_limit_bytes=VMEM_LIMIT),
        name="retention",
    )(lg, vq, kt, kt, vq)


def _merge_kernel(u_ref, gb_ref, up_ref, un_ref, ga_ref, gr_ref, z_ref, gsw_ref,
                  wconv_ref, wa_ref, wr_ref, o_ref, *, tiles_per_seq):
    i = pl.program_id(0)
    tm = u_ref.shape[0]
    u = u_ref[...].astype(F32)
    pos = i % tiles_per_seq
    last = BF16_SUBLANES - 1
    u_before = up_ref[last:last + 1, :].astype(F32) * (pos != 0).astype(F32)
    u_after = un_ref[0:1, :].astype(F32) * (pos != tiles_per_seq - 1).astype(F32)
    row = lax.broadcasted_iota(jnp.int32, u.shape, 0)
    u_prev = jnp.where(row == 0, u_before, pltpu.roll(u, 1, axis=0))
    u_next = jnp.where(row == tm - 1, u_after, pltpu.roll(u, tm - 1, axis=0))
    conv = u_prev * wconv_ref[0:1, :] + u * wconv_ref[1:2, :] + u_next * wconv_ref[2:3, :]
    a_in = (gb_ref[...].astype(F32) * conv).astype(BF16)

    def branch_outputs(rows):
        z = jnp.concatenate([z_ref[hh, rows, :] for hh in range(RET_HEADS)], axis=1)
        y_r = jnp.dot(z * gsw_ref[rows, :], wr_ref[...], preferred_element_type=F32)
        y_a = jnp.dot(a_in[rows, :], wa_ref[...], preferred_element_type=F32)
        return y_a, y_r

    blocks = [slice(r, r + SUBBLOCK_ROWS) for r in range(0, tm, SUBBLOCK_ROWS)]
    y_next = branch_outputs(blocks[0])
    for s, rows in enumerate(blocks):
        y_a, y_r = y_next
        if s + 1 < len(blocks):
            y_next = branch_outputs(blocks[s + 1])
        merged = ga_ref[rows, :].astype(F32) * y_a + gr_ref[rows, :].astype(F32) * y_r
        o_ref[rows, :] = merged.astype(BF16)


def _merge(proj, z, w_conv, w_a, w_r, seq, tm):
    t, d = proj.shape[0], D_MODEL
    hb = tm // BF16_SUBLANES
    n_halo_blocks = t // BF16_SUBLANES
    halo = (BF16_SUBLANES, SEC)
    return pl.pallas_call(
        functools.partial(_merge_kernel, tiles_per_seq=seq // tm),
        grid=(t // tm,),
        in_specs=[
            pl.BlockSpec((tm, SEC), lambda i: (i, SEC_U)),
            pl.BlockSpec((tm, SEC), lambda i: (i, SEC_GB)),
            pl.BlockSpec(halo, lambda i: (jnp.maximum(i * hb - 1, 0), SEC_U)),
            pl.BlockSpec(halo, lambda i: (jnp.minimum((i + 1) * hb, n_halo_blocks - 1), SEC_U)),
            pl.BlockSpec((tm, SEC), lambda i: (i, SEC_GA)),
            pl.BlockSpec((tm, SEC), lambda i: (i, SEC_GR)),
            pl.BlockSpec((RET_HEADS, tm, V_DIM), lambda i: (0, i, 0)),
            pl.BlockSpec((tm, RET_V), lambda i: (i, (SEC_GSW * SEC) // RET_V)),
            _const_spec(w_conv.shape),
            _const_spec(w_a.shape),
            _const_spec(w_r.shape),
        ],
        out_specs=pl.BlockSpec((tm, d), lambda i: (i, 0)),
        out_shape=jax.ShapeDtypeStruct((t, d), BF16),
        compiler_params=pltpu.CompilerParams(
            dimension_semantics=("parallel",),
            vmem_limit_bytes=VMEM_LIMIT),
        name="merge",
    )(proj, proj, proj, proj, proj, proj, z, proj, w_conv, w_a, w_r)


def _ffn_kernel(x_ref, m_ref, wo_ref, gffn_ref, wg_ref, wu_ref, wd_ref, gfin_ref, o_ref,
                *, final_norm):
    def residual(rows):
        return x_ref[rows, :] + jnp.dot(m_ref[rows, :], wo_ref[...], preferred_element_type=F32)

    def gate_up(x1):
        h2 = (_rms_scale(x1) * gffn_ref[...]).astype(BF16)
        return (jnp.dot(h2, wg_ref[...], preferred_element_type=F32),
                jnp.dot(h2, wu_ref[...], preferred_element_type=F32), x1)

    blocks = [slice(r, r + SUBBLOCK_ROWS) for r in range(0, x_ref.shape[0], SUBBLOCK_ROWS)]
    x1s = [residual(rows) for rows in blocks[:2]]
    gu_next = gate_up(x1s[0])
    for s, rows in enumerate(blocks):
        gate, up, x1 = gu_next
        if s + 2 < len(blocks):
            x1s.append(residual(blocks[s + 2]))
        if s + 1 < len(blocks):
            gu_next = gate_up(x1s[s + 1])
        act = (gate * _sigmoid(gate) * up).astype(BF16)
        y = x1 + jnp.dot(act, wd_ref[...], preferred_element_type=F32)
        o_ref[rows, :] = _rms_scale(y) * gfin_ref[...] if final_norm else y


def _ffn(x2, merged, w_o, g_ffn, w_gate, w_up, w_down, g_final, final_norm, tm):
    t, d = x2.shape
    return pl.pallas_call(
        functools.partial(_ffn_kernel, final_norm=final_norm),
        grid=(t // tm,),
        in_specs=[
            pl.BlockSpec((tm, d), lambda i: (i, 0)),
            pl.BlockSpec((tm, d), lambda i: (i, 0)),
            _const_spec(w_o.shape),
            _const_spec(g_ffn.shape),
            _const_spec(w_gate.shape),
            _const_spec(w_up.shape),
            _const_spec(w_down.shape),
            _const_spec(g_final.shape),
        ],
        out_specs=pl.BlockSpec((tm, d), lambda i: (i, 0)),
        out_shape=jax.ShapeDtypeStruct((t, d), F32),
        compiler_params=pltpu.CompilerParams(
            dimension_semantics=("parallel",),
            vmem_limit_bytes=VMEM_LIMIT),
        name="ffn",
    )(x2, merged, w_o, g_ffn, w_gate, w_up, w_down, g_final)


def _rotary_tables(seq, tm):
    freqs = ROPE_BASE ** (-jnp.arange(0, QK_DIM, 2, dtype=F32) / QK_DIM)
    base = (jnp.arange(seq // tm, dtype=F32) * tm)[:, None] * freqs[None, :]
    off = jnp.arange(tm, dtype=F32)[:, None] * freqs[None, :]
    return jnp.cos(base), jnp.sin(base), jnp.cos(off), jnp.sin(off)


def kernel(x, g_mix, w_in, w_conv, dec_f, dec_b, g_ret, w_a_out, w_r_out, w_o,
           g_ffn, w_ff_gate, w_ff_up, w_ff_down, g_final):
    batch, seq, d = x.shape
    depth = w_in.shape[0]
    assert d == D_MODEL and seq % RET_BLOCK_TOKENS == 0 and seq % MERGE_ROWS == 0
    x2 = x.reshape(batch * seq, d)
    rot = _rotary_tables(seq, INPROJ_ROWS)
    for l in range(depth):
        lg = jnp.stack([jax.nn.log_sigmoid(dec_f[l].astype(F32)),
                        jax.nn.log_sigmoid(dec_b[l].astype(F32))])
        side = [w_a_out[l], w_r_out[l], w_o[l], w_ff_gate[l], w_ff_up[l], w_ff_down[l]]
        proj, kt, vq, (w_a, w_r, w_ob, w_gate, w_up, w_down) = _inproj(
            x2, g_mix[l][None, :], w_in[l], rot,
            g_ret[l][None, :].astype(F32), side, tm=INPROJ_ROWS)
        z = _retention(lg, vq, kt, batch, seq, block_tokens=RET_BLOCK_TOKENS)
        merged = _merge(proj, z, w_conv[l], w_a, w_r, seq, tm=MERGE_ROWS)
        x2 = _ffn(x2, merged, w_ob, g_ffn[l][None, :], w_gate, w_up, w_down,
                  g_final[None, :], final_norm=(l == depth - 1), tm=FFN_ROWS)
    return x2.reshape(batch, seq, d)
```

```python
import functools

import jax
import jax.numpy as jnp
from jax import lax
from jax.experimental import pallas as pl
from jax.experimental.pallas import tpu as pltpu

F32 = jnp.float32
BF16 = jnp.bfloat16

D_MODEL = 1024
RET_HEADS = 4
QK_DIM = D_MODEL // RET_HEADS
V_DIM = 2 * QK_DIM
RET_V = RET_HEADS * V_DIM
ROPE_BASE = 10000.0
EPS = 1e-6

SEC = 1024
W_XC, W_GB, W_GC, W_Q, W_K, W_V, W_GSW, W_GA, W_GR, N_W_SEC = 0, 1, 2, 3, 4, 5, 7, 9, 10, 11
SEC_U, SEC_GB, SEC_GSW, SEC_GA, SEC_GR, N_OUT_SEC = 0, 1, 2, 4, 5, 6
VQ_DIM = V_DIM + QK_DIM

RET_CHUNK = 256
SIDE_CAST_STEPS = 16
SUBBLOCK_ROWS = 256
INPROJ_ROWS = 256
RET_BLOCK_TOKENS = 4096
MERGE_ROWS = 512
FFN_ROWS = 512
LANES = 128
BF16_SUBLANES = 16

VMEM_LIMIT = 56 * 1024 * 1024


def _sigmoid(x):
    return 1.0 / (1.0 + jnp.exp(-x))


def _rms_scale(x):
    return x * lax.rsqrt(jnp.mean(x * x, axis=-1, keepdims=True) + EPS)


def _const_spec(shape):
    return pl.BlockSpec(shape, lambda *_: (0,) * len(shape), pipeline_mode=pl.Buffered(1))


def _inproj_kernel(x_ref, g_ref, w_ref, cos_base_ref, sin_base_ref, cos_off_ref, sin_off_ref,
                   gret_ref, *rest, n_pos_blocks, n_side):
    side_in = rest[:n_side]
    o_ref, kt_ref, vq_ref = rest[n_side:n_side + 3]
    side_out = rest[n_side + 3:2 * n_side + 3]
    w_bf16_ref = rest[2 * n_side + 3]
    step = pl.program_id(0) - N_W_SEC

    @pl.when(step < 0)
    def _():
        w_bf16_ref[pl.program_id(0)] = w_ref[...].astype(BF16)

    @pl.when(step >= 0)
    def _():
        h = (_rms_scale(x_ref[...]) * g_ref[...]).astype(BF16)
        half = QK_DIM // 2
        pos_block = pl.ds(step % n_pos_blocks, 1)
        cos_b = cos_base_ref[pos_block, :]
        sin_b = sin_base_ref[pos_block, :]
        cos = cos_b * cos_off_ref[...] - sin_b * sin_off_ref[...]
        sin = sin_b * cos_off_ref[...] + cos_b * sin_off_ref[...]

        def project(first_sec, n_sec):
            return jnp.concatenate(
                [jnp.dot(h, w_bf16_ref[sec], preferred_element_type=F32)
                 for sec in range(first_sec, first_sec + n_sec)], axis=1)

        def out_cols(sec, n_sec=1):
            return slice(sec * SEC, (sec + n_sec) * SEC)

        qk = project(W_Q, 2)
        for sec in range(2):
            for hh in range(RET_HEADS):
                lo = sec * SEC + hh * QK_DIM
                t1 = qk[:, lo:lo + half]
                t2 = qk[:, lo + half:lo + QK_DIM]
                r1 = t1 * cos - t2 * sin
                r2 = t1 * sin + t2 * cos
                if sec == 0:
                    scale = QK_DIM ** -0.5
                    vq_ref[hh, :, V_DIM:V_DIM + half] = (r1 * scale).astype(BF16)
                    vq_ref[hh, :, V_DIM + half:VQ_DIM] = (r2 * scale).astype(BF16)
                else:
                    out = hh * QK_DIM
                    kt_ref[out:out + half, :] = r1.T.astype(BF16)
                    kt_ref[out + half:out + QK_DIM, :] = r2.T.astype(BF16)

        gsw = project(W_GSW, 2)
        o_ref[:, out_cols(SEC_GSW, 2)] = (gsw * _sigmoid(gsw) * gret_ref[...]).astype(BF16)

        gates = project(W_GA, 2)
        o_ref[:, out_cols(SEC_GA, 2)] = _sigmoid(gates).astype(BF16)

        conv_in = project(W_XC, 3)
        o_ref[:, out_cols(SEC_U)] = (conv_in[:, out_cols(W_GC)] * conv_in[:, out_cols(W_XC)]
                                     ).astype(BF16)
        o_ref[:, out_cols(SEC_GB)] = conv_in[:, out_cols(W_GB)].astype(BF16)

        values = project(W_V, 2)
        for hh in range(RET_HEADS):
            vq_ref[hh, :, :V_DIM] = values[:, hh * V_DIM:(hh + 1) * V_DIM].astype(BF16)

    @pl.when(jnp.logical_and(step >= 0, step < SIDE_CAST_STEPS))
    def _():
        for src, dst in zip(side_in, side_out):
            dst[...] = src[...].astype(BF16)


def _inproj(x2, g_mix, w_in, rot, g_ret, side_weights, tm):
    t, d = x2.shape
    cos_base, sin_base, cos_off, sin_off = rot
    assert cos_off.shape[0] == tm and w_in.shape[1] == N_W_SEC * SEC
    n_out = N_OUT_SEC * SEC
    n_steps = t // tm
    assert n_steps >= SIDE_CAST_STEPS

    def tile(i):
        return jnp.maximum(i - N_W_SEC, 0)

    def slab_spec(w):
        rows = w.shape[0] // SIDE_CAST_STEPS
        assert rows * SIDE_CAST_STEPS == w.shape[0] and rows % BF16_SUBLANES == 0
        return pl.BlockSpec((rows, w.shape[1]),
                            lambda i: (jnp.minimum(tile(i), SIDE_CAST_STEPS - 1), 0))

    side_specs = [slab_spec(w) for w in side_weights]
    outs = pl.pallas_call(
        functools.partial(_inproj_kernel, n_pos_blocks=cos_base.shape[0],
                          n_side=len(side_weights)),
        grid=(N_W_SEC + n_steps,),
        in_specs=[
            pl.BlockSpec((tm, d), lambda i: (tile(i), 0)),
            _const_spec(g_mix.shape),
            pl.BlockSpec((d, SEC), lambda i: (0, jnp.minimum(i, N_W_SEC - 1))),
            _const_spec(cos_base.shape),
            _const_spec(sin_base.shape),
            _const_spec(cos_off.shape),
            _const_spec(sin_off.shape),
            _const_spec(g_ret.shape),
        ] + side_specs,
        out_specs=[
            pl.BlockSpec((tm, n_out), lambda i: (tile(i), 0)),
            pl.BlockSpec((RET_HEADS * QK_DIM, tm), lambda i: (0, tile(i))),
            pl.BlockSpec((RET_HEADS, tm, VQ_DIM), lambda i: (0, tile(i), 0)),
        ] + side_specs,
        out_shape=[
            jax.ShapeDtypeStruct((t, n_out), BF16),
            jax.ShapeDtypeStruct((RET_HEADS * QK_DIM, t), BF16),
            jax.ShapeDtypeStruct((RET_HEADS, t, VQ_DIM), BF16),
        ] + [jax.ShapeDtypeStruct(w.shape, BF16) for w in side_weights],
        scratch_shapes=[pltpu.VMEM((N_W_SEC, d, SEC), BF16)],
        compiler_params=pltpu.CompilerParams(
            dimension_semantics=("arbitrary",),
            vmem_limit_bytes=VMEM_LIMIT),
        name="inproj",
    )(x2, g_mix, w_in, cos_base, sin_base, cos_off, sin_off, g_ret, *side_weights)
    return outs[0], outs[1], outs[2], outs[3:]


def _retention_kernel(lg_ref, vq_ref, ktf_ref, ktb_ref, vb_ref, z_ref,
                      decay_ref, xif_ref, xib_ref, zetaf_ref, zetab_ref,
                      fstate_ref, bstate_ref, rall_ref,
                      *, n_pairs, n_blocks, chunks_per_block):
    c = RET_CHUNK
    cpb = chunks_per_block
    n_chunks = n_blocks * cpb
    p = pl.program_id(0)
    j = pl.program_id(1)
    head_f = jnp.maximum(p - 1, 0) % RET_HEADS
    head_b = jnp.minimum(p, n_pairs - 1) % RET_HEADS
    lg_f = lg_ref[0, head_f]
    lg_fb = lg_ref[1, head_f]
    lg_b = lg_ref[1, head_b]
    has_fwd = p >= 1
    has_bwd = p < n_pairs

    @pl.when(jnp.logical_and(j == 0, has_fwd))
    def _():
        row = lax.broadcasted_iota(jnp.int32, (c, c), 0).astype(F32)
        col = lax.broadcasted_iota(jnp.int32, (c, c), 1).astype(F32)
        diff = row - col
        decay_ref[...] = jnp.where(diff >= 0.0,
                                   jnp.exp(jnp.maximum(diff, 0.0) * lg_f),
                                   jnp.exp(jnp.maximum(-diff, 0.0) * lg_fb))
        zetaf_ref[...] = jnp.exp((c - 1.0 - col) * lg_f).astype(BF16)
        idx = lax.broadcasted_iota(jnp.int32, (c, LANES), 0).astype(F32)
        xif_ref[...] = jnp.exp((idx + 1.0) * lg_f).astype(BF16)
        xib_ref[...] = jnp.exp((c - idx) * lg_fb).astype(BF16)
        fstate_ref[...] = jnp.zeros_like(fstate_ref)

    @pl.when(jnp.logical_and(j == 0, has_bwd))
    def _():
        col = lax.broadcasted_iota(jnp.int32, (c, c), 1).astype(F32)
        zetab_ref[...] = jnp.exp(col * lg_b).astype(BF16)
        bstate_ref[...] = jnp.zeros_like(bstate_ref)

    def row_scaled(a, scale_ref):
        scale = scale_ref[...]
        return jnp.concatenate(
            [a[:, n * LANES:(n + 1) * LANES] * scale
             for n in range(a.shape[1] // LANES)], axis=1)

    def decayed_kv(kt_ref, v_ref, zeta_ref, ci):
        rows = slice(ci * c, (ci + 1) * c)
        return jnp.dot(kt_ref[:, rows] * zeta_ref[...], v_ref[rows, :V_DIM],
                       preferred_element_type=F32)

    def bwd_sweep():
        blk = n_blocks - 1 - j
        slot = (p % 2) * n_chunks
        chunk_decay = jnp.exp(jnp.zeros((1, V_DIM), F32) + c * lg_b)
        state = bstate_ref[...]
        for ci in reversed(range(cpb)):
            rall_ref[slot + blk * cpb + ci] = state.astype(BF16)
            state = chunk_decay * state + decayed_kv(ktb_ref, vb_ref, zetab_ref, ci)
        bstate_ref[...] = state

    def fwd_sweep():
        slot = ((p - 1) % 2) * n_chunks
        chunk_decay = jnp.exp(jnp.zeros((1, V_DIM), F32) + c * lg_f)

        def decayed_scores(ci):
            rows = slice(ci * c, (ci + 1) * c)
            scores = jnp.dot(vq_ref[rows, V_DIM:], ktf_ref[:, rows],
                             preferred_element_type=F32)
            return (scores * decay_ref[...]).astype(BF16)

        state = fstate_ref[...]
        pmat_next = decayed_scores(0)
        for ci in range(cpb):
            rows = slice(ci * c, (ci + 1) * c)
            pmat = pmat_next
            kv = decayed_kv(ktf_ref, vq_ref, zetaf_ref, ci)
            if ci + 1 < cpb:
                pmat_next = decayed_scores(ci + 1)
            q = vq_ref[rows, V_DIM:]
            lhs = jnp.concatenate([pmat, row_scaled(q, xib_ref), row_scaled(q, xif_ref)],
                                  axis=1)
            rhs = jnp.concatenate([vq_ref[rows, :V_DIM], rall_ref[slot + j * cpb + ci],
                                   state.astype(BF16)], axis=0)
            ret = jnp.dot(lhs, rhs, preferred_element_type=F32)
            mu = jnp.mean(ret, axis=-1, keepdims=True)
            var = jnp.mean(jnp.square(ret - mu), axis=-1, keepdims=True)
            rstd = lax.rsqrt(var + EPS)
            z_ref[rows, :] = (ret * rstd - mu * rstd).astype(BF16)
            state = chunk_decay * state + kv
        fstate_ref[...] = state

    @pl.when(jnp.logical_not(has_fwd))
    def _():
        bwd_sweep()

    @pl.when(jnp.logical_and(has_fwd, has_bwd))
    def _():
        fwd_sweep()
        bwd_sweep()

    @pl.when(jnp.logical_not(has_bwd))
    def _():
        fwd_sweep()


def _retention(lg, vq, kt, batch, seq, block_tokens):
    c = RET_CHUNK
    nb = seq // block_tokens
    cpb = block_tokens // c
    nc = seq // c
    t = batch * seq
    n_pairs = batch * RET_HEADS

    def fwd_idx(p, j):
        pair = jnp.maximum(p - 1, 0)
        return pair // RET_HEADS, pair % RET_HEADS, jnp.where(p == 0, 0, j)

    def bwd_idx(p, j):
        pair = jnp.minimum(p, n_pairs - 1)
        return pair // RET_HEADS, pair % RET_HEADS, jnp.where(p == n_pairs, 0, nb - 1 - j)

    def head_rows(idx_fn):
        def index_map(p, j):
            b, h, blk = idx_fn(p, j)
            return h, b * nb + blk, 0
        return index_map

    def head_major(idx_fn):
        def index_map(p, j):
            b, h, blk = idx_fn(p, j)
            return h, b * nb + blk
        return index_map

    return pl.pallas_call(
        functools.partial(_retention_kernel, n_pairs=n_pairs, n_blocks=nb,
                          chunks_per_block=cpb),
        grid=(n_pairs + 1, nb),
        in_specs=[
            pl.BlockSpec(memory_space=pltpu.SMEM),
            pl.BlockSpec((None, block_tokens, VQ_DIM), head_rows(fwd_idx)),
            pl.BlockSpec((QK_DIM, block_tokens), head_major(fwd_idx)),
            pl.BlockSpec((QK_DIM, block_tokens), head_major(bwd_idx)),
            pl.BlockSpec((None, block_tokens, V_DIM), head_rows(bwd_idx)),
        ],
        out_specs=pl.BlockSpec((None, block_tokens, V_DIM), head_rows(fwd_idx)),
        out_shape=jax.ShapeDtypeStruct((RET_HEADS, t, V_DIM), BF16),
        scratch_shapes=[
            pltpu.VMEM((c, c), F32),
            pltpu.VMEM((c, LANES), BF16),
            pltpu.VMEM((c, LANES), BF16),
            pltpu.VMEM((QK_DIM, c), BF16),
            pltpu.VMEM((QK_DIM, c), BF16),
            pltpu.VMEM((QK_DIM, V_DIM), F32),
            pltpu.VMEM((QK_DIM, V_DIM), F32),
            pltpu.VMEM((2 * nc, QK_DIM, V_DIM), BF16),
        ],
        compiler_params=pltpu.CompilerParams(
            dimension_semantics=("arbitrary", "arbitrary"),
            vmem_limit_bytes=VMEM_LIMIT),
        name="retention",
    )(lg, vq, kt, kt, vq)


def _merge_kernel(u_ref, gb_ref, up_ref, un_ref, ga_ref, gr_ref, z_ref, gsw_ref,
                  wconv_ref, wa_ref, wr_ref, o_ref, *, tiles_per_seq):
    i = pl.program_id(0)
    tm = u_ref.shape[0]
    u = u_ref[...].astype(F32)
    pos = i % tiles_per_seq
    last = BF16_SUBLANES - 1
    u_before = up_ref[last:last + 1, :].astype(F32) * (pos != 0).astype(F32)
    u_after = un_ref[0:1, :].astype(F32) * (pos != tiles_per_seq - 1).astype(F32)
    row = lax.broadcasted_iota(jnp.int32, u.shape, 0)
    u_prev = jnp.where(row == 0, u_before, pltpu.roll(u, 1, axis=0))
    u_next = jnp.where(row == tm - 1, u_after, pltpu.roll(u, tm - 1, axis=0))
    conv = u_prev * wconv_ref[0:1, :] + u * wconv_ref[1:2, :] + u_next * wconv_ref[2:3, :]
    a_in = (gb_ref[...].astype(F32) * conv).astype(BF16)

    def branch_outputs(rows):
        z = jnp.concatenate([z_ref[hh, rows, :] for hh in range(RET_HEADS)], axis=1)
        y_r = jnp.dot(z * gsw_ref[rows, :], wr_ref[...], preferred_element_type=F32)
        y_a = jnp.dot(a_in[rows, :], wa_ref[...], preferred_element_type=F32)
        return y_a, y_r

    blocks = [slice(r, r + SUBBLOCK_ROWS) for r in range(0, tm, SUBBLOCK_ROWS)]
    y_next = branch_outputs(blocks[0])
    for s, rows in enumerate(blocks):
        y_a, y_r = y_next
        if s + 1 < len(blocks):
            y_next = branch_outputs(blocks[s + 1])
        merged = ga_ref[rows, :].astype(F32) * y_a + gr_ref[rows, :].astype(F32) * y_r
        o_ref[rows, :] = merged.astype(BF16)


def _merge(proj, z, w_conv, w_a, w_r, seq, tm):
    t, d = proj.shape[0], D_MODEL
    hb = tm // BF16_SUBLANES
    n_halo_blocks = t // BF16_SUBLANES
    halo = (BF16_SUBLANES, SEC)
    return pl.pallas_call(
        functools.partial(_merge_kernel, tiles_per_seq=seq // tm),
        grid=(t // tm,),
        in_specs=[
            pl.BlockSpec((tm, SEC), lambda i: (i, SEC_U)),
            pl.BlockSpec((tm, SEC), lambda i: (i, SEC_GB)),
            pl.BlockSpec(halo, lambda i: (jnp.maximum(i * hb - 1, 0), SEC_U)),
            pl.BlockSpec(halo, lambda i: (jnp.minimum((i + 1) * hb, n_halo_blocks - 1), SEC_U)),
            pl.BlockSpec((tm, SEC), lambda i: (i, SEC_GA)),
            pl.BlockSpec((tm, SEC), lambda i: (i, SEC_GR)),
            pl.BlockSpec((RET_HEADS, tm, V_DIM), lambda i: (0, i, 0)),
            pl.BlockSpec((tm, RET_V), lambda i: (i, (SEC_GSW * SEC) // RET_V)),
            _const_spec(w_conv.shape),
            _const_spec(w_a.shape),
            _const_spec(w_r.shape),
        ],
        out_specs=pl.BlockSpec((tm, d), lambda i: (i, 0)),
        out_shape=jax.ShapeDtypeStruct((t, d), BF16),
        compiler_params=pltpu.CompilerParams(
            dimension_semantics=("arbitrary",),
            vmem_limit_bytes=VMEM_LIMIT),
        name="merge",
    )(proj, proj, proj, proj, proj, proj, z, proj, w_conv, w_a, w_r)


def _ffn_kernel(x_ref, m_ref, wo_ref, gffn_ref, wg_ref, wu_ref, wd_ref, gfin_ref, o_ref,
                *, final_norm):
    def residual(rows):
        return x_ref[rows, :] + jnp.dot(m_ref[rows, :], wo_ref[...], preferred_element_type=F32)

    def gate_up(x1):
        h2 = (_rms_scale(x1) * gffn_ref[...]).astype(BF16)
        return (jnp.dot(h2, wg_ref[...], preferred_element_type=F32),
                jnp.dot(h2, wu_ref[...], preferred_element_type=F32), x1)

    blocks = [slice(r, r + SUBBLOCK_ROWS) for r in range(0, x_ref.shape[0], SUBBLOCK_ROWS)]
    x1s = [residual(rows) for rows in blocks[:2]]
    gu_next = gate_up(x1s[0])
    for s, rows in enumerate(blocks):
        gate, up, x1 = gu_next
        if s + 2 < len(blocks):
            x1s.append(residual(blocks[s + 2]))
        if s + 1 < len(blocks):
            gu_next = gate_up(x1s[s + 1])
        act = (gate * _sigmoid(gate) * up).astype(BF16)
        y = x1 + jnp.dot(act, wd_ref[...], preferred_element_type=F32)
        o_ref[rows, :] = _rms_scale(y) * gfin_ref[...] if final_norm else y


def _ffn(x2, merged, w_o, g_ffn, w_gate, w_up, w_down, g_final, final_norm, tm):
    t, d = x2.shape
    return pl.pallas_call(
        functools.partial(_ffn_kernel, final_norm=final_norm),
        grid=(t // tm,),
        in_specs=[
            pl.BlockSpec((tm, d), lambda i: (i, 0)),
            pl.BlockSpec((tm, d), lambda i: (i, 0)),
            _const_spec(w_o.shape),
            _const_spec(g_ffn.shape),
            _const_spec(w_gate.shape),
            _const_spec(w_up.shape),
            _const_spec(w_down.shape),
            _const_spec(g_final.shape),
        ],
        out_specs=pl.BlockSpec((tm, d), lambda i: (i, 0)),
        out_shape=jax.ShapeDtypeStruct((t, d), F32),
        compiler_params=pltpu.CompilerParams(
            dimension_semantics=("arbitrary",),
            vmem_limit_bytes=VMEM_LIMIT),
        name="ffn",
    )(x2, merged, w_o, g_ffn, w_gate, w_up, w_down, g_final)


def _rotary_tables(seq, tm):
    freqs = ROPE_BASE ** (-jnp.arange(0, QK_DIM, 2, dtype=F32) / QK_DIM)
    base = (jnp.arange(seq // tm, dtype=F32) * tm)[:, None] * freqs[None, :]
    off = jnp.arange(tm, dtype=F32)[:, None] * freqs[None, :]
    return jnp.cos(base), jnp.sin(base), jnp.cos(off), jnp.sin(off)


def kernel(x, g_mix, w_in, w_conv, dec_f, dec_b, g_ret, w_a_out, w_r_out, w_o,
           g_ffn, w_ff_gate, w_ff_up, w_ff_down, g_final):
    batch, seq, d = x.shape
    depth = w_in.shape[0]
    assert d == D_MODEL and seq % RET_BLOCK_TOKENS == 0 and seq % MERGE_ROWS == 0
    x2 = x.reshape(batch * seq, d)
    rot = _rotary_tables(seq, INPROJ_ROWS)
    for l in range(depth):
        lg = jnp.stack([jax.nn.log_sigmoid(dec_f[l].astype(F32)),
                        jax.nn.log_sigmoid(dec_b[l].astype(F32))])
        side = [w_a_out[l], w_r_out[l], w_o[l], w_ff_gate[l], w_ff_up[l], w_ff_down[l]]
        proj, kt, vq, (w_a, w_r, w_ob, w_gate, w_up, w_down) = _inproj(
            x2, g_mix[l][None, :], w_in[l], rot,
            g_ret[l][None, :].astype(F32), side, tm=INPROJ_ROWS)
        z = _retention(lg, vq, kt, batch, seq, block_tokens=RET_BLOCK_TOKENS)
        merged = _merge(proj, z, w_conv[l], w_a, w_r, seq, tm=MERGE_ROWS)
        x2 = _ffn(x2, merged, w_ob, g_ffn[l][None, :], w_gate, w_up, w_down,
                  g_final[None, :], final_norm=(l == depth - 1), tm=FFN_ROWS)
    return x2.reshape(batch, seq, d)
```
